```python
import math
import jax, jax.numpy as jnp
from jax import lax
import numpy as np

D_MODEL = 1024
BATCH = 16
SEQ = 2048
DEPTH = 2

N_META = 16
N_A_LAYERS = DEPTH // 2
N_B_LAYERS = DEPTH - N_A_LAYERS
CONV_WIDTH = 3
DIFF_HEAD_DIM = 64
N_DIFF_HEADS = D_MODEL // (2 * DIFF_HEAD_DIM)
DIFF_WIDTH = N_DIFF_HEADS * 2 * DIFF_HEAD_DIM
Q_BLOCK = 128
N_GROUPS = 4
EXPERTS_PER_GROUP = 4
N_EXPERTS = N_GROUPS * EXPERTS_PER_GROUP
TOP_K_IN_GROUP = 2
D_EXPERT = D_MODEL // 2
RMS_EPS = 1e-6
SUBLN_EPS = 1e-5

kernel_name = 'yoco_shortconv_diffattn_hmoe'


def rmsnorm(x, g, eps=RMS_EPS):
    xf = x.astype(jnp.float32)
    y = xf * lax.rsqrt(jnp.mean(xf * xf, axis=-1, keepdims=True) + eps)
    return (y * g.astype(jnp.float32)).astype(x.dtype)


def short_conv_mixer(xn, w_in, conv_w, w_out):
    bcu = xn @ w_in
    b, c, u = jnp.split(bcu, 3, axis=-1)
    z = c * u
    conv = lax.conv_general_dilated(
        z, conv_w[:, None, :], window_strides=(1,), padding=[(CONV_WIDTH - 1, 0)],
        dimension_numbers=('NWC', 'WIO', 'NWC'), feature_group_count=D_MODEL)
    return (b * conv) @ w_out


def lambda_init_for(layer_idx):
    return 0.8 - 0.6 * math.exp(-0.3 * layer_idx)


def diff_attention(xn, k, v, w_q, lam, subln_g, w_o, lambda_init):
    bsz, seq_len, _ = xn.shape
    lp = k.shape[1]
    n_blocks = lp // Q_BLOCK
    q = (xn @ w_q).reshape(bsz, seq_len, N_DIFF_HEADS, 2, DIFF_HEAD_DIM)
    q = jnp.pad(q, ((0, 0), (0, lp - seq_len), (0, 0), (0, 0), (0, 0)))
    q_blocks = q.reshape(bsz, n_blocks, Q_BLOCK, N_DIFF_HEADS, 2, DIFF_HEAD_DIM).transpose(1, 0, 2, 3, 4, 5)
    lam_f = lam.astype(jnp.float32)
    lambda_full = jnp.exp(jnp.sum(lam_f[0] * lam_f[1])) - jnp.exp(jnp.sum(lam_f[2] * lam_f[3])) + lambda_init
    key_pos = jnp.arange(lp)
    scale = DIFF_HEAD_DIM ** -0.5

    def one_block(args):
        qb, start = args
        s = jnp.einsum('bqhcd,bkhcd->bhcqk', qb, k).astype(jnp.float32) * scale
        q_pos = start + jnp.arange(Q_BLOCK)
        mask = key_pos[None, :] <= q_pos[:, None]
        s = jnp.where(mask, s, -jnp.inf)
        p = jax.nn.softmax(s, axis=-1)
        a = p[:, :, 0] - lambda_full * p[:, :, 1]
        return jnp.einsum('bhqk,bkhe->bqhe', a.astype(v.dtype), v)

    starts = jnp.arange(n_blocks) * Q_BLOCK
    o = lax.map(one_block, (q_blocks, starts))
    o = o.transpose(1, 0, 2, 3, 4).reshape(bsz, lp, N_DIFF_HEADS, 2 * DIFF_HEAD_DIM)[:, :seq_len]
    o = rmsnorm(o, subln_g, SUBLN_EPS) * (1.0 - lambda_init)
    return o.reshape(bsz, seq_len, DIFF_WIDTH) @ w_o


def hier_moe(xn, w_rg, b_rg, w_re, b_re, w_gate, w_up, w_down):
    bsz, seq_len, d = xn.shape
    t = xn.reshape(-1, d)
    lg = (t @ w_rg).astype(jnp.float32) + b_rg.astype(jnp.float32)
    pg = jax.nn.softmax(lg, axis=-1)
    g_idx = jnp.argmax(lg, axis=-1)
    p_sel = jnp.take_along_axis(pg, g_idx[:, None], axis=-1)[:, 0]
    le = ((t @ w_re).astype(jnp.float32) + b_re.astype(jnp.float32)).reshape(-1, N_GROUPS, EXPERTS_PER_GROUP)
    le_sel = jnp.take_along_axis(le, g_idx[:, None, None], axis=1)[:, 0]
    top_v, top_i = lax.top_k(le_sel, TOP_K_IN_GROUP)
    w_k = jax.nn.softmax(top_v, axis=-1) * p_sel[:, None]
    within = jnp.sum(jax.nn.one_hot(top_i, EXPERTS_PER_GROUP, dtype=jnp.float32) * w_k[..., None], axis=1)
    combine = (jax.nn.one_hot(g_idx, N_GROUPS, dtype=jnp.float32)[:, :, None] * within[:, None, :])
    combine = combine.reshape(-1, N_EXPERTS).astype(t.dtype)
    y = jnp.zeros_like(t)
    for e in range(N_EXPERTS):
        h = jax.nn.silu(t @ w_gate[e]) * (t @ w_up[e])
        y = y + (h * combine[:, e:e + 1]) @ w_down[e]
    return y.reshape(bsz, seq_len, d)


def setup_inputs(seed: int = 0) -> dict:
    key = jax.random.key(seed)
    ks = jax.random.split(key, 24)
    f32 = jnp.float32
    nrm = lambda k, shape, s: jax.random.normal(k, shape, f32) * s
    gain = lambda k, shape: 1.0 + 0.02 * jax.random.normal(k, shape, f32)
    D = D_MODEL
    return {
        'x': nrm(ks[0], (BATCH, SEQ, D), 1.0),
        'meta_tokens': nrm(ks[1], (N_META, D), 1.0),
        'a_norm': gain(ks[2], (N_A_LAYERS, D)),
        'a_w_in': nrm(ks[3], (N_A_LAYERS, D, 3 * D), D ** -0.5),
        'a_conv': nrm(ks[4], (N_A_LAYERS, CONV_WIDTH, D), CONV_WIDTH ** -0.5),
        'a_w_out': nrm(ks[5], (N_A_LAYERS, D, D), D ** -0.5),
        'kv_norm': gain(ks[6], (D,)),
        'w_kv': nrm(ks[7], (D, 2 * DIFF_WIDTH), D ** -0.5),
        'b_norm': gain(ks[8], (N_B_LAYERS, D)),
        'b_w_q': nrm(ks[9], (N_B_LAYERS, D, DIFF_WIDTH), D ** -0.5),
        'b_lambda': nrm(ks[10], (N_B_LAYERS, 4, DIFF_HEAD_DIM), 0.1),
        'b_subln': gain(ks[11], (N_B_LAYERS, 2 * DIFF_HEAD_DIM)),
        'b_w_o': nrm(ks[12], (N_B_LAYERS, DIFF_WIDTH, D), DIFF_WIDTH ** -0.5),
        'ffn_norm': gain(ks[13], (DEPTH, D)),
        'router_group_w': nrm(ks[14], (DEPTH, D, N_GROUPS), D ** -0.5),
        'router_group_b': nrm(ks[15], (DEPTH, N_GROUPS), 0.01),
        'router_expert_w': nrm(ks[16], (DEPTH, D, N_EXPERTS), D ** -0.5),
        'router_expert_b': nrm(ks[17], (DEPTH, N_EXPERTS), 0.01),
        'expert_w_gate': nrm(ks[18], (DEPTH, N_EXPERTS, D, D_EXPERT), D ** -0.5),
        'expert_w_up': nrm(ks[19], (DEPTH, N_EXPERTS, D, D_EXPERT), D ** -0.5),
        'expert_w_down': nrm(ks[20], (DEPTH, N_EXPERTS, D_EXPERT, D), D_EXPERT ** -0.5),
        'final_norm': gain(ks[21], (D,)),
    }


def reference(x, meta_tokens, a_norm, a_w_in, a_conv, a_w_out, kv_norm, w_kv, b_norm, b_w_q, b_lambda,
              b_subln, b_w_o, ffn_norm, router_group_w, router_group_b, router_expert_w, router_expert_b,
              expert_w_gate, expert_w_up, expert_w_down, final_norm):
    bsz = x.shape[0]
    meta = jnp.broadcast_to(meta_tokens[None].astype(x.dtype), (bsz, N_META, D_MODEL))
    h = jnp.concatenate([meta, x], axis=1)
    seq_len = h.shape[1]
    lp = ((seq_len + Q_BLOCK - 1) // Q_BLOCK) * Q_BLOCK
    k_shared = None
    v_shared = None
    for layer in range(DEPTH):
        if layer < N_A_LAYERS:
            i = layer
            h = h + short_conv_mixer(rmsnorm(h, a_norm[i]), a_w_in[i], a_conv[i], a_w_out[i])
        else:
            j = layer - N_A_LAYERS
            if j == 0:
                kv = rmsnorm(h, kv_norm) @ w_kv
                k_part, v_part = jnp.split(kv, 2, axis=-1)
                k_shared = k_part.reshape(bsz, seq_len, N_DIFF_HEADS, 2, DIFF_HEAD_DIM)
                v_shared = v_part.reshape(bsz, seq_len, N_DIFF_HEADS, 2 * DIFF_HEAD_DIM)
                k_shared = jnp.pad(k_shared, ((0, 0), (0, lp - seq_len), (0, 0), (0, 0), (0, 0)))
                v_shared = jnp.pad(v_shared, ((0, 0), (0, lp - seq_len), (0, 0), (0, 0)))
            h = h + diff_attention(rmsnorm(h, b_norm[j]), k_shared, v_shared, b_w_q[j], b_lambda[j],
                                   b_subln[j], b_w_o[j], lambda_init_for(layer))
        h = h + hier_moe(rmsnorm(h, ffn_norm[layer]), router_group_w[layer], router_group_b[layer],
                         router_expert_w[layer], router_expert_b[layer], expert_w_gate[layer],
                         expert_w_up[layer], expert_w_down[layer])
    return rmsnorm(h, final_norm)[:, N_META:]
```

```python
import functools
import math

import jax
import jax.numpy as jnp
from jax import lax
from jax.experimental import pallas as pl
from jax.experimental.pallas import tpu as pltpu

F32 = jnp.float32
BF16 = jnp.bfloat16
I32 = jnp.int32
U32 = jnp.uint32

D = 1024
N_META = 16
Q_BLOCK = 128
HEAD_DIM = 64
N_HEADS = D // (2 * HEAD_DIM)
N_GROUPS = 4
EPG = 4
N_EXPERTS = N_GROUPS * EPG
D_EXPERT = D // 2
RMS_EPS = 1e-6
SUBLN_EPS = 1e-5

LANES = 128
SUBLANES = 8
TOK_ROWS = SUBLANES
XCHUNKS = D // LANES

TM = 384
TQ = 768
TMM = 256
TPERM = 768
N_PAIRS = 6
N_BUCKETS = N_GROUPS * N_PAIRS
BUCKET_ROWS = 32
PAIR_LO = (0, 0, 0, 1, 1, 2)
PAIR_HI = (1, 2, 3, 2, 3, 3)
VMEM_LIMIT = 56 * 1024 * 1024


def _lambda_init(layer_idx):
    return 0.8 - 0.6 * math.exp(-0.3 * layer_idx)


def _rms(x, g, eps):
    ms = jnp.mean(x * x, axis=-1, keepdims=True)
    return x * lax.rsqrt(ms + eps) * g


def _load_token_tiled(ref, n_tok, n_chunks):
    return [ref[pl.ds(r, n_tok, stride=TOK_ROWS), :] for r in range(n_chunks)]


def _store_token_tiled(ref, chunks, n_tok):
    for r, c in enumerate(chunks):
        ref[pl.ds(r, n_tok, stride=TOK_ROWS), :] = c


def _route_tail(h, fg_ref, wrt_ref, br_ref, tri_ref, xtt_ref, ri_ref, cnt_ref, run_ref):
    tm = h.shape[0]
    xn = _rms(h, fg_ref[...], RMS_EPS)
    xb = xn.astype(BF16)

    lt = lax.dot_general(wrt_ref[...], xb, (((1,), (1,)), ((), ())), preferred_element_type=F32)
    lt = lt + br_ref[...]
    lg = [lt[k:k + 1, :] for k in range(N_GROUPS)]
    m = jnp.maximum(jnp.maximum(lg[0], lg[1]), jnp.maximum(lg[2], lg[3]))
    gidx = jnp.where(lg[0] == m, 0, jnp.where(lg[1] == m, 1, jnp.where(lg[2] == m, 2, 3))).astype(I32)
    se = jnp.exp(lg[0] - m) + jnp.exp(lg[1] - m) + jnp.exp(lg[2] - m) + jnp.exp(lg[3] - m)
    p_sel = 1.0 / se

    def le_row(g, j):
        r = N_GROUPS + g * EPG + j
        return lt[r:r + 1, :]

    sel = [jnp.where(gidx == 0, le_row(0, j),
                     jnp.where(gidx == 1, le_row(1, j),
                               jnp.where(gidx == 2, le_row(2, j), le_row(3, j)))) for j in range(EPG)]

    def first_argmax(vals):
        v = jnp.maximum(jnp.maximum(vals[0], vals[1]), jnp.maximum(vals[2], vals[3]))
        i = jnp.where(vals[0] == v, 0, jnp.where(vals[1] == v, 1, jnp.where(vals[2] == v, 2, 3))).astype(I32)
        return v, i

    v1, i1 = first_argmax(sel)
    sel2 = [jnp.where(i1 == j, -jnp.inf, sel[j]) for j in range(EPG)]
    v2, i2 = first_argmax(sel2)
    e2 = jnp.exp(v2 - v1)
    den = 1.0 + e2
    w1 = (1.0 / den) * p_sel
    w2 = (e2 / den) * p_sel
    lo = jnp.minimum(i1, i2)
    hi = jnp.maximum(i1, i2)
    first_is_lo = i1 < i2
    w_lo = jnp.where(first_is_lo, w1, w2)
    w_hi = jnp.where(first_is_lo, w2, w1)
    pair = jnp.where(lo == 0, hi - 1, jnp.where(lo == 1, hi + 1, 5))
    bucket = gidx * N_PAIRS + pair

    rows = lax.broadcasted_iota(I32, (BUCKET_ROWS, tm), 0)
    ohf = (rows == bucket).astype(F32)
    cum = jnp.dot(ohf.astype(BF16), tri_ref[...], preferred_element_type=F32)
    run = run_ref[:, 0:1]
    rank = jnp.sum(ohf * (cum - 1.0 + run), axis=0, keepdims=True)
    run_ref[...] = run_ref[...] + jnp.sum(ohf, axis=1, keepdims=True)
    cnt_ref[...] = run_ref[...]

    ri_ref[0:1, :] = bucket
    ri_ref[1:2, :] = rank.astype(I32)
    ri_ref[2:SUBLANES, :] = jnp.zeros((SUBLANES - 2, tm), I32)

    wt = jnp.concatenate([w_lo, w_hi, jnp.zeros((LANES - 2, tm), F32)], axis=0).T

    bits = lax.bitcast_convert_type(xb.astype(F32), U32)
    half = D // 2
    packed = (bits[:, :half] >> 16) | (bits[:, half:] & jnp.uint32(0xFFFF0000))
    n_pk = half // LANES
    chunks = [packed[:, r * LANES:(r + 1) * LANES] for r in range(n_pk)]
    chunks.append(lax.bitcast_convert_type(wt, U32))
    chunks += [jnp.zeros((tm, LANES), U32)] * (TOK_ROWS - n_pk - 1)
    _store_token_tiled(xtt_ref, chunks, tm)


def _route_out_shapes(tp):
    return (jax.ShapeDtypeStruct((tp * TOK_ROWS, LANES), U32),
            jax.ShapeDtypeStruct((SUBLANES, tp), I32),
            jax.ShapeDtypeStruct((BUCKET_ROWS, LANES), F32))


def _route_out_specs():
    return (pl.BlockSpec((TM * TOK_ROWS, LANES), lambda i: (i, 0)),
            pl.BlockSpec((SUBLANES, TM), lambda i: (0, i)),
            pl.BlockSpec((BUCKET_ROWS, LANES), lambda i: (0, 0)))


def _route_in_specs():
    return [pl.BlockSpec((1, D), lambda i: (0, 0)),
            pl.BlockSpec((BUCKET_ROWS, D), lambda i: (0, 0)),
            pl.BlockSpec((BUCKET_ROWS, 1), lambda i: (0, 0)),
            pl.BlockSpec((TM, TM), lambda i: (0, 0))]


def _mixer_a_kernel(tiles_per_seq, h_ref, g_ref, win_ref, conv_ref, wout_ref,
                    fg_ref, wrt_ref, br_ref, tri_ref,
                    h1_ref, xtt_ref, ri_ref, cnt_ref, zs_ref, run_ref):
    i = pl.program_id(0)

    @pl.when(i == 0)
    def _():
        run_ref[...] = jnp.zeros_like(run_ref)

    @pl.when(i % tiles_per_seq == 0)
    def _():
        zs_ref[0:SUBLANES, :] = jnp.zeros((SUBLANES, D), F32)

    h = h_ref[...]
    xn = _rms(h, g_ref[...], RMS_EPS).astype(BF16)
    bcu = jnp.dot(xn, win_ref[...], preferred_element_type=F32)
    z = bcu[:, D:2 * D] * bcu[:, 2 * D:3 * D]
    zs_ref[SUBLANES:SUBLANES + TM, :] = z
    cw = conv_ref[...]
    conv = (cw[0:1, :] * zs_ref[SUBLANES - 2:SUBLANES - 2 + TM, :]
            + cw[1:2, :] * zs_ref[SUBLANES - 1:SUBLANES - 1 + TM, :]
            + cw[2:3, :] * z)
    zs_ref[0:SUBLANES, :] = zs_ref[TM:TM + SUBLANES, :]
    mix = jnp.dot((bcu[:, 0:D] * conv).astype(BF16), wout_ref[...], preferred_element_type=F32)
    h1 = h + mix
    h1_ref[...] = h1
    _route_tail(h1, fg_ref, wrt_ref, br_ref, tri_ref, xtt_ref, ri_ref, cnt_ref, run_ref)


def _mixer_a(h, g, w_in, conv_w, w_out, fg, wrt, br, tri, tiles_per_seq):
    tp = h.shape[0]
    return pl.pallas_call(
        functools.partial(_mixer_a_kernel, tiles_per_seq),
        grid=(tp // TM,),
        in_specs=[pl.BlockSpec((TM, D), lambda i: (i, 0)),
                  pl.BlockSpec((1, D), lambda i: (0, 0)),
                  pl.BlockSpec((D, 3 * D), lambda i: (0, 0)),
                  pl.BlockSpec((3, D), lambda i: (0, 0)),
                  pl.BlockSpec((D, D), lambda i: (0, 0))] + _route_in_specs(),
        out_specs=(pl.BlockSpec((TM, D), lambda i: (i, 0)),) + _route_out_specs(),
        out_shape=(jax.ShapeDtypeStruct((tp, D), F32),) + _route_out_shapes(tp),
        scratch_shapes=[pltpu.VMEM((TM + SUBLANES, D), F32),
                        pltpu.VMEM((BUCKET_ROWS, LANES), F32)],
        compiler_params=pltpu.CompilerParams(dimension_semantics=("arbitrary",),
                                             vmem_limit_bytes=VMEM_LIMIT),
        name="mixer_a",
    )(h, g, w_in, conv_w, w_out, fg, wrt, br, tri)


def _permute_kernel(n_src_steps, a_ref, b_ref, src_ref, dst_ref, zero_ref, sem):
    i = pl.program_id(0)

    def dst_at(k):
        b = pl.multiple_of(b_ref[0, 0, k] * TOK_ROWS, TOK_ROWS)
        return dst_ref.at[pl.ds(b, TOK_ROWS), :]

    def move(k):
        a = pl.multiple_of(a_ref[0, 0, k] * TOK_ROWS, TOK_ROWS)
        return pltpu.make_async_copy(src_ref.at[pl.ds(a, TOK_ROWS), :], dst_at(k), sem)

    def fill(k):
        return pltpu.make_async_copy(zero_ref, dst_at(k), sem)

    def run(copy):
        def start(k, c):
            copy(k).start()
            return c

        def wait(k, c):
            copy(k).wait()
            return c

        lax.fori_loop(0, TPERM, start, 0)
        lax.fori_loop(0, TPERM, wait, 0)

    @pl.when(i < n_src_steps)
    def _():
        run(move)

    @pl.when(i >= n_src_steps)
    def _():
        zero_ref[...] = jnp.zeros_like(zero_ref)
        run(fill)


def _permute(src, a_idx, b_idx, n_dst_tok):
    n_src = a_idx.shape[0]
    n = b_idx.shape[0]
    steps = n // TPERM
    a3 = jnp.concatenate([a_idx, jnp.zeros((n - n_src,), I32)]).reshape(steps, 1, TPERM)
    b3 = b_idx.reshape(steps, 1, TPERM)
    smem_spec = pl.BlockSpec((1, 1, TPERM), lambda i: (i, 0, 0), memory_space=pltpu.SMEM)
    return pl.pallas_call(
        functools.partial(_permute_kernel, n_src // TPERM),
        grid=(steps,),
        in_specs=[smem_spec, smem_spec, pl.BlockSpec(memory_space=pl.ANY)],
        out_specs=pl.BlockSpec(memory_space=pl.ANY),
        out_shape=jax.ShapeDtypeStruct((n_dst_tok * TOK_ROWS, LANES), src.dtype),
        scratch_shapes=[pltpu.VMEM((TOK_ROWS, LANES), src.dtype), pltpu.SemaphoreType.DMA],
        compiler_params=pltpu.CompilerParams(dimension_semantics=("arbitrary",)),
        name="permute_rows",
    )(a3, b3, src)


def _moe_kernel(blk_ref, ea_ref, eb_ref, nrow_ref,
                xs_ref, wga_ref, wua_ref, wda_ref, wgb_ref, wub_ref, wdb_ref, ys_ref):
    j = pl.program_id(0)
    nrow = nrow_ref[j]

    @pl.when(nrow > 0)
    def _():
        n_pk = (D // 2) // LANES
        words = _load_token_tiled(xs_ref, TMM, n_pk + 1)
        lo = [lax.bitcast_convert_type(w << 16, F32) for w in words[:n_pk]]
        hi = [lax.bitcast_convert_type(w & jnp.uint32(0xFFFF0000), F32) for w in words[:n_pk]]
        x = jnp.concatenate(lo + hi, axis=1).astype(BF16)
        wrow = lax.bitcast_convert_type(words[n_pk], F32)

        def expert(wg_ref, wu_ref, wd_ref, c):
            g = jnp.dot(x, wg_ref[0], preferred_element_type=F32)
            u = jnp.dot(x, wu_ref[0], preferred_element_type=F32)
            hmid = (jax.nn.silu(g) * u) * c
            return jnp.dot(hmid.astype(BF16), wd_ref[0], preferred_element_type=F32)

        y = expert(wga_ref, wua_ref, wda_ref, wrow[:, 0:1]) + expert(wgb_ref, wub_ref, wdb_ref, wrow[:, 1:2])
        _store_token_tiled(ys_ref, [y[:, r * LANES:(r + 1) * LANES] for r in range(XCHUNKS)], TMM)

    @pl.when(nrow == 0)
    def _():
        ys_ref[...] = jnp.zeros_like(ys_ref)


def _moe(xs, w_gate, w_up, w_down, blk, ea, eb, nrow):
    n_tiles = blk.shape[0]
    p_tok = xs.shape[0] // TOK_ROWS
    tok_spec = pl.BlockSpec((TMM * TOK_ROWS, LANES), lambda j, blk, ea, eb, nr: (blk[j], 0))

    def wspec(shape, which):
        if which == 0:
            return pl.BlockSpec((1,) + shape, lambda j, blk, ea, eb, nr: (ea[j], 0, 0))
        return pl.BlockSpec((1,) + shape, lambda j, blk, ea, eb, nr: (eb[j], 0, 0))

    grid_spec = pltpu.PrefetchScalarGridSpec(
        num_scalar_prefetch=4,
        grid=(n_tiles,),
        in_specs=[tok_spec,
                  wspec((D, D_EXPERT), 0), wspec((D, D_EXPERT), 0), wspec((D_EXPERT, D), 0),
                  wspec((D, D_EXPERT), 1), wspec((D, D_EXPERT), 1), wspec((D_EXPERT, D), 1)],
        out_specs=pl.BlockSpec((TMM * TOK_ROWS, LANES), lambda j, blk, ea, eb, nr: (j, 0)),
    )
    return pl.pallas_call(
        _moe_kernel,
        grid_spec=grid_spec,
        out_shape=jax.ShapeDtypeStruct((p_tok * TOK_ROWS, LANES), F32),
        compiler_params=pltpu.CompilerParams(dimension_semantics=("arbitrary",),
                                             vmem_limit_bytes=VMEM_LIMIT),
        name="moe_experts",
    )(blk, ea, eb, nrow, xs, w_gate, w_up, w_down, w_gate, w_up, w_down)


def _count_le(ends, v):
    return jnp.sum((ends[None, :] <= v[:, None]).astype(I32), axis=1)


def _moe_layer(xtt, ri, cnt, w_gate, w_up, w_down, tp):
    counts = cnt[:N_BUCKETS, 0].astype(I32)
    ntile = (counts + TMM - 1) // TMM
    tend = jnp.cumsum(ntile)
    tstart = tend - ntile
    total = tend[-1]
    pos = (tstart * TMM)[ri[0]] + ri[1]

    n_tiles = tp // TMM + N_BUCKETS
    j = jnp.arange(n_tiles, dtype=I32)
    blk = jnp.minimum(j, total - 1)
    tb = jnp.minimum(_count_le(tend, blk), N_BUCKETS - 1)
    grp = tb // N_PAIRS
    pair = tb % N_PAIRS
    ea = grp * EPG + jnp.asarray(PAIR_LO, I32)[pair]
    eb = grp * EPG + jnp.asarray(PAIR_HI, I32)[pair]
    nrow = jnp.clip(counts[tb] - (j - tstart[tb]) * TMM, 0, TMM)
    nrow = jnp.where(j < total, nrow, 0).astype(I32)

    npad = ntile * TMM - counts
    pend = jnp.cumsum(npad)
    kk = jnp.arange(n_tiles * TMM - tp, dtype=I32)
    pb = _count_le(pend, kk)
    pbc = jnp.minimum(pb, N_BUCKETS - 1)
    in_bucket = tstart[pbc] * TMM + counts[pbc] + (kk - (pend[pbc] - npad[pbc]))
    pad_slots = jnp.where(pb < N_BUCKETS, in_bucket, total * TMM + (kk - pend[-1]))

    tok = jnp.arange(tp, dtype=I32)
    xs = _permute(xtt, tok, jnp.concatenate([pos, pad_slots]), n_tiles * TMM)
    ys = _moe(xs, w_gate, w_up, w_down, blk, ea, eb, nrow)
    return _permute(ys, pos, tok, tp)


def _attn_proj_kernel(h_ref, ytt_ref, gkv_ref, gq_ref, wkv_ref, wq_ref,
                      h2_ref, q_ref, k_ref, v_ref):
    y = jnp.concatenate(_load_token_tiled(ytt_ref, TM, XCHUNKS), axis=1)
    h2 = h_ref[...] + y
    h2_ref[...] = h2
    ms = jnp.mean(h2 * h2, axis=-1, keepdims=True)
    xhat = h2 * lax.rsqrt(ms + RMS_EPS)
    kv = jnp.dot((xhat * gkv_ref[...]).astype(BF16), wkv_ref[...], preferred_element_type=F32)
    k_ref[...] = kv[:, 0:D].astype(BF16)
    v_ref[...] = kv[:, D:2 * D].astype(BF16)
    q = jnp.dot((xhat * gq_ref[...]).astype(BF16), wq_ref[...], preferred_element_type=F32)
    q_ref[...] = (q * (HEAD_DIM ** -0.5)).astype(BF16)


def _attn_proj(h, ytt, gkv, gq, wkv, wq):
    tp = h.shape[0]
    row = pl.BlockSpec((TM, D), lambda i: (i, 0))
    return pl.pallas_call(
        _attn_proj_kernel,
        grid=(tp // TM,),
        in_specs=[row, pl.BlockSpec((TM * TOK_ROWS, LANES), lambda i: (i, 0)),
                  pl.BlockSpec((1, D), lambda i: (0, 0)), pl.BlockSpec((1, D), lambda i: (0, 0)),
                  pl.BlockSpec((D, 2 * D), lambda i: (0, 0)), pl.BlockSpec((D, D), lambda i: (0, 0))],
        out_specs=(row, row, row, row),
        out_shape=(jax.ShapeDtypeStruct((tp, D), F32),) + (jax.ShapeDtypeStruct((tp, D), BF16),) * 3,
        compiler_params=pltpu.CompilerParams(dimension_semantics=("arbitrary",),
                                             vmem_limit_bytes=VMEM_LIMIT),
        name="attn_proj",
    )(h, ytt, gkv, gq, wkv, wq)


def _attn_kernel(lambda_init, q_ref, k_ref, v_ref, lam_ref, sg_ref, o_ref, m_ref, l_ref, acc_ref):
    qi = pl.program_id(2)
    q = q_ref[...]
    lane = lax.broadcasted_iota(I32, (TQ, 2 * HEAD_DIM), 1)
    zero = jnp.zeros_like(q)
    qc = (jnp.where(lane < HEAD_DIM, q, zero), jnp.where(lane >= HEAD_DIM, q, zero))

    m_ref[...] = jnp.full(m_ref.shape, -jnp.inf, F32)
    l_ref[...] = jnp.zeros(l_ref.shape, F32)
    acc_ref[...] = jnp.zeros(acc_ref.shape, F32)

    def chunk(kc, masked):
        start = pl.multiple_of(kc * TQ, TQ)
        k = k_ref[pl.ds(start, TQ), :]
        v = v_ref[pl.ds(start, TQ), :]
        for c in range(2):
            s = lax.dot_general(qc[c], k, (((1,), (1,)), ((), ())), preferred_element_type=F32)
            if masked:
                r = lax.broadcasted_iota(I32, (TQ, TQ), 0)
                cidx = lax.broadcasted_iota(I32, (TQ, TQ), 1)
                s = jnp.where(cidx <= r, s, -jnp.inf)
            m_old = m_ref[c]
            m_new = jnp.maximum(m_old, jnp.max(s, axis=1, keepdims=True))
            alpha = jnp.exp(m_old - m_new)
            p = jnp.exp(s - m_new)
            l_ref[c] = alpha * l_ref[c] + jnp.sum(p, axis=1, keepdims=True)
            acc_ref[c] = alpha * acc_ref[c] + jnp.dot(p.astype(BF16), v, preferred_element_type=F32)
            m_ref[c] = m_new

    def body(kc, carry):
        chunk(kc, False)
        return carry

    lax.fori_loop(0, qi, body, 0)
    chunk(qi, True)

    lam = lam_ref[...]
    lam_full = (jnp.exp(jnp.sum(lam[0:1, :] * lam[1:2, :], axis=1, keepdims=True))
                - jnp.exp(jnp.sum(lam[2:3, :] * lam[3:4, :], axis=1, keepdims=True)) + lambda_init)
    o = acc_ref[0] / l_ref[0] - lam_full * (acc_ref[1] / l_ref[1])
    o = _rms(o, sg_ref[...], SUBLN_EPS) * (1.0 - lambda_init)
    o_ref[...] = o.astype(BF16)


def _attention(q, k, v, lam, sg, lambda_init, bsz, lpad):
    tp = q.shape[0]
    nq = lpad // TQ
    hw = 2 * HEAD_DIM
    return pl.pallas_call(
        functools.partial(_attn_kernel, lambda_init),
        grid=(bsz, N_HEADS, nq),
        in_specs=[pl.BlockSpec((TQ, hw), lambda b, h, i: (b * nq + i, h)),
                  pl.BlockSpec((lpad, hw), lambda b, h, i: (b, h)),
                  pl.BlockSpec((lpad, hw), lambda b, h, i: (b, h)),
                  pl.BlockSpec((4, HEAD_DIM), lambda b, h, i: (0, 0)),
                  pl.BlockSpec((1, hw), lambda b, h, i: (0, 0))],
        out_specs=pl.BlockSpec((TQ, hw), lambda b, h, i: (b * nq + i, h)),
        out_shape=jax.ShapeDtypeStruct((tp, D), BF16),
        scratch_shapes=[pltpu.VMEM((2, TQ, 1), F32), pltpu.VMEM((2, TQ, 1), F32),
                        pltpu.VMEM((2, TQ, hw), F32)],
        compiler_params=pltpu.CompilerParams(dimension_semantics=("arbitrary",) * 3,
                                             vmem_limit_bytes=VMEM_LIMIT),
        name="diff_attention",
    )(q, k, v, lam, sg)


def _attn_out_kernel(h_ref, o_ref, wo_ref, fg_ref, wrt_ref, br_ref, tri_ref,
                     h3_ref, xtt_ref, ri_ref, cnt_ref, run_ref):
    @pl.when(pl.program_id(0) == 0)
    def _():
        run_ref[...] = jnp.zeros_like(run_ref)

    h3 = h_ref[...] + jnp.dot(o_ref[...], wo_ref[...], preferred_element_type=F32)
    h3_ref[...] = h3
    _route_tail(h3, fg_ref, wrt_ref, br_ref, tri_ref, xtt_ref, ri_ref, cnt_ref, run_ref)


def _attn_out(h, o, wo, fg, wrt, br, tri):
    tp = h.shape[0]
    row = pl.BlockSpec((TM, D), lambda i: (i, 0))
    return pl.pallas_call(
        _attn_out_kernel,
        grid=(tp // TM,),
        in_specs=[row, row, pl.BlockSpec((D, D), lambda i: (0, 0))] + _route_in_specs(),
        out_specs=(row,) + _route_out_specs(),
        out_shape=(jax.ShapeDtypeStruct((tp, D), F32),) + _route_out_shapes(tp),
        scratch_shapes=[pltpu.VMEM((BUCKET_ROWS, LANES), F32)],
        compiler_params=pltpu.CompilerParams(dimension_semantics=("arbitrary",),
                                             vmem_limit_bytes=VMEM_LIMIT),
        name="attn_out",
    )(h, o, wo, fg, wrt, br, tri)


def _final_kernel(h_ref, ytt_ref, g_ref, o_ref):
    y = jnp.concatenate(_load_token_tiled(ytt_ref, TM, XCHUNKS), axis=1)
    o_ref[...] = _rms(h_ref[...] + y, g_ref[...], RMS_EPS)


def _final(h, ytt, g):
    tp = h.shape[0]
    row = pl.BlockSpec((TM, D), lambda i: (i, 0))
    return pl.pallas_call(
        _final_kernel,
        grid=(tp // TM,),
        in_specs=[row, pl.BlockSpec((TM * TOK_ROWS, LANES), lambda i: (i, 0)),
                  pl.BlockSpec((1, D), lambda i: (0, 0))],
        out_specs=row,
        out_shape=jax.ShapeDtypeStruct((tp, D), F32),
        compiler_params=pltpu.CompilerParams(dimension_semantics=("arbitrary",),
                                             vmem_limit_bytes=VMEM_LIMIT),
        name="final_norm",
    )(h, ytt, g)


def _router_params(w_rg, b_rg, w_re, b_re):
    wr = jnp.concatenate([w_rg, w_re], axis=1)
    wrt = jnp.zeros((BUCKET_ROWS, D), F32).at[:N_GROUPS + N_EXPERTS].set(wr.T).astype(BF16)
    br = jnp.zeros((BUCKET_ROWS, 1), F32).at[:N_GROUPS + N_EXPERTS, 0].set(jnp.concatenate([b_rg, b_re]))
    return wrt, br


def kernel(x, meta_tokens, a_norm, a_w_in, a_conv, a_w_out, kv_norm, w_kv, b_norm, b_w_q, b_lambda, b_subln, b_w_o, ffn_norm, router_group_w, router_group_b, router_expert_w, router_expert_b, expert_w_gate, expert_w_up, expert_w_down, final_norm):
    bsz, seq, d = x.shape
    assert d == D and a_norm.shape[0] == 1 and b_norm.shape[0] == 1
    seq_len = N_META + seq
    lpad = -(-seq_len // TQ) * TQ
    tp = bsz * lpad
    assert lpad % TM == 0 and tp % TPERM == 0 and tp % TMM == 0

    meta = jnp.broadcast_to(meta_tokens[None].astype(x.dtype), (bsz, N_META, D))
    h = jnp.concatenate([meta, x, jnp.zeros((bsz, lpad - seq_len, D), x.dtype)], axis=1).reshape(tp, D)
    tri = jnp.triu(jnp.ones((TM, TM), F32)).astype(BF16)
    bf = lambda w: w.astype(BF16)

    wrt, br = _router_params(router_group_w[0], router_group_b[0], router_expert_w[0], router_expert_b[0])
    h, xtt, ri, cnt = _mixer_a(h, a_norm[0][None], bf(a_w_in[0]), a_conv[0], bf(a_w_out[0]),
                               ffn_norm[0][None], wrt, br, tri, lpad // TM)
    ytt = _moe_layer(xtt, ri, cnt, bf(expert_w_gate[0]), bf(expert_w_up[0]), bf(expert_w_down[0]), tp)

    h, q, k, v = _attn_proj(h, ytt, kv_norm[None], b_norm[0][None], bf(w_kv), bf(b_w_q[0]))
    o = _attention(q, k, v, b_lambda[0], b_subln[0][None], _lambda_init(1), bsz, lpad)
    wrt, br = _router_params(router_group_w[1], router_group_b[1], router_expert_w[1], router_expert_b[1])
    h, xtt, ri, cnt = _attn_out(h, o, bf(b_w_o[0]), ffn_norm[1][None], wrt, br, tri)
    ytt = _moe_layer(xtt, ri, cnt, bf(expert_w_gate[1]), bf(expert_w_up[1]), bf(expert_w_down[1]), tp)

    out = _final(h, ytt, final_norm[None])
    return out.reshape(bsz, lpad, D)[:, N_META:seq_len]
```

```python
import functools
import math

import jax
import jax.numpy as jnp
from jax import lax
from jax.experimental import pallas as pl
from jax.experimental.pallas import tpu as pltpu

F32 = jnp.float32
BF16 = jnp.bfloat16
I32 = jnp.int32
U32 = jnp.uint32

D = 1024
N_META = 16
Q_BLOCK = 128
HEAD_DIM = 64
N_HEADS = D // (2 * HEAD_DIM)
N_GROUPS = 4
EPG = 4
N_EXPERTS = N_GROUPS * EPG
D_EXPERT = D // 2
RMS_EPS = 1e-6
SUBLN_EPS = 1e-5

LANES = 128
SUBLANES = 8
TOK_ROWS = SUBLANES
XCHUNKS = D // LANES

TM = 384
SEQ_ALIGN = 768
QB = 512
KB = 512
TMM = 256
TPERM = 768
PERM_UNROLL = 8
N_PAIRS = 6
N_BUCKETS = N_GROUPS * N_PAIRS
BUCKET_ROWS = 32
PAIR_LO = (0, 0, 0, 1, 1, 2)
PAIR_HI = (1, 2, 3, 2, 3, 3)
VMEM_LIMIT = 56 * 1024 * 1024


def _lambda_init(layer_idx):
    return 0.8 - 0.6 * math.exp(-0.3 * layer_idx)


def _rms(x, g, eps):
    ms = jnp.mean(x * x, axis=-1, keepdims=True)
    return x * lax.rsqrt(ms + eps) * g


def _load_token_tiled(ref, n_tok, n_chunks):
    return [ref[pl.ds(r, n_tok, stride=TOK_ROWS), :] for r in range(n_chunks)]


def _store_token_tiled(ref, chunks, n_tok):
    for r, c in enumerate(chunks):
        ref[pl.ds(r, n_tok, stride=TOK_ROWS), :] = c


def _route_tail(h, fg_ref, wrt_ref, br_ref, tri_ref, xtt_ref, ri_ref, cnt_ref, run_ref):
    tm = h.shape[0]
    xn = _rms(h, fg_ref[...], RMS_EPS)
    xb = xn.astype(BF16)

    lt = lax.dot_general(wrt_ref[...], xb, (((1,), (1,)), ((), ())), preferred_element_type=F32)
    lt = lt + br_ref[...]
    lg = [lt[k:k + 1, :] for k in range(N_GROUPS)]
    m = jnp.maximum(jnp.maximum(lg[0], lg[1]), jnp.maximum(lg[2], lg[3]))
    gidx = jnp.where(lg[0] == m, 0, jnp.where(lg[1] == m, 1, jnp.where(lg[2] == m, 2, 3))).astype(I32)
    se = jnp.exp(lg[0] - m) + jnp.exp(lg[1] - m) + jnp.exp(lg[2] - m) + jnp.exp(lg[3] - m)
    p_sel = 1.0 / se

    def le_row(g, j):
        r = N_GROUPS + g * EPG + j
        return lt[r:r + 1, :]

    sel = [jnp.where(gidx == 0, le_row(0, j),
                     jnp.where(gidx == 1, le_row(1, j),
                               jnp.where(gidx == 2, le_row(2, j), le_row(3, j)))) for j in range(EPG)]

    def first_argmax(vals):
        v = jnp.maximum(jnp.maximum(vals[0], vals[1]), jnp.maximum(vals[2], vals[3]))
        i = jnp.where(vals[0] == v, 0, jnp.where(vals[1] == v, 1, jnp.where(vals[2] == v, 2, 3))).astype(I32)
        return v, i

    v1, i1 = first_argmax(sel)
    sel2 = [jnp.where(i1 == j, -jnp.inf, sel[j]) for j in range(EPG)]
    v2, i2 = first_argmax(sel2)
    e2 = jnp.exp(v2 - v1)
    den = 1.0 + e2
    w1 = (1.0 / den) * p_sel
    w2 = (e2 / den) * p_sel
    lo = jnp.minimum(i1, i2)
    hi = jnp.maximum(i1, i2)
    first_is_lo = i1 < i2
    w_lo = jnp.where(first_is_lo, w1, w2)
    w_hi = jnp.where(first_is_lo, w2, w1)
    pair = jnp.where(lo == 0, hi - 1, jnp.where(lo == 1, hi + 1, 5))
    bucket = gidx * N_PAIRS + pair

    rows = lax.broadcasted_iota(I32, (BUCKET_ROWS, tm), 0)
    ohf = (rows == bucket).astype(F32)
    cum = jnp.dot(ohf.astype(BF16), tri_ref[...], preferred_element_type=F32)
    run = run_ref[:, 0:1]
    rank = jnp.sum(ohf * (cum - 1.0 + run), axis=0, keepdims=True)
    run_ref[...] = run_ref[...] + jnp.sum(ohf, axis=1, keepdims=True)
    cnt_ref[...] = run_ref[...]

    ri_ref[0:1, :] = bucket
    ri_ref[1:2, :] = rank.astype(I32)
    ri_ref[2:SUBLANES, :] = jnp.zeros((SUBLANES - 2, tm), I32)

    wt = jnp.concatenate([w_lo, w_hi, jnp.zeros((LANES - 2, tm), F32)], axis=0).T

    bits = lax.bitcast_convert_type(xb.astype(F32), U32)
    half = D // 2
    packed = (bits[:, :half] >> 16) | (bits[:, half:] & jnp.uint32(0xFFFF0000))
    n_pk = half // LANES
    chunks = [packed[:, r * LANES:(r + 1) * LANES] for r in range(n_pk)]
    chunks.append(lax.bitcast_convert_type(wt, U32))
    chunks += [jnp.zeros((tm, LANES), U32)] * (TOK_ROWS - n_pk - 1)
    _store_token_tiled(xtt_ref, chunks, tm)


def _route_out_shapes(tp):
    return (jax.ShapeDtypeStruct((tp * TOK_ROWS, LANES), U32),
            jax.ShapeDtypeStruct((SUBLANES, tp), I32),
            jax.ShapeDtypeStruct((BUCKET_ROWS, LANES), F32))


def _route_out_specs():
    return (pl.BlockSpec((TM * TOK_ROWS, LANES), lambda i: (i, 0)),
            pl.BlockSpec((SUBLANES, TM), lambda i: (0, i)),
            pl.BlockSpec((BUCKET_ROWS, LANES), lambda i: (0, 0)))


def _route_in_specs():
    return [pl.BlockSpec((1, D), lambda i: (0, 0)),
            pl.BlockSpec((BUCKET_ROWS, D), lambda i: (0, 0)),
            pl.BlockSpec((BUCKET_ROWS, 1), lambda i: (0, 0)),
            pl.BlockSpec((TM, TM), lambda i: (0, 0))]


def _mixer_a_kernel(tiles_per_seq, h_ref, g_ref, win_ref, conv_ref, wout_ref,
                    fg_ref, wrt_ref, br_ref, tri_ref,
                    h1_ref, xtt_ref, ri_ref, cnt_ref, zs_ref, run_ref):
    i = pl.program_id(0)

    @pl.when(i == 0)
    def _():
        run_ref[...] = jnp.zeros_like(run_ref)

    @pl.when(i % tiles_per_seq == 0)
    def _():
        zs_ref[0:SUBLANES, :] = jnp.zeros((SUBLANES, D), F32)

    h = h_ref[...]
    xn = _rms(h, g_ref[...], RMS_EPS).astype(BF16)
    bcu = jnp.dot(xn, win_ref[...], preferred_element_type=F32)
    z = bcu[:, D:2 * D] * bcu[:, 2 * D:3 * D]
    zs_ref[SUBLANES:SUBLANES + TM, :] = z
    cw = conv_ref[...]
    conv = (cw[0:1, :] * zs_ref[SUBLANES - 2:SUBLANES - 2 + TM, :]
            + cw[1:2, :] * zs_ref[SUBLANES - 1:SUBLANES - 1 + TM, :]
            + cw[2:3, :] * z)
    zs_ref[0:SUBLANES, :] = zs_ref[TM:TM + SUBLANES, :]
    mix = jnp.dot((bcu[:, 0:D] * conv).astype(BF16), wout_ref[...], preferred_element_type=F32)
    h1 = h + mix
    h1_ref[...] = h1
    _route_tail(h1, fg_ref, wrt_ref, br_ref, tri_ref, xtt_ref, ri_ref, cnt_ref, run_ref)


def _mixer_a(h, g, w_in, conv_w, w_out, fg, wrt, br, tri, tiles_per_seq):
    tp = h.shape[0]
    return pl.pallas_call(
        functools.partial(_mixer_a_kernel, tiles_per_seq),
        grid=(tp // TM,),
        in_specs=[pl.BlockSpec((TM, D), lambda i: (i, 0)),
                  pl.BlockSpec((1, D), lambda i: (0, 0)),
                  pl.BlockSpec((D, 3 * D), lambda i: (0, 0)),
                  pl.BlockSpec((3, D), lambda i: (0, 0)),
                  pl.BlockSpec((D, D), lambda i: (0, 0))] + _route_in_specs(),
        out_specs=(pl.BlockSpec((TM, D), lambda i: (i, 0)),) + _route_out_specs(),
        out_shape=(jax.ShapeDtypeStruct((tp, D), F32),) + _route_out_shapes(tp),
        scratch_shapes=[pltpu.VMEM((TM + SUBLANES, D), F32),
                        pltpu.VMEM((BUCKET_ROWS, LANES), F32)],
        compiler_params=pltpu.CompilerParams(dimension_semantics=("arbitrary",),
                                             vmem_limit_bytes=VMEM_LIMIT),
        name="mixer_a",
    )(h, g, w_in, conv_w, w_out, fg, wrt, br, tri)


def _run_token_copies(copy):
    def start(g, c):
        for u in range(PERM_UNROLL):
            copy(g * PERM_UNROLL + u).start()
        return c

    def wait(g, c):
        for u in range(PERM_UNROLL):
            copy(g * PERM_UNROLL + u).wait()
        return c

    lax.fori_loop(0, TPERM // PERM_UNROLL, start, 0)
    lax.fori_loop(0, TPERM // PERM_UNROLL, wait, 0)


def _tok_rows(ref, t):
    return ref.at[pl.ds(pl.multiple_of(t * TOK_ROWS, TOK_ROWS), TOK_ROWS), :]


def _scatter_kernel(n_src_steps, b_ref, src_ref, dst_ref, zero_ref, sem):
    i = pl.program_id(0)

    @pl.when(i < n_src_steps)
    def _():
        _run_token_copies(lambda k: pltpu.make_async_copy(
            _tok_rows(src_ref, k), _tok_rows(dst_ref, b_ref[0, 0, k]), sem))

    @pl.when(i >= n_src_steps)
    def _():
        zero_ref[...] = jnp.zeros_like(zero_ref)
        _run_token_copies(lambda k: pltpu.make_async_copy(
            zero_ref, _tok_rows(dst_ref, b_ref[0, 0, k]), sem))


def _scatter_tokens(src, b_idx, n_dst_tok):
    n_src_steps = src.shape[0] // (TPERM * TOK_ROWS)
    steps = b_idx.shape[0] // TPERM
    return pl.pallas_call(
        functools.partial(_scatter_kernel, n_src_steps),
        grid=(steps,),
        in_specs=[pl.BlockSpec((1, 1, TPERM), lambda i: (i, 0, 0), memory_space=pltpu.SMEM),
                  pl.BlockSpec((TPERM * TOK_ROWS, LANES), lambda i: (jnp.minimum(i, n_src_steps - 1), 0))],
        out_specs=pl.BlockSpec(memory_space=pl.ANY),
        out_shape=jax.ShapeDtypeStruct((n_dst_tok * TOK_ROWS, LANES), src.dtype),
        scratch_shapes=[pltpu.VMEM((TOK_ROWS, LANES), src.dtype), pltpu.SemaphoreType.DMA],
        compiler_params=pltpu.CompilerParams(dimension_semantics=("arbitrary",)),
        name="scatter_tokens",
    )(b_idx.reshape(steps, 1, TPERM), src)


def _gather_kernel(a_ref, src_ref, dst_ref, sem):
    _run_token_copies(lambda k: pltpu.make_async_copy(
        _tok_rows(src_ref, a_ref[0, 0, k]), _tok_rows(dst_ref, k), sem))


def _gather_tokens(src, a_idx):
    steps = a_idx.shape[0] // TPERM
    return pl.pallas_call(
        _gather_kernel,
        grid=(steps,),
        in_specs=[pl.BlockSpec((1, 1, TPERM), lambda i: (i, 0, 0), memory_space=pltpu.SMEM),
                  pl.BlockSpec(memory_space=pl.ANY)],
        out_specs=pl.BlockSpec((TPERM * TOK_ROWS, LANES), lambda i: (i, 0)),
        out_shape=jax.ShapeDtypeStruct((a_idx.shape[0] * TOK_ROWS, LANES), src.dtype),
        scratch_shapes=[pltpu.SemaphoreType.DMA],
        compiler_params=pltpu.CompilerParams(dimension_semantics=("arbitrary",)),
        name="gather_tokens",
    )(a_idx.reshape(steps, 1, TPERM), src)


def _moe_kernel(blk_ref, ea_ref, eb_ref, nrow_ref,
                xs_ref, wga_ref, wua_ref, wda_ref, wgb_ref, wub_ref, wdb_ref, ys_ref):
    j = pl.program_id(0)
    nrow = nrow_ref[j]

    @pl.when(nrow > 0)
    def _():
        n_pk = (D // 2) // LANES
        words = _load_token_tiled(xs_ref, TMM, n_pk + 1)
        lo = [lax.bitcast_convert_type(w << 16, F32) for w in words[:n_pk]]
        hi = [lax.bitcast_convert_type(w & jnp.uint32(0xFFFF0000), F32) for w in words[:n_pk]]
        x = jnp.concatenate(lo + hi, axis=1).astype(BF16)
        wrow = lax.bitcast_convert_type(words[n_pk], F32)

        def expert(wg_ref, wu_ref, wd_ref, c):
            g = jnp.dot(x, wg_ref[0], preferred_element_type=F32)
            u = jnp.dot(x, wu_ref[0], preferred_element_type=F32)
            hmid = (jax.nn.silu(g) * u) * c
            return jnp.dot(hmid.astype(BF16), wd_ref[0], preferred_element_type=F32)

        y = expert(wga_ref, wua_ref, wda_ref, wrow[:, 0:1]) + expert(wgb_ref, wub_ref, wdb_ref, wrow[:, 1:2])
        _store_token_tiled(ys_ref, [y[:, r * LANES:(r + 1) * LANES] for r in range(XCHUNKS)], TMM)

    @pl.when(nrow == 0)
    def _():
        ys_ref[...] = jnp.zeros_like(ys_ref)


def _moe(xs, w_gate, w_up, w_down, blk, ea, eb, nrow):
    n_tiles = blk.shape[0]
    p_tok = xs.shape[0] // TOK_ROWS
    tok_spec = pl.BlockSpec((TMM * TOK_ROWS, LANES), lambda j, blk, ea, eb, nr: (blk[j], 0))

    def wspec(shape, which):
        if which == 0:
            return pl.BlockSpec((1,) + shape, lambda j, blk, ea, eb, nr: (ea[j], 0, 0))
        return pl.BlockSpec((1,) + shape, lambda j, blk, ea, eb, nr: (eb[j], 0, 0))

    grid_spec = pltpu.PrefetchScalarGridSpec(
        num_scalar_prefetch=4,
        grid=(n_tiles,),
        in_specs=[tok_spec,
                  wspec((D, D_EXPERT), 0), wspec((D, D_EXPERT), 0), wspec((D_EXPERT, D), 0),
                  wspec((D, D_EXPERT), 1), wspec((D, D_EXPERT), 1), wspec((D_EXPERT, D), 1)],
        out_specs=pl.BlockSpec((TMM * TOK_ROWS, LANES), lambda j, blk, ea, eb, nr: (j, 0)),
    )
    return pl.pallas_call(
        _moe_kernel,
        grid_spec=grid_spec,
        out_shape=jax.ShapeDtypeStruct((p_tok * TOK_ROWS, LANES), F32),
        compiler_params=pltpu.CompilerParams(dimension_semantics=("arbitrary",),
                                             vmem_limit_bytes=VMEM_LIMIT),
        name="moe_experts",
    )(blk, ea, eb, nrow, xs, w_gate, w_up, w_down, w_gate, w_up, w_down)


def _count_le(ends, v):
    return jnp.sum((ends[None, :] <= v[:, None]).astype(I32), axis=1)


def _moe_layer(xtt, ri, cnt, w_gate, w_up, w_down, tp):
    counts = cnt[:N_BUCKETS, 0].astype(I32)
    ntile = (counts + TMM - 1) // TMM
    tend = jnp.cumsum(ntile)
    tstart = tend - ntile
    total = tend[-1]
    pos = (tstart * TMM)[ri[0]] + ri[1]

    n_tiles = tp // TMM + N_BUCKETS
    j = jnp.arange(n_tiles, dtype=I32)
    blk = jnp.minimum(j, total - 1)
    tb = jnp.minimum(_count_le(tend, blk), N_BUCKETS - 1)
    grp = tb // N_PAIRS
    pair = tb % N_PAIRS
    ea = grp * EPG + jnp.asarray(PAIR_LO, I32)[pair]
    eb = grp * EPG + jnp.asarray(PAIR_HI, I32)[pair]
    nrow = jnp.clip(counts[tb] - (j - tstart[tb]) * TMM, 0, TMM)
    nrow = jnp.where(j < total, nrow, 0).astype(I32)

    npad = ntile * TMM - counts
    pend = jnp.cumsum(npad)
    kk = jnp.arange(n_tiles * TMM - tp, dtype=I32)
    pb = _count_le(pend, kk)
    pbc = jnp.minimum(pb, N_BUCKETS - 1)
    in_bucket = tstart[pbc] * TMM + counts[pbc] + (kk - (pend[pbc] - npad[pbc]))
    pad_slots = jnp.where(pb < N_BUCKETS, in_bucket, total * TMM + (kk - pend[-1]))

    xs = _scatter_tokens(xtt, jnp.concatenate([pos, pad_slots]), n_tiles * TMM)
    ys = _moe(xs, w_gate, w_up, w_down, blk, ea, eb, nrow)
    return _gather_tokens(ys, pos)


def _attn_proj_kernel(h_ref, ytt_ref, gkv_ref, gq_ref, wk_ref, wvt_ref, wqt_ref,
                      h2_ref, qt_ref, k_ref, vt_ref):
    y = jnp.concatenate(_load_token_tiled(ytt_ref, TM, XCHUNKS), axis=1)
    h2 = h_ref[...] + y
    h2_ref[...] = h2
    ms = jnp.mean(h2 * h2, axis=-1, keepdims=True)
    xhat = h2 * lax.rsqrt(ms + RMS_EPS)
    xkv = (xhat * gkv_ref[...]).astype(BF16)
    xq = (xhat * gq_ref[...]).astype(BF16)
    nt = (((1,), (1,)), ((), ()))
    k_ref[...] = jnp.dot(xkv, wk_ref[...], preferred_element_type=F32).astype(BF16)
    vt_ref[...] = lax.dot_general(wvt_ref[...], xkv, nt, preferred_element_type=F32).astype(BF16)
    qt = lax.dot_general(wqt_ref[...], xq, nt, preferred_element_type=F32)
    qt_ref[...] = (qt * (HEAD_DIM ** -0.5)).astype(BF16)


def _attn_proj(h, ytt, gkv, gq, wk, wvt, wqt):
    tp = h.shape[0]
    row = pl.BlockSpec((TM, D), lambda i: (i, 0))
    col = pl.BlockSpec((D, TM), lambda i: (0, i))
    vec = pl.BlockSpec((1, D), lambda i: (0, 0))
    mat = pl.BlockSpec((D, D), lambda i: (0, 0))
    return pl.pallas_call(
        _attn_proj_kernel,
        grid=(tp // TM,),
        in_specs=[row, pl.BlockSpec((TM * TOK_ROWS, LANES), lambda i: (i, 0)), vec, vec, mat, mat, mat],
        out_specs=(row, col, row, col),
        out_shape=(jax.ShapeDtypeStruct((tp, D), F32), jax.ShapeDtypeStruct((D, tp), BF16),
                   jax.ShapeDtypeStruct((tp, D), BF16), jax.ShapeDtypeStruct((D, tp), BF16)),
        compiler_params=pltpu.CompilerParams(dimension_semantics=("arbitrary",),
                                             vmem_limit_bytes=VMEM_LIMIT),
        name="attn_proj",
    )(h, ytt, gkv, gq, wk, wvt, wqt)


def _attn_kernel(lambda_init, seq_len, qt_ref, k_ref, vt_ref, lam_ref, sg_ref, o_ref, s_ref, acc_ref):
    hw = 2 * HEAD_DIM
    lpad = o_ref.shape[0]
    lam = lam_ref[...]
    lam_full = (jnp.exp(jnp.sum(lam[0:1, :] * lam[1:2, :], axis=1, keepdims=True))
                - jnp.exp(jnp.sum(lam[2:3, :] * lam[3:4, :], axis=1, keepdims=True)) + lambda_init)

    def query_block(q_start, n_prev, qw, kd):
        qt = qt_ref[:, pl.ds(q_start, qw)]
        frow = lax.broadcasted_iota(I32, (hw, qw), 0)
        zero = jnp.zeros_like(qt)
        qc = (jnp.where(frow < HEAD_DIM, qt, zero), jnp.where(frow >= HEAD_DIM, qt, zero))

        def scores(k0, kw, mparts, masked):
            kblk = k_ref[pl.ds(k0, kw), :]
            out = []
            for c in range(2):
                s = jnp.dot(kblk, qc[c], preferred_element_type=F32)
                if masked:
                    krow = lax.broadcasted_iota(I32, (kw, qw), 0)
                    qcol = lax.broadcasted_iota(I32, (kw, qw), 1)
                    s = jnp.where(krow <= qcol, s, -jnp.inf)
                s_ref[c, pl.ds(k0, kw), 0:qw] = s
                out.append(jnp.maximum(mparts[c], jnp.max(s.reshape(kw // SUBLANES, SUBLANES, qw), axis=0)))
            return tuple(out)

        neg = jnp.full((SUBLANES, qw), -jnp.inf, F32)
        mparts = lax.fori_loop(
            0, n_prev, lambda kb, mp: scores(pl.multiple_of(kb * KB, KB), KB, mp, False), (neg, neg))
        mparts = scores(q_start, kd, mparts, True)
        m = [jnp.max(mp, axis=0, keepdims=True) for mp in mparts]

        acc_ref[...] = jnp.zeros(acc_ref.shape, F32)

        def weighted_values(k0, kw, lparts):
            vblk = vt_ref[:, pl.ds(k0, kw)]
            out = []
            for c in range(2):
                p = jnp.exp(s_ref[c, pl.ds(k0, kw), 0:qw] - m[c])
                acc_ref[c, :, 0:qw] += jnp.dot(vblk, p.astype(BF16), preferred_element_type=F32)
                out.append(lparts[c] + jnp.sum(p.reshape(kw // SUBLANES, SUBLANES, qw), axis=0))
            return tuple(out)

        zero8 = jnp.zeros((SUBLANES, qw), F32)
        lparts = lax.fori_loop(
            0, n_prev, lambda kb, lp: weighted_values(pl.multiple_of(kb * KB, KB), KB, lp), (zero8, zero8))
        lparts = weighted_values(q_start, kd, lparts)
        l = [jnp.sum(lp, axis=0, keepdims=True) for lp in lparts]
        ot = acc_ref[0, :, 0:qw] / l[0] - lam_full * (acc_ref[1, :, 0:qw] / l[1])
        ms = jnp.mean(ot * ot, axis=0, keepdims=True)
        y = ot * lax.rsqrt(ms + SUBLN_EPS) * sg_ref[...] * (1.0 - lambda_init)
        o_ref[pl.ds(q_start, qw), :] = y.T.astype(BF16)

    assert QB == KB
    n_full = seq_len // QB

    def body(r, carry):
        query_block(pl.multiple_of(r * QB, QB), r, QB, KB)
        return carry

    lax.fori_loop(0, n_full, body, 0)
    done = n_full * QB
    if seq_len > done:
        assert seq_len - done <= LANES
        query_block(done, n_full, LANES, min(KB, lpad - done))
        done += LANES
    if lpad > done:
        o_ref[done:lpad, :] = jnp.zeros((lpad - done, hw), BF16)


def _attention(qt, k, vt, lam, sg, lambda_init, bsz, lpad, seq_len):
    tp = k.shape[0]
    hw = 2 * HEAD_DIM
    fmaj = pl.BlockSpec((hw, lpad), lambda b, h: (h, b))
    tmaj = pl.BlockSpec((lpad, hw), lambda b, h: (b, h))
    return pl.pallas_call(
        functools.partial(_attn_kernel, lambda_init, seq_len),
        grid=(bsz, N_HEADS),
        in_specs=[fmaj, tmaj, fmaj,
                  pl.BlockSpec((4, HEAD_DIM), lambda b, h: (0, 0)),
                  pl.BlockSpec((hw, 1), lambda b, h: (0, 0))],
        out_specs=tmaj,
        out_shape=jax.ShapeDtypeStruct((tp, D), BF16),
        scratch_shapes=[pltpu.VMEM((2, lpad, QB), F32), pltpu.VMEM((2, hw, QB), F32)],
        compiler_params=pltpu.CompilerParams(dimension_semantics=("arbitrary",) * 2,
                                             vmem_limit_bytes=VMEM_LIMIT),
        name="diff_attention",
    )(qt, k, vt, lam, sg)


def _attn_out_kernel(h_ref, o_ref, wo_ref, fg_ref, wrt_ref, br_ref, tri_ref,
                     h3_ref, xtt_ref, ri_ref, cnt_ref, run_ref):
    @pl.when(pl.program_id(0) == 0)
    def _():
        run_ref[...] = jnp.zeros_like(run_ref)

    h3 = h_ref[...] + jnp.dot(o_ref[...], wo_ref[...], preferred_element_type=F32)
    h3_ref[...] = h3
    _route_tail(h3, fg_ref, wrt_ref, br_ref, tri_ref, xtt_ref, ri_ref, cnt_ref, run_ref)


def _attn_out(h, o, wo, fg, wrt, br, tri):
    tp = h.shape[0]
    row = pl.BlockSpec((TM, D), lambda i: (i, 0))
    return pl.pallas_call(
        _attn_out_kernel,
        grid=(tp // TM,),
        in_specs=[row, row, pl.BlockSpec((D, D), lambda i: (0, 0))] + _route_in_specs(),
        out_specs=(row,) + _route_out_specs(),
        out_shape=(jax.ShapeDtypeStruct((tp, D), F32),) + _route_out_shapes(tp),
        scratch_shapes=[pltpu.VMEM((BUCKET_ROWS, LANES), F32)],
        compiler_params=pltpu.CompilerParams(dimension_semantics=("arbitrary",),
                                             vmem_limit_bytes=VMEM_LIMIT),
        name="attn_out",
    )(h, o, wo, fg, wrt, br, tri)


def _final_kernel(h_ref, ytt_ref, g_ref, o_ref):
    y = jnp.concatenate(_load_token_tiled(ytt_ref, TM, XCHUNKS), axis=1)
    o_ref[...] = _rms(h_ref[...] + y, g_ref[...], RMS_EPS)


def _final(h, ytt, g):
    tp = h.shape[0]
    row = pl.BlockSpec((TM, D), lambda i: (i, 0))
    return pl.pallas_call(
        _final_kernel,
        grid=(tp // TM,),
        in_specs=[row, pl.BlockSpec((TM * TOK_ROWS, LANES), lambda i: (i, 0)),
                  pl.BlockSpec((1, D), lambda i: (0, 0))],
        out_specs=row,
        out_shape=jax.ShapeDtypeStruct((tp, D), F32),
        compiler_params=pltpu.CompilerParams(dimension_semantics=("arbitrary",),
                                             vmem_limit_bytes=VMEM_LIMIT),
        name="final_norm",
    )(h, ytt, g)


def _router_params(w_rg, b_rg, w_re, b_re):
    wr = jnp.concatenate([w_rg, w_re], axis=1)
    wrt = jnp.zeros((BUCKET_ROWS, D), F32).at[:N_GROUPS + N_EXPERTS].set(wr.T).astype(BF16)
    br = jnp.zeros((BUCKET_ROWS, 1), F32).at[:N_GROUPS + N_EXPERTS, 0].set(jnp.concatenate([b_rg, b_re]))
    return wrt, br


def kernel(x, meta_tokens, a_norm, a_w_in, a_conv, a_w_out, kv_norm, w_kv, b_norm, b_w_q, b_lambda, b_subln, b_w_o, ffn_norm, router_group_w, router_group_b, router_expert_w, router_expert_b, expert_w_gate, expert_w_up, expert_w_down, final_norm):
    bsz, seq, d = x.shape
    assert d == D and a_norm.shape[0] == 1 and b_norm.shape[0] == 1
    seq_len = N_META + seq
    lpad = -(-seq_len // SEQ_ALIGN) * SEQ_ALIGN
    tp = bsz * lpad
    assert lpad % TM == 0 and lpad % LANES == 0 and tp % TPERM == 0 and tp % TMM == 0

    meta = jnp.broadcast_to(meta_tokens[None].astype(x.dtype), (bsz, N_META, D))
    h = jnp.concatenate([meta, x, jnp.zeros((bsz, lpad - seq_len, D), x.dtype)], axis=1).reshape(tp, D)
    tri = jnp.triu(jnp.ones((TM, TM), F32)).astype(BF16)
    bf = lambda w: w.astype(BF16)

    wrt, br = _router_params(router_group_w[0], router_group_b[0], router_expert_w[0], router_expert_b[0])
    h, xtt, ri, cnt = _mixer_a(h, a_norm[0][None], bf(a_w_in[0]), a_conv[0], bf(a_w_out[0]),
                               ffn_norm[0][None], wrt, br, tri, lpad // TM)
    ytt = _moe_layer(xtt, ri, cnt, bf(expert_w_gate[0]), bf(expert_w_up[0]), bf(expert_w_down[0]), tp)

    h, qt, k, vt = _attn_proj(h, ytt, kv_norm[None], b_norm[0][None],
                              bf(w_kv[:, :D]), bf(w_kv[:, D:].T), bf(b_w_q[0].T))
    o = _attention(qt, k, vt, b_lambda[0], b_subln[0][:, None], _lambda_init(1), bsz, lpad, seq_len)
    wrt, br = _router_params(router_group_w[1], router_group_b[1], router_expert_w[1], router_expert_b[1])
    h, xtt, ri, cnt = _attn_out(h, o, bf(b_w_o[0]), ffn_norm[1][None], wrt, br, tri)
    ytt = _moe_layer(xtt, ri, cnt, bf(expert_w_gate[1]), bf(expert_w_up[1]), bf(expert_w_down[1]), tp)

    out = _final(h, ytt, final_norm[None])
    return out.reshape(bsz, lpad, D)[:, N_META:seq_len]
```

```python
import functools
import math

import jax
import jax.numpy as jnp
from jax import lax
from jax.experimental import pallas as pl
from jax.experimental.pallas import tpu as pltpu

F32 = jnp.float32
BF16 = jnp.bfloat16
I32 = jnp.int32
U32 = jnp.uint32

D = 1024
N_META = 16
Q_BLOCK = 128
HEAD_DIM = 64
N_HEADS = D // (2 * HEAD_DIM)
N_GROUPS = 4
EPG = 4
N_EXPERTS = N_GROUPS * EPG
D_EXPERT = D // 2
RMS_EPS = 1e-6
SUBLN_EPS = 1e-5
LOG2E = math.log2(math.e)

LANES = 128
SUBLANES = 8
TOK_ROWS = SUBLANES
XCHUNKS = D // LANES

TM = 256
QB = 512
KB = 512
TMM = 256
TPERM = 768
PERM_UNROLL = 8
N_PAIRS = 6
N_BUCKETS = N_GROUPS * N_PAIRS
BUCKET_ROWS = 32
PAIR_LO = (0, 0, 0, 1, 1, 2)
PAIR_HI = (1, 2, 3, 2, 3, 3)
VMEM_LIMIT = 56 * 1024 * 1024


def _lambda_init(layer_idx):
    return 0.8 - 0.6 * math.exp(-0.3 * layer_idx)


def _rms(x, g, eps):
    ms = jnp.mean(x * x, axis=-1, keepdims=True)
    return x * lax.rsqrt(ms + eps) * g


def _load_token_tiled(ref, n_tok, n_chunks):
    return [ref[pl.ds(r, n_tok, stride=TOK_ROWS), :] for r in range(n_chunks)]


def _store_token_tiled(ref, chunks, n_tok):
    for r, c in enumerate(chunks):
        ref[pl.ds(r, n_tok, stride=TOK_ROWS), :] = c


def _route_tail(h, fg_ref, wrt_ref, br_ref, tri_ref, xtt_ref, ri_ref, cnt_ref, run_ref):
    tm = h.shape[0]
    xn = _rms(h, fg_ref[...], RMS_EPS)
    xb = xn.astype(BF16)

    lt = lax.dot_general(wrt_ref[...], xb, (((1,), (1,)), ((), ())), preferred_element_type=F32)
    lt = lt + br_ref[...]
    lg = [lt[k:k + 1, :] for k in range(N_GROUPS)]
    m = jnp.maximum(jnp.maximum(lg[0], lg[1]), jnp.maximum(lg[2], lg[3]))
    gidx = jnp.where(lg[0] == m, 0, jnp.where(lg[1] == m, 1, jnp.where(lg[2] == m, 2, 3))).astype(I32)
    se = jnp.exp(lg[0] - m) + jnp.exp(lg[1] - m) + jnp.exp(lg[2] - m) + jnp.exp(lg[3] - m)
    p_sel = 1.0 / se

    def le_row(g, j):
        r = N_GROUPS + g * EPG + j
        return lt[r:r + 1, :]

    sel = [jnp.where(gidx == 0, le_row(0, j),
                     jnp.where(gidx == 1, le_row(1, j),
                               jnp.where(gidx == 2, le_row(2, j), le_row(3, j)))) for j in range(EPG)]

    def first_argmax(vals):
        v = jnp.maximum(jnp.maximum(vals[0], vals[1]), jnp.maximum(vals[2], vals[3]))
        i = jnp.where(vals[0] == v, 0, jnp.where(vals[1] == v, 1, jnp.where(vals[2] == v, 2, 3))).astype(I32)
        return v, i

    v1, i1 = first_argmax(sel)
    sel2 = [jnp.where(i1 == j, -jnp.inf, sel[j]) for j in range(EPG)]
    v2, i2 = first_argmax(sel2)
    e2 = jnp.exp(v2 - v1)
    den = 1.0 + e2
    w1 = (1.0 / den) * p_sel
    w2 = (e2 / den) * p_sel
    lo = jnp.minimum(i1, i2)
    hi = jnp.maximum(i1, i2)
    first_is_lo = i1 < i2
    w_lo = jnp.where(first_is_lo, w1, w2)
    w_hi = jnp.where(first_is_lo, w2, w1)
    pair = jnp.where(lo == 0, hi - 1, jnp.where(lo == 1, hi + 1, 5))
    bucket = gidx * N_PAIRS + pair

    rows = lax.broadcasted_iota(I32, (BUCKET_ROWS, tm), 0)
    ohf = (rows == bucket).astype(F32)
    cum = jnp.dot(ohf.astype(BF16), tri_ref[...], preferred_element_type=F32)
    run = run_ref[:, 0:1]
    rank = jnp.sum(ohf * (cum - 1.0 + run), axis=0, keepdims=True)
    run_ref[...] = run_ref[...] + jnp.sum(ohf, axis=1, keepdims=True)
    cnt_ref[...] = run_ref[...]

    ri_ref[0:1, :] = bucket
    ri_ref[1:2, :] = rank.astype(I32)
    ri_ref[2:SUBLANES, :] = jnp.zeros((SUBLANES - 2, tm), I32)

    wt = jnp.concatenate([w_lo, w_hi, jnp.zeros((LANES - 2, tm), F32)], axis=0).T

    bits = lax.bitcast_convert_type(xb.astype(F32), U32)
    half = D // 2
    packed = (bits[:, :half] >> 16) | (bits[:, half:] & jnp.uint32(0xFFFF0000))
    n_pk = half // LANES
    chunks = [packed[:, r * LANES:(r + 1) * LANES] for r in range(n_pk)]
    chunks.append(lax.bitcast_convert_type(wt, U32))
    chunks += [jnp.zeros((tm, LANES), U32)] * (TOK_ROWS - n_pk - 1)
    _store_token_tiled(xtt_ref, chunks, tm)


def _route_out_shapes(tp):
    return (jax.ShapeDtypeStruct((tp * TOK_ROWS, LANES), U32),
            jax.ShapeDtypeStruct((SUBLANES, tp), I32),
            jax.ShapeDtypeStruct((BUCKET_ROWS, LANES), F32))


def _route_out_specs():
    return (pl.BlockSpec((TM * TOK_ROWS, LANES), lambda i: (i, 0)),
            pl.BlockSpec((SUBLANES, TM), lambda i: (0, i)),
            pl.BlockSpec((BUCKET_ROWS, LANES), lambda i: (0, 0)))


def _route_in_specs():
    return [pl.BlockSpec((1, D), lambda i: (0, 0)),
            pl.BlockSpec((BUCKET_ROWS, D), lambda i: (0, 0)),
            pl.BlockSpec((BUCKET_ROWS, 1), lambda i: (0, 0)),
            pl.BlockSpec((TM, TM), lambda i: (0, 0))]


def _mixer_a_kernel(x_tiles, n_x_tiles, x_ref, meta_ref, g_ref, win_ref, conv_ref, wout_ref,
                    fg_ref, wrt_ref, br_ref, tri_ref,
                    h1_ref, xtt_ref, ri_ref, cnt_ref, zs_ref, zmeta_ref, run_ref):
    i = pl.program_id(0)

    def conv_inputs(h):
        xn = _rms(h, g_ref[...], RMS_EPS).astype(BF16)
        bcu = jnp.dot(xn, win_ref[...], preferred_element_type=F32)
        return bcu, bcu[:, D:2 * D] * bcu[:, 2 * D:3 * D]

    @pl.when(i == 0)
    def _():
        run_ref[...] = jnp.zeros_like(run_ref)
        _, zm = conv_inputs(meta_ref[...])
        zmeta_ref[...] = zm[N_META - SUBLANES:N_META, :]

    @pl.when(i % x_tiles == 0)
    def _():
        zs_ref[0:SUBLANES, :] = zmeta_ref[...]

    @pl.when(i == n_x_tiles)
    def _():
        zs_ref[0:SUBLANES, :] = jnp.zeros((SUBLANES, D), F32)

    h = jnp.where(i < n_x_tiles, x_ref[...], meta_ref[...])
    bcu, z = conv_inputs(h)
    zs_ref[SUBLANES:SUBLANES + TM, :] = z
    cw = conv_ref[...]
    conv = (cw[0:1, :] * zs_ref[SUBLANES - 2:SUBLANES - 2 + TM, :]
            + cw[1:2, :] * zs_ref[SUBLANES - 1:SUBLANES - 1 + TM, :]
            + cw[2:3, :] * z)
    zs_ref[0:SUBLANES, :] = zs_ref[TM:TM + SUBLANES, :]
    mix = jnp.dot((bcu[:, 0:D] * conv).astype(BF16), wout_ref[...], preferred_element_type=F32)
    h1 = h + mix
    h1_ref[...] = h1
    _route_tail(h1, fg_ref, wrt_ref, br_ref, tri_ref, xtt_ref, ri_ref, cnt_ref, run_ref)


def _mixer_a(x2, metapad, g, w_in, conv_w, w_out, fg, wrt, br, tri, x_tiles):
    n_x_tiles = x2.shape[0] // TM
    tp = (n_x_tiles + 1) * TM
    return pl.pallas_call(
        functools.partial(_mixer_a_kernel, x_tiles, n_x_tiles),
        grid=(tp // TM,),
        in_specs=[pl.BlockSpec((TM, D), lambda i: (jnp.minimum(i, n_x_tiles - 1), 0)),
                  pl.BlockSpec((TM, D), lambda i: (0, 0)),
                  pl.BlockSpec((1, D), lambda i: (0, 0)),
                  pl.BlockSpec((D, 3 * D), lambda i: (0, 0)),
                  pl.BlockSpec((3, D), lambda i: (0, 0)),
                  pl.BlockSpec((D, D), lambda i: (0, 0))] + _route_in_specs(),
        out_specs=(pl.BlockSpec((TM, D), lambda i: (i, 0)),) + _route_out_specs(),
        out_shape=(jax.ShapeDtypeStruct((tp, D), F32),) + _route_out_shapes(tp),
        scratch_shapes=[pltpu.VMEM((TM + SUBLANES, D), F32),
                        pltpu.VMEM((SUBLANES, D), F32),
                        pltpu.VMEM((BUCKET_ROWS, LANES), F32)],
        compiler_params=pltpu.CompilerParams(dimension_semantics=("arbitrary",),
                                             vmem_limit_bytes=VMEM_LIMIT),
        name="mixer_a",
    )(x2, metapad, g, w_in, conv_w, w_out, fg, wrt, br, tri)


def _run_token_copies(copy):
    def start(g, c):
        for u in range(PERM_UNROLL):
            copy(g * PERM_UNROLL + u).start()
        return c

    def wait(g, c):
        for u in range(PERM_UNROLL):
            copy(g * PERM_UNROLL + u).wait()
        return c

    lax.fori_loop(0, TPERM // PERM_UNROLL, start, 0)
    lax.fori_loop(0, TPERM // PERM_UNROLL, wait, 0)


def _tok_rows(ref, t):
    return ref.at[pl.ds(pl.multiple_of(t * TOK_ROWS, TOK_ROWS), TOK_ROWS), :]


def _scatter_kernel(n_src_steps, b_ref, src_ref, dst_ref, zero_ref, sem):
    i = pl.program_id(0)

    @pl.when(i < n_src_steps)
    def _():
        _run_token_copies(lambda k: pltpu.make_async_copy(
            _tok_rows(src_ref, k), _tok_rows(dst_ref, b_ref[0, 0, k]), sem))

    @pl.when(i >= n_src_steps)
    def _():
        zero_ref[...] = jnp.zeros_like(zero_ref)
        _run_token_copies(lambda k: pltpu.make_async_copy(
            zero_ref, _tok_rows(dst_ref, b_ref[0, 0, k]), sem))


def _scatter_tokens(src, b_idx, n_dst_tok):
    n_src_steps = src.shape[0] // (TPERM * TOK_ROWS)
    steps = b_idx.shape[0] // TPERM
    return pl.pallas_call(
        functools.partial(_scatter_kernel, n_src_steps),
        grid=(steps,),
        in_specs=[pl.BlockSpec((1, 1, TPERM), lambda i: (i, 0, 0), memory_space=pltpu.SMEM),
                  pl.BlockSpec((TPERM * TOK_ROWS, LANES), lambda i: (jnp.minimum(i, n_src_steps - 1), 0))],
        out_specs=pl.BlockSpec(memory_space=pl.ANY),
        out_shape=jax.ShapeDtypeStruct((n_dst_tok * TOK_ROWS, LANES), src.dtype),
        scratch_shapes=[pltpu.VMEM((TOK_ROWS, LANES), src.dtype), pltpu.SemaphoreType.DMA],
        compiler_params=pltpu.CompilerParams(dimension_semantics=("arbitrary",)),
        name="scatter_tokens",
    )(b_idx.reshape(steps, 1, TPERM), src)


def _gather_kernel(a_ref, src_ref, dst_ref, sem):
    _run_token_copies(lambda k: pltpu.make_async_copy(
        _tok_rows(src_ref, a_ref[0, 0, k]), _tok_rows(dst_ref, k), sem))


def _gather_tokens(src, a_idx):
    steps = a_idx.shape[0] // TPERM
    return pl.pallas_call(
        _gather_kernel,
        grid=(steps,),
        in_specs=[pl.BlockSpec((1, 1, TPERM), lambda i: (i, 0, 0), memory_space=pltpu.SMEM),
                  pl.BlockSpec(memory_space=pl.ANY)],
        out_specs=pl.BlockSpec((TPERM * TOK_ROWS, LANES), lambda i: (i, 0)),
        out_shape=jax.ShapeDtypeStruct((a_idx.shape[0] * TOK_ROWS, LANES), src.dtype),
        scratch_shapes=[pltpu.SemaphoreType.DMA],
        compiler_params=pltpu.CompilerParams(dimension_semantics=("arbitrary",)),
        name="gather_tokens",
    )(a_idx.reshape(steps, 1, TPERM), src)


def _moe_kernel(blk_ref, ea_ref, eb_ref, nrow_ref,
                xs_ref, wga_ref, wua_ref, wda_ref, wgb_ref, wub_ref, wdb_ref, ys_ref):
    j = pl.program_id(0)
    nrow = nrow_ref[j]

    @pl.when(nrow > 0)
    def _():
        n_pk = (D // 2) // LANES
        words = _load_token_tiled(xs_ref, TMM, n_pk + 1)
        lo = [lax.bitcast_convert_type(w << 16, F32) for w in words[:n_pk]]
        hi = [lax.bitcast_convert_type(w & jnp.uint32(0xFFFF0000), F32) for w in words[:n_pk]]
        x = jnp.concatenate(lo + hi, axis=1).astype(BF16)
        wrow = lax.bitcast_convert_type(words[n_pk], F32)

        def expert(wg_ref, wu_ref, wd_ref, c):
            g = jnp.dot(x, wg_ref[0], preferred_element_type=F32)
            u = jnp.dot(x, wu_ref[0], preferred_element_type=F32)
            hmid = (jax.nn.silu(g) * u) * c
            return jnp.dot(hmid.astype(BF16), wd_ref[0], preferred_element_type=F32)

        y = expert(wga_ref, wua_ref, wda_ref, wrow[:, 0:1]) + expert(wgb_ref, wub_ref, wdb_ref, wrow[:, 1:2])
        _store_token_tiled(ys_ref, [y[:, r * LANES:(r + 1) * LANES] for r in range(XCHUNKS)], TMM)

    @pl.when(nrow == 0)
    def _():
        ys_ref[...] = jnp.zeros_like(ys_ref)


def _moe(xs, w_gate, w_up, w_down, blk, ea, eb, nrow):
    n_tiles = blk.shape[0]
    p_tok = xs.shape[0] // TOK_ROWS
    tok_spec = pl.BlockSpec((TMM * TOK_ROWS, LANES), lambda j, blk, ea, eb, nr: (blk[j], 0))

    def wspec(shape, which):
        if which == 0:
            return pl.BlockSpec((1,) + shape, lambda j, blk, ea, eb, nr: (ea[j], 0, 0))
        return pl.BlockSpec((1,) + shape, lambda j, blk, ea, eb, nr: (eb[j], 0, 0))

    grid_spec = pltpu.PrefetchScalarGridSpec(
        num_scalar_prefetch=4,
        grid=(n_tiles,),
        in_specs=[tok_spec,
                  wspec((D, D_EXPERT), 0), wspec((D, D_EXPERT), 0), wspec((D_EXPERT, D), 0),
                  wspec((D, D_EXPERT), 1), wspec((D, D_EXPERT), 1), wspec((D_EXPERT, D), 1)],
        out_specs=pl.BlockSpec((TMM * TOK_ROWS, LANES), lambda j, blk, ea, eb, nr: (j, 0)),
    )
    return pl.pallas_call(
        _moe_kernel,
        grid_spec=grid_spec,
        out_shape=jax.ShapeDtypeStruct((p_tok * TOK_ROWS, LANES), F32),
        compiler_params=pltpu.CompilerParams(dimension_semantics=("arbitrary",),
                                             vmem_limit_bytes=VMEM_LIMIT),
        name="moe_experts",
    )(blk, ea, eb, nrow, xs, w_gate, w_up, w_down, w_gate, w_up, w_down)


def _count_le(ends, v):
    return jnp.sum((ends[None, :] <= v[:, None]).astype(I32), axis=1)


def _moe_layer(xtt, ri, cnt, w_gate, w_up, w_down, tp):
    counts = cnt[:N_BUCKETS, 0].astype(I32)
    ntile = (counts + TMM - 1) // TMM
    tend = jnp.cumsum(ntile)
    tstart = tend - ntile
    total = tend[-1]
    pos = (tstart * TMM)[ri[0]] + ri[1]

    n_tiles = tp // TMM + N_BUCKETS
    j = jnp.arange(n_tiles, dtype=I32)
    blk = jnp.minimum(j, total - 1)
    tb = jnp.minimum(_count_le(tend, blk), N_BUCKETS - 1)
    grp = tb // N_PAIRS
    pair = tb % N_PAIRS
    ea = grp * EPG + jnp.asarray(PAIR_LO, I32)[pair]
    eb = grp * EPG + jnp.asarray(PAIR_HI, I32)[pair]
    nrow = jnp.clip(counts[tb] - (j - tstart[tb]) * TMM, 0, TMM)
    nrow = jnp.where(j < total, nrow, 0).astype(I32)

    npad = ntile * TMM - counts
    pend = jnp.cumsum(npad)
    kk = jnp.arange(n_tiles * TMM - tp, dtype=I32)
    pb = _count_le(pend, kk)
    pbc = jnp.minimum(pb, N_BUCKETS - 1)
    in_bucket = tstart[pbc] * TMM + counts[pbc] + (kk - (pend[pbc] - npad[pbc]))
    pad_slots = jnp.where(pb < N_BUCKETS, in_bucket, total * TMM + (kk - pend[-1]))

    xs = _scatter_tokens(xtt, jnp.concatenate([pos, pad_slots]), n_tiles * TMM)
    ys = _moe(xs, w_gate, w_up, w_down, blk, ea, eb, nrow)
    return _gather_tokens(ys, pos)


def _attn_proj_kernel(h_ref, ytt_ref, gkv_ref, gq_ref, wk_ref, wvt_ref, wqt_ref,
                      h2_ref, qt_ref, k_ref, vt_ref):
    y = jnp.concatenate(_load_token_tiled(ytt_ref, TM, XCHUNKS), axis=1)
    h2 = h_ref[...] + y
    h2_ref[...] = h2
    ms = jnp.mean(h2 * h2, axis=-1, keepdims=True)
    xhat = h2 * lax.rsqrt(ms + RMS_EPS)
    xkv = (xhat * gkv_ref[...]).astype(BF16)
    xq = (xhat * gq_ref[...]).astype(BF16)
    nt = (((1,), (1,)), ((), ()))
    k_ref[...] = jnp.dot(xkv, wk_ref[...], preferred_element_type=F32).astype(BF16)
    vt_ref[...] = lax.dot_general(wvt_ref[...], xkv, nt, preferred_element_type=F32).astype(BF16)
    qt = lax.dot_general(wqt_ref[...], xq, nt, preferred_element_type=F32)
    qt_ref[...] = (qt * (LOG2E * HEAD_DIM ** -0.5)).astype(BF16)


def _attn_proj(h, ytt, gkv, gq, wk, wvt, wqt):
    tp = h.shape[0]
    row = pl.BlockSpec((TM, D), lambda i: (i, 0))
    col = pl.BlockSpec((D, TM), lambda i: (0, i))
    vec = pl.BlockSpec((1, D), lambda i: (0, 0))
    mat = pl.BlockSpec((D, D), lambda i: (0, 0))
    return pl.pallas_call(
        _attn_proj_kernel,
        grid=(tp // TM,),
        in_specs=[row, pl.BlockSpec((TM * TOK_ROWS, LANES), lambda i: (i, 0)), vec, vec, mat, mat, mat],
        out_specs=(row, col, row, col),
        out_shape=(jax.ShapeDtypeStruct((tp, D), F32), jax.ShapeDtypeStruct((D, tp), BF16),
                   jax.ShapeDtypeStruct((tp, D), BF16), jax.ShapeDtypeStruct((D, tp), BF16)),
        compiler_params=pltpu.CompilerParams(dimension_semantics=("arbitrary",),
                                             vmem_limit_bytes=VMEM_LIMIT),
        name="attn_proj",
    )(h, ytt, gkv, gq, wk, wvt, wqt)


def _attn_kernel(lambda_init, qt_ref, k_ref, km_ref, vt_ref, vtm_ref, lam_ref, sg_ref, o_ref,
                 s_ref, m_ref, l_ref, acc_ref):
    hw = 2 * HEAD_DIM
    n_x = o_ref.shape[0]
    half = QB // 2
    assert QB == KB and n_x % QB == 0 and N_META <= LANES
    lam = lam_ref[...]
    lam_full = (jnp.exp(jnp.sum(lam[0:1, :] * lam[1:2, :], axis=1, keepdims=True))
                - jnp.exp(jnp.sum(lam[2:3, :] * lam[3:4, :], axis=1, keepdims=True)) + lambda_init)

    def query_block(q_start, n_prev, qw, diag):
        qt = qt_ref[:, pl.ds(q_start, qw)]
        frow = lax.broadcasted_iota(I32, (hw, qw), 0)
        zero = jnp.zeros_like(qt)
        qc = (jnp.where(frow < HEAD_DIM, qt, zero), jnp.where(frow >= HEAD_DIM, qt, zero))
        m_ref[...] = jnp.full(m_ref.shape, -jnp.inf, F32)
        l_ref[...] = jnp.zeros(l_ref.shape, F32)
        acc_ref[...] = jnp.zeros(acc_ref.shape, F32)

        def scores(k0, kw, qo, qn, visible, meta=False):
            kblk = km_ref[0:kw, :] if meta else k_ref[pl.ds(k0, kw), :]
            k0 = n_x if meta else k0
            for c in range(2):
                s = jnp.dot(kblk, qc[c][:, qo:qo + qn], preferred_element_type=F32)
                if visible is not None:
                    krow = lax.broadcasted_iota(I32, (kw, qn), 0)
                    qcol = lax.broadcasted_iota(I32, (kw, qn), 1)
                    s = jnp.where(visible(krow, qcol), s, -jnp.inf)
                s_ref[c, pl.ds(k0, kw), qo:qo + qn] = s
                smax = jnp.max(s.reshape(kw // SUBLANES, SUBLANES, qn), axis=0)
                m_ref[c, :, qo:qo + qn] = jnp.maximum(m_ref[c, :, qo:qo + qn], smax)

        def diag_and_meta(fn):
            for ko, kw, qo, qn in diag:
                fn(q_start + ko, kw, qo, qn, lambda krow, qcol, d=ko - qo: krow + d <= qcol)
            fn(0, LANES, 0, qw, lambda krow, qcol: krow < N_META, meta=True)

        for kb in range(n_prev):
            scores(kb * KB, KB, 0, qw, None)
        diag_and_meta(scores)
        m = [jnp.max(m_ref[c, :, 0:qw], axis=0, keepdims=True) for c in range(2)]

        def weighted_values(k0, kw, qo, qn, visible, meta=False):
            del visible
            vblk = vtm_ref[:, 0:kw] if meta else vt_ref[:, pl.ds(k0, kw)]
            k0 = n_x if meta else k0
            for c in range(2):
                p = jnp.exp2(s_ref[c, pl.ds(k0, kw), qo:qo + qn] - m[c][:, qo:qo + qn])
                acc_ref[c, :, qo:qo + qn] += jnp.dot(vblk, p.astype(BF16), preferred_element_type=F32)
                l_ref[c, :, qo:qo + qn] += jnp.sum(p.reshape(kw // SUBLANES, SUBLANES, qn), axis=0)

        for kb in range(n_prev):
            weighted_values(kb * KB, KB, 0, qw, None)
        diag_and_meta(weighted_values)
        l = [jnp.sum(l_ref[c, :, 0:qw], axis=0, keepdims=True) for c in range(2)]
        ot = acc_ref[0, :, 0:qw] / l[0] - lam_full * (acc_ref[1, :, 0:qw] / l[1])
        ms = jnp.mean(ot * ot, axis=0, keepdims=True)
        y = ot * lax.rsqrt(ms + SUBLN_EPS) * sg_ref[...] * (1.0 - lambda_init)
        o_ref[pl.ds(q_start, qw), :] = y.T.astype(BF16)

    x_diag = ((0, half, 0, QB), (half, half, half, half))

    for r in range(n_x // QB):
        query_block(r * QB, r, QB, x_diag)


def _attention(qt, k, vt, lam, sg, lambda_init, bsz, n_x):
    hw = 2 * HEAD_DIM
    meta_blk = bsz * n_x // TM
    fmaj = pl.BlockSpec((hw, n_x), lambda b, h: (h, b))
    tmaj = pl.BlockSpec((n_x, hw), lambda b, h: (b, h))
    return pl.pallas_call(
        functools.partial(_attn_kernel, lambda_init),
        grid=(bsz, N_HEADS),
        in_specs=[fmaj, tmaj, pl.BlockSpec((TM, hw), lambda b, h: (meta_blk, h)),
                  fmaj, pl.BlockSpec((hw, TM), lambda b, h: (h, meta_blk)),
                  pl.BlockSpec((4, HEAD_DIM), lambda b, h: (0, 0)),
                  pl.BlockSpec((hw, 1), lambda b, h: (0, 0))],
        out_specs=tmaj,
        out_shape=jax.ShapeDtypeStruct((bsz * n_x, D), BF16),
        scratch_shapes=[pltpu.VMEM((2, n_x + LANES, QB), F32), pltpu.VMEM((2, SUBLANES, QB), F32),
                        pltpu.VMEM((2, SUBLANES, QB), F32), pltpu.VMEM((2, hw, QB), F32)],
        compiler_params=pltpu.CompilerParams(dimension_semantics=("arbitrary",) * 2,
                                             vmem_limit_bytes=VMEM_LIMIT),
        name="diff_attention",
    )(qt, k, k, vt, vt, lam, sg)


def _attn_out_kernel(h_ref, o_ref, wo_ref, fg_ref, wrt_ref, br_ref, tri_ref,
                     h3_ref, xtt_ref, ri_ref, cnt_ref, run_ref):
    @pl.when(pl.program_id(0) == 0)
    def _():
        run_ref[...] = jnp.zeros_like(run_ref)

    h3 = h_ref[...] + jnp.dot(o_ref[...], wo_ref[...], preferred_element_type=F32)
    h3_ref[...] = h3
    _route_tail(h3, fg_ref, wrt_ref, br_ref, tri_ref, xtt_ref, ri_ref, cnt_ref, run_ref)


def _attn_out(h, o, wo, fg, wrt, br, tri):
    tp = h.shape[0]
    last_o = o.shape[0] // TM - 1
    row = pl.BlockSpec((TM, D), lambda i: (i, 0))
    return pl.pallas_call(
        _attn_out_kernel,
        grid=(tp // TM,),
        in_specs=[row, pl.BlockSpec((TM, D), lambda i: (jnp.minimum(i, last_o), 0)),
                  pl.BlockSpec((D, D), lambda i: (0, 0))] + _route_in_specs(),
        out_specs=(row,) + _route_out_specs(),
        out_shape=(jax.ShapeDtypeStruct((tp, D), F32),) + _route_out_shapes(tp),
        scratch_shapes=[pltpu.VMEM((BUCKET_ROWS, LANES), F32)],
        compiler_params=pltpu.CompilerParams(dimension_semantics=("arbitrary",),
                                             vmem_limit_bytes=VMEM_LIMIT),
        name="attn_out",
    )(h, o, wo, fg, wrt, br, tri)


def _final_kernel(h_ref, ytt_ref, g_ref, o_ref):
    y = jnp.concatenate(_load_token_tiled(ytt_ref, TM, XCHUNKS), axis=1)
    o_ref[...] = _rms(h_ref[...] + y, g_ref[...], RMS_EPS)


def _final(h, ytt, g, n_x_tiles):
    row = pl.BlockSpec((TM, D), lambda i: (i, 0))
    return pl.pallas_call(
        _final_kernel,
        grid=(n_x_tiles,),
        in_specs=[row, pl.BlockSpec((TM * TOK_ROWS, LANES), lambda i: (i, 0)),
                  pl.BlockSpec((1, D), lambda i: (0, 0))],
        out_specs=row,
        out_shape=jax.ShapeDtypeStruct((n_x_tiles * TM, D), F32),
        compiler_params=pltpu.CompilerParams(dimension_semantics=("arbitrary",),
                                             vmem_limit_bytes=VMEM_LIMIT),
        name="final_norm",
    )(h, ytt, g)


def _router_params(w_rg, b_rg, w_re, b_re):
    wr = jnp.concatenate([w_rg, w_re], axis=1)
    wrt = jnp.zeros((BUCKET_ROWS, D), F32).at[:N_GROUPS + N_EXPERTS].set(wr.T).astype(BF16)
    br = jnp.zeros((BUCKET_ROWS, 1), F32).at[:N_GROUPS + N_EXPERTS, 0].set(jnp.concatenate([b_rg, b_re]))
    return wrt, br


def kernel(x, meta_tokens, a_norm, a_w_in, a_conv, a_w_out, kv_norm, w_kv, b_norm, b_w_q, b_lambda, b_subln, b_w_o, ffn_norm, router_group_w, router_group_b, router_expert_w, router_expert_b, expert_w_gate, expert_w_up, expert_w_down, final_norm):
    bsz, seq, d = x.shape
    assert d == D and a_norm.shape[0] == 1 and b_norm.shape[0] == 1
    assert meta_tokens.shape[0] == N_META and seq % TM == 0
    x_tiles = seq // TM
    n_x_tiles = bsz * x_tiles
    tp = (n_x_tiles + 1) * TM
    assert tp % TPERM == 0 and tp % TMM == 0

    metapad = jnp.concatenate([meta_tokens.astype(x.dtype), jnp.zeros((TM - N_META, D), x.dtype)])
    tri = jnp.triu(jnp.ones((TM, TM), F32)).astype(BF16)
    bf = lambda w: w.astype(BF16)

    wrt, br = _router_params(router_group_w[0], router_group_b[0], router_expert_w[0], router_expert_b[0])
    h, xtt, ri, cnt = _mixer_a(x.reshape(bsz * seq, D), metapad, a_norm[0][None], bf(a_w_in[0]), a_conv[0],
                               bf(a_w_out[0]), ffn_norm[0][None], wrt, br, tri, x_tiles)
    ytt = _moe_layer(xtt, ri, cnt, bf(expert_w_gate[0]), bf(expert_w_up[0]), bf(expert_w_down[0]), tp)

    h, qt, k, vt = _attn_proj(h, ytt, kv_norm[None], b_norm[0][None],
                              bf(w_kv[:, :D]), bf(w_kv[:, D:].T), bf(b_w_q[0].T))
    o = _attention(qt, k, vt, b_lambda[0], b_subln[0][:, None], _lambda_init(1), bsz, seq)
    wrt, br = _router_params(router_group_w[1], router_group_b[1], router_expert_w[1], router_expert_b[1])
    h, xtt, ri, cnt = _attn_out(h, o, bf(b_w_o[0]), ffn_norm[1][None], wrt, br, tri)
    ytt = _moe_layer(xtt, ri, cnt, bf(expert_w_gate[1]), bf(expert_w_up[1]), bf(expert_w_down[1]), tp)

    out = _final(h, ytt, final_norm[None], n_x_tiles)
    return out.reshape(bsz, seq, D)
```

```python
import functools
import math

import jax
import jax.numpy as jnp
from jax import lax
from jax.experimental import pallas as pl
from jax.experimental.pallas import tpu as pltpu

F32 = jnp.float32
BF16 = jnp.bfloat16
I32 = jnp.int32
U32 = jnp.uint32

D = 1024
N_META = 16
Q_BLOCK = 128
HEAD_DIM = 64
N_HEADS = D // (2 * HEAD_DIM)
N_GROUPS = 4
EPG = 4
N_EXPERTS = N_GROUPS * EPG
D_EXPERT = D // 2
RMS_EPS = 1e-6
SUBLN_EPS = 1e-5
LOG2E = math.log2(math.e)

LANES = 128
SUBLANES = 8
TOK_ROWS = SUBLANES
XCHUNKS = D // LANES

TM = 256
QB = 512
KB = 512
TMM = 256
TPERM = 768
PERM_UNROLL = 8
N_PAIRS = 6
N_BUCKETS = N_GROUPS * N_PAIRS
BUCKET_ROWS = 32
PAIR_LO = (0, 0, 0, 1, 1, 2)
PAIR_HI = (1, 2, 3, 2, 3, 3)
VMEM_LIMIT = 56 * 1024 * 1024


def _lambda_init(layer_idx):
    return 0.8 - 0.6 * math.exp(-0.3 * layer_idx)


def _rms(x, g, eps):
    ms = jnp.mean(x * x, axis=-1, keepdims=True)
    return x * lax.rsqrt(ms + eps) * g


def _load_token_tiled(ref, n_tok, n_chunks):
    return [ref[pl.ds(r, n_tok, stride=TOK_ROWS), :] for r in range(n_chunks)]


def _store_token_tiled(ref, chunks, n_tok):
    for r, c in enumerate(chunks):
        ref[pl.ds(r, n_tok, stride=TOK_ROWS), :] = c


def _route_tail(h, fg_ref, wrt_ref, br_ref, tri_ref, xtt_ref, ri_ref, cnt_ref, run_ref):
    tm = h.shape[0]
    xn = _rms(h, fg_ref[...], RMS_EPS)
    xb = xn.astype(BF16)

    lt = lax.dot_general(wrt_ref[...], xb, (((1,), (1,)), ((), ())), preferred_element_type=F32)
    lt = lt + br_ref[...]
    lg = [lt[k:k + 1, :] for k in range(N_GROUPS)]
    m = jnp.maximum(jnp.maximum(lg[0], lg[1]), jnp.maximum(lg[2], lg[3]))
    gidx = jnp.where(lg[0] == m, 0, jnp.where(lg[1] == m, 1, jnp.where(lg[2] == m, 2, 3))).astype(I32)
    se = jnp.exp(lg[0] - m) + jnp.exp(lg[1] - m) + jnp.exp(lg[2] - m) + jnp.exp(lg[3] - m)
    p_sel = 1.0 / se

    def le_row(g, j):
        r = N_GROUPS + g * EPG + j
        return lt[r:r + 1, :]

    sel = [jnp.where(gidx == 0, le_row(0, j),
                     jnp.where(gidx == 1, le_row(1, j),
                               jnp.where(gidx == 2, le_row(2, j), le_row(3, j)))) for j in range(EPG)]

    def first_argmax(vals):
        v = jnp.maximum(jnp.maximum(vals[0], vals[1]), jnp.maximum(vals[2], vals[3]))
        i = jnp.where(vals[0] == v, 0, jnp.where(vals[1] == v, 1, jnp.where(vals[2] == v, 2, 3))).astype(I32)
        return v, i

    v1, i1 = first_argmax(sel)
    sel2 = [jnp.where(i1 == j, -jnp.inf, sel[j]) for j in range(EPG)]
    v2, i2 = first_argmax(sel2)
    e2 = jnp.exp(v2 - v1)
    den = 1.0 + e2
    w1 = (1.0 / den) * p_sel
    w2 = (e2 / den) * p_sel
    lo = jnp.minimum(i1, i2)
    hi = jnp.maximum(i1, i2)
    first_is_lo = i1 < i2
    w_lo = jnp.where(first_is_lo, w1, w2)
    w_hi = jnp.where(first_is_lo, w2, w1)
    pair = jnp.where(lo == 0, hi - 1, jnp.where(lo == 1, hi + 1, 5))
    bucket = gidx * N_PAIRS + pair

    rows = lax.broadcasted_iota(I32, (BUCKET_ROWS, tm), 0)
    ohf = (rows == bucket).astype(F32)
    cum = jnp.dot(ohf.astype(BF16), tri_ref[...], preferred_element_type=F32)
    run = run_ref[:, 0:1]
    rank = jnp.sum(ohf * (cum - 1.0 + run), axis=0, keepdims=True)
    run_ref[...] = run_ref[...] + jnp.sum(ohf, axis=1, keepdims=True)
    cnt_ref[...] = run_ref[...]

    ri_ref[0:1, :] = bucket
    ri_ref[1:2, :] = rank.astype(I32)
    ri_ref[2:SUBLANES, :] = jnp.zeros((SUBLANES - 2, tm), I32)

    wt = jnp.concatenate([w_lo, w_hi, jnp.zeros((LANES - 2, tm), F32)], axis=0).T

    bits = lax.bitcast_convert_type(xb.astype(F32), U32)
    half = D // 2
    packed = (bits[:, :half] >> 16) | (bits[:, half:] & jnp.uint32(0xFFFF0000))
    n_pk = half // LANES
    chunks = [packed[:, r * LANES:(r + 1) * LANES] for r in range(n_pk)]
    chunks.append(lax.bitcast_convert_type(wt, U32))
    chunks += [jnp.zeros((tm, LANES), U32)] * (TOK_ROWS - n_pk - 1)
    _store_token_tiled(xtt_ref, chunks, tm)


def _route_out_shapes(tp):
    return (jax.ShapeDtypeStruct((tp * TOK_ROWS, LANES), U32),
            jax.ShapeDtypeStruct((SUBLANES, tp), I32),
            jax.ShapeDtypeStruct((BUCKET_ROWS, LANES), F32))


def _route_out_specs():
    return (pl.BlockSpec((TM * TOK_ROWS, LANES), lambda i: (i, 0)),
            pl.BlockSpec((SUBLANES, TM), lambda i: (0, i)),
            pl.BlockSpec((BUCKET_ROWS, LANES), lambda i: (0, 0)))


def _route_in_specs():
    return [pl.BlockSpec((1, D), lambda i: (0, 0)),
            pl.BlockSpec((BUCKET_ROWS, D), lambda i: (0, 0)),
            pl.BlockSpec((BUCKET_ROWS, 1), lambda i: (0, 0)),
            pl.BlockSpec((TM, TM), lambda i: (0, 0))]


def _mixer_a_kernel(x_tiles, n_x_tiles, x_ref, meta_ref, g_ref, win_ref, conv_ref, wout_ref,
                    fg_ref, wrt_ref, br_ref, tri_ref,
                    h1_ref, xtt_ref, ri_ref, cnt_ref, zs_ref, zmeta_ref, run_ref):
    i = pl.program_id(0)

    def conv_inputs(h):
        xn = _rms(h, g_ref[...], RMS_EPS).astype(BF16)
        bcu = jnp.dot(xn, win_ref[...], preferred_element_type=F32)
        return bcu, bcu[:, D:2 * D] * bcu[:, 2 * D:3 * D]

    @pl.when(i == 0)
    def _():
        run_ref[...] = jnp.zeros_like(run_ref)
        _, zm = conv_inputs(meta_ref[...])
        zmeta_ref[...] = zm[N_META - SUBLANES:N_META, :]

    @pl.when(i % x_tiles == 0)
    def _():
        zs_ref[0:SUBLANES, :] = zmeta_ref[...]

    @pl.when(i == n_x_tiles)
    def _():
        zs_ref[0:SUBLANES, :] = jnp.zeros((SUBLANES, D), F32)

    h = jnp.where(i < n_x_tiles, x_ref[...], meta_ref[...])
    bcu, z = conv_inputs(h)
    zs_ref[SUBLANES:SUBLANES + TM, :] = z
    cw = conv_ref[...]
    conv = (cw[0:1, :] * zs_ref[SUBLANES - 2:SUBLANES - 2 + TM, :]
            + cw[1:2, :] * zs_ref[SUBLANES - 1:SUBLANES - 1 + TM, :]
            + cw[2:3, :] * z)
    zs_ref[0:SUBLANES, :] = zs_ref[TM:TM + SUBLANES, :]
    mix = jnp.dot((bcu[:, 0:D] * conv).astype(BF16), wout_ref[...], preferred_element_type=F32)
    h1 = h + mix
    h1_ref[...] = h1
    _route_tail(h1, fg_ref, wrt_ref, br_ref, tri_ref, xtt_ref, ri_ref, cnt_ref, run_ref)


def _mixer_a(x2, metapad, g, w_in, conv_w, w_out, fg, wrt, br, tri, x_tiles):
    n_x_tiles = x2.shape[0] // TM
    tp = (n_x_tiles + 1) * TM
    return pl.pallas_call(
        functools.partial(_mixer_a_kernel, x_tiles, n_x_tiles),
        grid=(tp // TM,),
        in_specs=[pl.BlockSpec((TM, D), lambda i: (jnp.minimum(i, n_x_tiles - 1), 0)),
                  pl.BlockSpec((TM, D), lambda i: (0, 0)),
                  pl.BlockSpec((1, D), lambda i: (0, 0)),
                  pl.BlockSpec((D, 3 * D), lambda i: (0, 0)),
                  pl.BlockSpec((3, D), lambda i: (0, 0)),
                  pl.BlockSpec((D, D), lambda i: (0, 0))] + _route_in_specs(),
        out_specs=(pl.BlockSpec((TM, D), lambda i: (i, 0)),) + _route_out_specs(),
        out_shape=(jax.ShapeDtypeStruct((tp, D), F32),) + _route_out_shapes(tp),
        scratch_shapes=[pltpu.VMEM((TM + SUBLANES, D), F32),
                        pltpu.VMEM((SUBLANES, D), F32),
                        pltpu.VMEM((BUCKET_ROWS, LANES), F32)],
        compiler_params=pltpu.CompilerParams(dimension_semantics=("arbitrary",),
                                             vmem_limit_bytes=VMEM_LIMIT),
        name="mixer_a",
    )(x2, metapad, g, w_in, conv_w, w_out, fg, wrt, br, tri)


def _run_token_copies(copy):
    def start(g, c):
        for u in range(PERM_UNROLL):
            copy(g * PERM_UNROLL + u).start()
        return c

    def wait(g, c):
        for u in range(PERM_UNROLL):
            copy(g * PERM_UNROLL + u).wait()
        return c

    lax.fori_loop(0, TPERM // PERM_UNROLL, start, 0)
    lax.fori_loop(0, TPERM // PERM_UNROLL, wait, 0)


def _tok_rows(ref, t):
    return ref.at[pl.ds(pl.multiple_of(t * TOK_ROWS, TOK_ROWS), TOK_ROWS), :]


def _scatter_kernel(n_src_steps, b_ref, src_ref, dst_ref, zero_ref, sem):
    i = pl.program_id(0)

    @pl.when(i < n_src_steps)
    def _():
        _run_token_copies(lambda k: pltpu.make_async_copy(
            _tok_rows(src_ref, k), _tok_rows(dst_ref, b_ref[0, 0, k]), sem))

    @pl.when(i >= n_src_steps)
    def _():
        zero_ref[...] = jnp.zeros_like(zero_ref)
        _run_token_copies(lambda k: pltpu.make_async_copy(
            zero_ref, _tok_rows(dst_ref, b_ref[0, 0, k]), sem))


def _scatter_tokens(src, b_idx, n_dst_tok):
    n_src_steps = src.shape[0] // (TPERM * TOK_ROWS)
    steps = b_idx.shape[0] // TPERM
    return pl.pallas_call(
        functools.partial(_scatter_kernel, n_src_steps),
        grid=(steps,),
        in_specs=[pl.BlockSpec((1, 1, TPERM), lambda i: (i, 0, 0), memory_space=pltpu.SMEM),
                  pl.BlockSpec((TPERM * TOK_ROWS, LANES), lambda i: (jnp.minimum(i, n_src_steps - 1), 0))],
        out_specs=pl.BlockSpec(memory_space=pl.ANY),
        out_shape=jax.ShapeDtypeStruct((n_dst_tok * TOK_ROWS, LANES), src.dtype),
        scratch_shapes=[pltpu.VMEM((TOK_ROWS, LANES), src.dtype), pltpu.SemaphoreType.DMA],
        compiler_params=pltpu.CompilerParams(dimension_semantics=("arbitrary",)),
        name="scatter_tokens",
    )(b_idx.reshape(steps, 1, TPERM), src)


def _moe_kernel(blk_ref, ea_ref, eb_ref, nrow_ref, newa_ref, newb_ref,
                xs_ref, wga_f32, wua_f32, wda_f32, wgb_f32, wub_f32, wdb_f32, ys_ref,
                wga_ref, wua_ref, wda_ref, wgb_ref, wub_ref, wdb_ref):
    j = pl.program_id(0)
    nrow = nrow_ref[j]

    @pl.when(newa_ref[j] == 1)
    def _():
        for dst, src in ((wga_ref, wga_f32), (wua_ref, wua_f32), (wda_ref, wda_f32)):
            dst[...] = src[...].astype(BF16)

    @pl.when(newb_ref[j] == 1)
    def _():
        for dst, src in ((wgb_ref, wgb_f32), (wub_ref, wub_f32), (wdb_ref, wdb_f32)):
            dst[...] = src[...].astype(BF16)

    @pl.when(nrow > 0)
    def _():
        n_pk = (D // 2) // LANES
        words = _load_token_tiled(xs_ref, TMM, n_pk + 1)
        lo = [lax.bitcast_convert_type(w << 16, F32) for w in words[:n_pk]]
        hi = [lax.bitcast_convert_type(w & jnp.uint32(0xFFFF0000), F32) for w in words[:n_pk]]
        x = jnp.concatenate(lo + hi, axis=1).astype(BF16)
        wrow = lax.bitcast_convert_type(words[n_pk], F32)

        def expert(wg_ref, wu_ref, wd_ref, c):
            g = jnp.dot(x, wg_ref[...], preferred_element_type=F32)
            u = jnp.dot(x, wu_ref[...], preferred_element_type=F32)
            hmid = (jax.nn.silu(g) * u) * c
            return jnp.dot(hmid.astype(BF16), wd_ref[...], preferred_element_type=F32)

        y = expert(wga_ref, wua_ref, wda_ref, wrow[:, 0:1]) + expert(wgb_ref, wub_ref, wdb_ref, wrow[:, 1:2])
        _store_token_tiled(ys_ref, [y[:, r * LANES:(r + 1) * LANES] for r in range(XCHUNKS)], TMM)

    @pl.when(nrow == 0)
    def _():
        ys_ref[...] = jnp.zeros_like(ys_ref)


def _moe(xs, w_gate, w_up, w_down, blk, ea, eb, nrow):
    n_tiles = blk.shape[0]
    p_tok = xs.shape[0] // TOK_ROWS
    first = jnp.ones((1,), I32)
    newa = jnp.concatenate([first, (ea[1:] != ea[:-1]).astype(I32)])
    newb = jnp.concatenate([first, (eb[1:] != eb[:-1]).astype(I32)])
    tok_spec = pl.BlockSpec((TMM * TOK_ROWS, LANES), lambda j, blk, *_: (blk[j], 0))

    def wspec(shape, which):
        if which == 0:
            return pl.BlockSpec((None,) + shape, lambda j, blk, ea, eb, *_: (ea[j], 0, 0))
        return pl.BlockSpec((None,) + shape, lambda j, blk, ea, eb, *_: (eb[j], 0, 0))

    w_in, w_out = (D, D_EXPERT), (D_EXPERT, D)
    grid_spec = pltpu.PrefetchScalarGridSpec(
        num_scalar_prefetch=6,
        grid=(n_tiles,),
        in_specs=[tok_spec, wspec(w_in, 0), wspec(w_in, 0), wspec(w_out, 0),
                  wspec(w_in, 1), wspec(w_in, 1), wspec(w_out, 1)],
        out_specs=pl.BlockSpec((TMM * TOK_ROWS, LANES), lambda j, *_: (j, 0)),
        scratch_shapes=[pltpu.VMEM(w_in, BF16), pltpu.VMEM(w_in, BF16), pltpu.VMEM(w_out, BF16)] * 2,
    )
    return pl.pallas_call(
        _moe_kernel,
        grid_spec=grid_spec,
        out_shape=jax.ShapeDtypeStruct((p_tok * TOK_ROWS, LANES), F32),
        compiler_params=pltpu.CompilerParams(dimension_semantics=("arbitrary",),
                                             vmem_limit_bytes=VMEM_LIMIT),
        name="moe_experts",
    )(blk, ea, eb, nrow, newa, newb, xs, w_gate, w_up, w_down, w_gate, w_up, w_down)


def _count_le(ends, v):
    return jnp.sum((ends[None, :] <= v[:, None]).astype(I32), axis=1)


def _moe_layer(xtt, ri, cnt, w_gate, w_up, w_down, tp):
    counts = cnt[:N_BUCKETS, 0].astype(I32)
    ntile = (counts + TMM - 1) // TMM
    tend = jnp.cumsum(ntile)
    tstart = tend - ntile
    total = tend[-1]
    pos = (tstart * TMM)[ri[0]] + ri[1]

    n_tiles = tp // TMM + N_BUCKETS
    j = jnp.arange(n_tiles, dtype=I32)
    blk = jnp.minimum(j, total - 1)
    tb = jnp.minimum(_count_le(tend, blk), N_BUCKETS - 1)
    grp = tb // N_PAIRS
    pair = tb % N_PAIRS
    ea = grp * EPG + jnp.asarray(PAIR_LO, I32)[pair]
    eb = grp * EPG + jnp.asarray(PAIR_HI, I32)[pair]
    nrow = jnp.clip(counts[tb] - (j - tstart[tb]) * TMM, 0, TMM)
    nrow = jnp.where(j < total, nrow, 0).astype(I32)

    npad = ntile * TMM - counts
    pend = jnp.cumsum(npad)
    kk = jnp.arange(n_tiles * TMM - tp, dtype=I32)
    pb = _count_le(pend, kk)
    pbc = jnp.minimum(pb, N_BUCKETS - 1)
    in_bucket = tstart[pbc] * TMM + counts[pbc] + (kk - (pend[pbc] - npad[pbc]))
    pad_slots = jnp.where(pb < N_BUCKETS, in_bucket, total * TMM + (kk - pend[-1]))

    xs = _scatter_tokens(xtt, jnp.concatenate([pos, pad_slots]), n_tiles * TMM)
    ys = _moe(xs, w_gate, w_up, w_down, blk, ea, eb, nrow)
    return ys, pos.reshape(tp // TM, 1, TM)


def _gather_expert_rows(pos_ref, nxt_ref, ys_ref, buf_ref, sem):
    i = pl.program_id(0)
    slot = i % 2

    def copies(idx_ref, s):
        return [pltpu.make_async_copy(_tok_rows(ys_ref, idx_ref[0, 0, k]),
                                      buf_ref.at[s, pl.ds(k * TOK_ROWS, TOK_ROWS), :], sem.at[s])
                for k in range(TM)]

    @pl.when(i == 0)
    def _():
        for c in copies(pos_ref, 0):
            c.start()

    for c in copies(pos_ref, slot):
        c.wait()
    y = jnp.concatenate([buf_ref[slot, pl.ds(r, TM, stride=TOK_ROWS), :] for r in range(XCHUNKS)], axis=1)
    for c in copies(nxt_ref, 1 - slot):
        c.start()

    def drain():
        @pl.when(i == pl.num_programs(0) - 1)
        def _():
            for c in copies(nxt_ref, 1 - slot):
                c.wait()

    return y, drain


def _gather_specs(n_steps):
    return [pl.BlockSpec((1, 1, TM), lambda i: (i, 0, 0), memory_space=pltpu.SMEM),
            pl.BlockSpec((1, 1, TM), lambda i: (jnp.minimum(i + 1, n_steps - 1), 0, 0),
                         memory_space=pltpu.SMEM),
            pl.BlockSpec(memory_space=pl.ANY)]


GATHER_SCRATCH = [pltpu.VMEM((2, TM * TOK_ROWS, LANES), F32), pltpu.SemaphoreType.DMA((2,))]


def _attn_proj_kernel(h_ref, pos_ref, nxt_ref, ys_ref, gkv_ref, gq_ref, wk_ref, wvt_ref, wqt_ref,
                      h2_ref, qt_ref, k_ref, vt_ref, buf_ref, sem):
    y, drain = _gather_expert_rows(pos_ref, nxt_ref, ys_ref, buf_ref, sem)
    h2 = h_ref[...] + y
    h2_ref[...] = h2
    ms = jnp.mean(h2 * h2, axis=-1, keepdims=True)
    xhat = h2 * lax.rsqrt(ms + RMS_EPS)
    xkv = (xhat * gkv_ref[...]).astype(BF16)
    xq = (xhat * gq_ref[...]).astype(BF16)
    nt = (((1,), (1,)), ((), ()))
    k_ref[...] = jnp.dot(xkv, wk_ref[...], preferred_element_type=F32).astype(BF16)
    vt_ref[...] = lax.dot_general(wvt_ref[...], xkv, nt, preferred_element_type=F32).astype(BF16)
    qt = lax.dot_general(wqt_ref[...], xq, nt, preferred_element_type=F32)
    qt_ref[...] = (qt * (LOG2E * HEAD_DIM ** -0.5)).astype(BF16)
    drain()


def _attn_proj(h, pos3, ys, gkv, gq, wk, wvt, wqt):
    tp = h.shape[0]
    steps = tp // TM
    row = pl.BlockSpec((TM, D), lambda i: (i, 0))
    col = pl.BlockSpec((D, TM), lambda i: (0, i))
    vec = pl.BlockSpec((1, D), lambda i: (0, 0))
    mat = pl.BlockSpec((D, D), lambda i: (0, 0))
    return pl.pallas_call(
        _attn_proj_kernel,
        grid=(steps,),
        in_specs=[row] + _gather_specs(steps) + [vec, vec, mat, mat, mat],
        out_specs=(row, col, row, col),
        out_shape=(jax.ShapeDtypeStruct((tp, D), F32), jax.ShapeDtypeStruct((D, tp), BF16),
                   jax.ShapeDtypeStruct((tp, D), BF16), jax.ShapeDtypeStruct((D, tp), BF16)),
        scratch_shapes=GATHER_SCRATCH,
        compiler_params=pltpu.CompilerParams(dimension_semantics=("arbitrary",),
                                             vmem_limit_bytes=VMEM_LIMIT),
        name="attn_proj",
    )(h, pos3, pos3, ys, gkv, gq, wk, wvt, wqt)


def _attn_kernel(lambda_init, qt_ref, k_ref, km_ref, vt_ref, vtm_ref, lam_ref, sg_ref, o_ref,
                 s_ref, m_ref, l_ref, acc_ref):
    hw = 2 * HEAD_DIM
    n_x = o_ref.shape[0]
    half = QB // 2
    assert QB == KB and n_x % QB == 0 and N_META <= LANES
    lam = lam_ref[...]
    lam_full = (jnp.exp(jnp.sum(lam[0:1, :] * lam[1:2, :], axis=1, keepdims=True))
                - jnp.exp(jnp.sum(lam[2:3, :] * lam[3:4, :], axis=1, keepdims=True)) + lambda_init)

    def query_block(q_start, n_prev, qw, diag):
        qt = qt_ref[:, pl.ds(q_start, qw)]
        frow = lax.broadcasted_iota(I32, (hw, qw), 0)
        zero = jnp.zeros_like(qt)
        qc = (jnp.where(frow < HEAD_DIM, qt, zero), jnp.where(frow >= HEAD_DIM, qt, zero))
        m_ref[...] = jnp.full(m_ref.shape, -jnp.inf, F32)
        l_ref[...] = jnp.zeros(l_ref.shape, F32)
        acc_ref[...] = jnp.zeros(acc_ref.shape, F32)

        def scores(k0, kw, qo, qn, visible, meta=False):
            kblk = km_ref[0:kw, :] if meta else k_ref[pl.ds(k0, kw), :]
            k0 = n_x if meta else k0
            for c in range(2):
                s = jnp.dot(kblk, qc[c][:, qo:qo + qn], preferred_element_type=F32)
                if visible is not None:
                    krow = lax.broadcasted_iota(I32, (kw, qn), 0)
                    qcol = lax.broadcasted_iota(I32, (kw, qn), 1)
                    s = jnp.where(visible(krow, qcol), s, -jnp.inf)
                s_ref[c, pl.ds(k0, kw), qo:qo + qn] = s
                smax = jnp.max(s.reshape(kw // SUBLANES, SUBLANES, qn), axis=0)
                m_ref[c, :, qo:qo + qn] = jnp.maximum(m_ref[c, :, qo:qo + qn], smax)

        def diag_and_meta(fn):
            for ko, kw, qo, qn in diag:
                fn(q_start + ko, kw, qo, qn, lambda krow, qcol, d=ko - qo: krow + d <= qcol)
            fn(0, LANES, 0, qw, lambda krow, qcol: krow < N_META, meta=True)

        for kb in range(n_prev):
            scores(kb * KB, KB, 0, qw, None)
        diag_and_meta(scores)
        m = [jnp.max(m_ref[c, :, 0:qw], axis=0, keepdims=True) for c in range(2)]

        def weighted_values(k0, kw, qo, qn, visible, meta=False):
            del visible
            vblk = vtm_ref[:, 0:kw] if meta else vt_ref[:, pl.ds(k0, kw)]
            k0 = n_x if meta else k0
            for c in range(2):
                p = jnp.exp2(s_ref[c, pl.ds(k0, kw), qo:qo + qn] - m[c][:, qo:qo + qn])
                acc_ref[c, :, qo:qo + qn] += jnp.dot(vblk, p.astype(BF16), preferred_element_type=F32)
                l_ref[c, :, qo:qo + qn] += jnp.sum(p.reshape(kw // SUBLANES, SUBLANES, qn), axis=0)

        for kb in range(n_prev):
            weighted_values(kb * KB, KB, 0, qw, None)
        diag_and_meta(weighted_values)
        l = [jnp.sum(l_ref[c, :, 0:qw], axis=0, keepdims=True) for c in range(2)]
        ot = acc_ref[0, :, 0:qw] / l[0] - lam_full * (acc_ref[1, :, 0:qw] / l[1])
        ms = jnp.mean(ot * ot, axis=0, keepdims=True)
        y = ot * lax.rsqrt(ms + SUBLN_EPS) * sg_ref[...] * (1.0 - lambda_init)
        o_ref[pl.ds(q_start, qw), :] = y.T.astype(BF16)

    x_diag = ((0, half, 0, QB), (half, half, half, half))

    for r in range(n_x // QB):
        query_block(r * QB, r, QB, x_diag)


def _attention(qt, k, vt, lam, sg, lambda_init, bsz, n_x):
    hw = 2 * HEAD_DIM
    meta_blk = bsz * n_x // TM
    fmaj = pl.BlockSpec((hw, n_x), lambda b, h: (h, b))
    tmaj = pl.BlockSpec((n_x, hw), lambda b, h: (b, h))
    return pl.pallas_call(
        functools.partial(_attn_kernel, lambda_init),
        grid=(bsz, N_HEADS),
        in_specs=[fmaj, tmaj, pl.BlockSpec((TM, hw), lambda b, h: (meta_blk, h)),
                  fmaj, pl.BlockSpec((hw, TM), lambda b, h: (h, meta_blk)),
                  pl.BlockSpec((4, HEAD_DIM), lambda b, h: (0, 0)),
                  pl.BlockSpec((hw, 1), lambda b, h: (0, 0))],
        out_specs=tmaj,
        out_shape=jax.ShapeDtypeStruct((bsz * n_x, D), BF16),
        scratch_shapes=[pltpu.VMEM((2, n_x + LANES, QB), F32), pltpu.VMEM((2, SUBLANES, QB), F32),
                        pltpu.VMEM((2, SUBLANES, QB), F32), pltpu.VMEM((2, hw, QB), F32)],
        compiler_params=pltpu.CompilerParams(dimension_semantics=("arbitrary",) * 2,
                                             vmem_limit_bytes=VMEM_LIMIT),
        name="diff_attention",
    )(qt, k, k, vt, vt, lam, sg)


def _attn_out_kernel(h_ref, o_ref, wo_ref, fg_ref, wrt_ref, br_ref, tri_ref,
                     h3_ref, xtt_ref, ri_ref, cnt_ref, run_ref):
    @pl.when(pl.program_id(0) == 0)
    def _():
        run_ref[...] = jnp.zeros_like(run_ref)

    h3 = h_ref[...] + jnp.dot(o_ref[...], wo_ref[...], preferred_element_type=F32)
    h3_ref[...] = h3
    _route_tail(h3, fg_ref, wrt_ref, br_ref, tri_ref, xtt_ref, ri_ref, cnt_ref, run_ref)


def _attn_out(h, o, wo, fg, wrt, br, tri):
    tp = h.shape[0]
    last_o = o.shape[0] // TM - 1
    row = pl.BlockSpec((TM, D), lambda i: (i, 0))
    return pl.pallas_call(
        _attn_out_kernel,
        grid=(tp // TM,),
        in_specs=[row, pl.BlockSpec((TM, D), lambda i: (jnp.minimum(i, last_o), 0)),
                  pl.BlockSpec((D, D), lambda i: (0, 0))] + _route_in_specs(),
        out_specs=(row,) + _route_out_specs(),
        out_shape=(jax.ShapeDtypeStruct((tp, D), F32),) + _route_out_shapes(tp),
        scratch_shapes=[pltpu.VMEM((BUCKET_ROWS, LANES), F32)],
        compiler_params=pltpu.CompilerParams(dimension_semantics=("arbitrary",),
                                             vmem_limit_bytes=VMEM_LIMIT),
        name="attn_out",
    )(h, o, wo, fg, wrt, br, tri)


def _final_kernel(h_ref, pos_ref, nxt_ref, ys_ref, g_ref, o_ref, buf_ref, sem):
    y, drain = _gather_expert_rows(pos_ref, nxt_ref, ys_ref, buf_ref, sem)
    o_ref[...] = _rms(h_ref[...] + y, g_ref[...], RMS_EPS)
    drain()


def _final(h, pos3, ys, g, n_x_tiles):
    row = pl.BlockSpec((TM, D), lambda i: (i, 0))
    return pl.pallas_call(
        _final_kernel,
        grid=(n_x_tiles,),
        in_specs=[row] + _gather_specs(n_x_tiles) + [pl.BlockSpec((1, D), lambda i: (0, 0))],
        out_specs=row,
        out_shape=jax.ShapeDtypeStruct((n_x_tiles * TM, D), F32),
        scratch_shapes=GATHER_SCRATCH,
        compiler_params=pltpu.CompilerParams(dimension_semantics=("arbitrary",),
                                             vmem_limit_bytes=VMEM_LIMIT),
        name="final_norm",
    )(h, pos3, pos3, ys, g)


def _router_params(w_rg, b_rg, w_re, b_re):
    wr = jnp.concatenate([w_rg, w_re], axis=1)
    wrt = jnp.zeros((BUCKET_ROWS, D), F32).at[:N_GROUPS + N_EXPERTS].set(wr.T).astype(BF16)
    br = jnp.zeros((BUCKET_ROWS, 1), F32).at[:N_GROUPS + N_EXPERTS, 0].set(jnp.concatenate([b_rg, b_re]))
    return wrt, br


def kernel(x, meta_tokens, a_norm, a_w_in, a_conv, a_w_out, kv_norm, w_kv, b_norm, b_w_q, b_lambda, b_subln, b_w_o, ffn_norm, router_group_w, router_group_b, router_expert_w, router_expert_b, expert_w_gate, expert_w_up, expert_w_down, final_norm):
    bsz, seq, d = x.shape
    assert d == D and a_norm.shape[0] == 1 and b_norm.shape[0] == 1
    assert meta_tokens.shape[0] == N_META and seq % TM == 0
    x_tiles = seq // TM
    n_x_tiles = bsz * x_tiles
    tp = (n_x_tiles + 1) * TM
    assert tp % TPERM == 0 and tp % TMM == 0

    metapad = jnp.concatenate([meta_tokens.astype(x.dtype), jnp.zeros((TM - N_META, D), x.dtype)])
    tri = jnp.triu(jnp.ones((TM, TM), F32)).astype(BF16)
    bf = lambda w: w.astype(BF16)

    wrt, br = _router_params(router_group_w[0], router_group_b[0], router_expert_w[0], router_expert_b[0])
    h, xtt, ri, cnt = _mixer_a(x.reshape(bsz * seq, D), metapad, a_norm[0][None], bf(a_w_in[0]), a_conv[0],
                               bf(a_w_out[0]), ffn_norm[0][None], wrt, br, tri, x_tiles)
    ys, pos3 = _moe_layer(xtt, ri, cnt, expert_w_gate[0], expert_w_up[0], expert_w_down[0], tp)

    h, qt, k, vt = _attn_proj(h, pos3, ys, kv_norm[None], b_norm[0][None],
                              bf(w_kv[:, :D]), bf(w_kv[:, D:].T), bf(b_w_q[0].T))
    o = _attention(qt, k, vt, b_lambda[0], b_subln[0][:, None], _lambda_init(1), bsz, seq)
    wrt, br = _router_params(router_group_w[1], router_group_b[1], router_expert_w[1], router_expert_b[1])
    h, xtt, ri, cnt = _attn_out(h, o, bf(b_w_o[0]), ffn_norm[1][None], wrt, br, tri)
    ys, pos3 = _moe_layer(xtt, ri, cnt, expert_w_gate[1], expert_w_up[1], expert_w_down[1], tp)

    out = _final(h, pos3, ys, final_norm[None], n_x_tiles)
    return out.reshape(bsz, seq, D)
```

```python
import functools
import math

import jax
import jax.numpy as jnp
from jax import lax
from jax.experimental import pallas as pl
from jax.experimental.pallas import tpu as pltpu

F32 = jnp.float32
BF16 = jnp.bfloat16
I32 = jnp.int32
U32 = jnp.uint32

D = 1024
N_META = 16
Q_BLOCK = 128
HEAD_DIM = 64
N_HEADS = D // (2 * HEAD_DIM)
N_GROUPS = 4
EPG = 4
N_EXPERTS = N_GROUPS * EPG
D_EXPERT = D // 2
RMS_EPS = 1e-6
SUBLN_EPS = 1e-5
LOG2E = math.log2(math.e)

LANES = 128
SUBLANES = 8
TOK_ROWS = SUBLANES
XCHUNKS = D // LANES

TM = 256
QB = 512
KB = 512
TMM = 256
TPERM = 768
PERM_UNROLL = 8
N_PAIRS = 6
N_BUCKETS = N_GROUPS * N_PAIRS
BUCKET_ROWS = 32
PAIR_LO = (0, 0, 0, 1, 1, 2)
PAIR_HI = (1, 2, 3, 2, 3, 3)
VMEM_LIMIT = 56 * 1024 * 1024


def _lambda_init(layer_idx):
    return 0.8 - 0.6 * math.exp(-0.3 * layer_idx)


def _rms(x, g, eps):
    ms = jnp.mean(x * x, axis=-1, keepdims=True)
    return x * lax.rsqrt(ms + eps) * g


def _load_token_tiled(ref, n_tok, n_chunks):
    return [ref[pl.ds(r, n_tok, stride=TOK_ROWS), :] for r in range(n_chunks)]


def _store_token_tiled(ref, chunks, n_tok):
    for r, c in enumerate(chunks):
        ref[pl.ds(r, n_tok, stride=TOK_ROWS), :] = c


def _route_tail(h, fg_ref, wrt_ref, br_ref, tri_ref, xtt_ref, ri_ref, cnt_ref, run_ref):
    tm = h.shape[0]
    xn = _rms(h, fg_ref[...], RMS_EPS)
    xb = xn.astype(BF16)

    lt = lax.dot_general(wrt_ref[...], xb, (((1,), (1,)), ((), ())), preferred_element_type=F32)
    lt = lt + br_ref[...]
    lg = [lt[k:k + 1, :] for k in range(N_GROUPS)]
    m = jnp.maximum(jnp.maximum(lg[0], lg[1]), jnp.maximum(lg[2], lg[3]))
    gidx = jnp.where(lg[0] == m, 0, jnp.where(lg[1] == m, 1, jnp.where(lg[2] == m, 2, 3))).astype(I32)
    se = jnp.exp(lg[0] - m) + jnp.exp(lg[1] - m) + jnp.exp(lg[2] - m) + jnp.exp(lg[3] - m)
    p_sel = 1.0 / se

    def le_row(g, j):
        r = N_GROUPS + g * EPG + j
        return lt[r:r + 1, :]

    sel = [jnp.where(gidx == 0, le_row(0, j),
                     jnp.where(gidx == 1, le_row(1, j),
                               jnp.where(gidx == 2, le_row(2, j), le_row(3, j)))) for j in range(EPG)]

    def first_argmax(vals):
        v = jnp.maximum(jnp.maximum(vals[0], vals[1]), jnp.maximum(vals[2], vals[3]))
        i = jnp.where(vals[0] == v, 0, jnp.where(vals[1] == v, 1, jnp.where(vals[2] == v, 2, 3))).astype(I32)
        return v, i

    v1, i1 = first_argmax(sel)
    sel2 = [jnp.where(i1 == j, -jnp.inf, sel[j]) for j in range(EPG)]
    v2, i2 = first_argmax(sel2)
    e2 = jnp.exp(v2 - v1)
    den = 1.0 + e2
    w1 = (1.0 / den) * p_sel
    w2 = (e2 / den) * p_sel
    lo = jnp.minimum(i1, i2)
    hi = jnp.maximum(i1, i2)
    first_is_lo = i1 < i2
    w_lo = jnp.where(first_is_lo, w1, w2)
    w_hi = jnp.where(first_is_lo, w2, w1)
    pair = jnp.where(lo == 0, hi - 1, jnp.where(lo == 1, hi + 1, 5))
    bucket = gidx * N_PAIRS + pair

    rows = lax.broadcasted_iota(I32, (BUCKET_ROWS, tm), 0)
    ohf = (rows == bucket).astype(F32)
    cum = jnp.dot(ohf.astype(BF16), tri_ref[...], preferred_element_type=F32)
    run = run_ref[:, 0:1]
    rank = jnp.sum(ohf * (cum - 1.0 + run), axis=0, keepdims=True)
    run_ref[...] = run_ref[...] + jnp.sum(ohf, axis=1, keepdims=True)
    cnt_ref[...] = run_ref[...]

    ri_ref[0:1, :] = bucket
    ri_ref[1:2, :] = rank.astype(I32)
    ri_ref[2:SUBLANES, :] = jnp.zeros((SUBLANES - 2, tm), I32)

    wt = jnp.concatenate([w_lo, w_hi, jnp.zeros((LANES - 2, tm), F32)], axis=0).T

    bits = lax.bitcast_convert_type(xb.astype(F32), U32)
    half = D // 2
    packed = (bits[:, :half] >> 16) | (bits[:, half:] & jnp.uint32(0xFFFF0000))
    n_pk = half // LANES
    chunks = [packed[:, r * LANES:(r + 1) * LANES] for r in range(n_pk)]
    chunks.append(lax.bitcast_convert_type(wt, U32))
    chunks += [jnp.zeros((tm, LANES), U32)] * (TOK_ROWS - n_pk - 1)
    _store_token_tiled(xtt_ref, chunks, tm)


def _route_out_shapes(tp):
    return (jax.ShapeDtypeStruct((tp * TOK_ROWS, LANES), U32),
            jax.ShapeDtypeStruct((SUBLANES, tp), I32),
            jax.ShapeDtypeStruct((BUCKET_ROWS, LANES), F32))


def _route_out_specs():
    return (pl.BlockSpec((TM * TOK_ROWS, LANES), lambda i: (i, 0)),
            pl.BlockSpec((SUBLANES, TM), lambda i: (0, i)),
            pl.BlockSpec((BUCKET_ROWS, LANES), lambda i: (0, 0)))


def _route_in_specs():
    return [pl.BlockSpec((1, D), lambda i: (0, 0)),
            pl.BlockSpec((BUCKET_ROWS, D), lambda i: (0, 0)),
            pl.BlockSpec((BUCKET_ROWS, 1), lambda i: (0, 0)),
            pl.BlockSpec((TM, TM), lambda i: (0, 0))]


def _mixer_a_kernel(x_tiles, n_x_tiles, x_ref, meta_ref, g_ref, win_ref, conv_ref, wout_ref,
                    fg_ref, wrt_ref, br_ref, tri_ref,
                    h1_ref, xtt_ref, ri_ref, cnt_ref, zs_ref, zmeta_ref, run_ref):
    i = pl.program_id(0)

    def conv_inputs(h):
        xn = _rms(h, g_ref[...], RMS_EPS).astype(BF16)
        bcu = jnp.dot(xn, win_ref[...], preferred_element_type=F32)
        return bcu, bcu[:, D:2 * D] * bcu[:, 2 * D:3 * D]

    @pl.when(i == 0)
    def _():
        run_ref[...] = jnp.zeros_like(run_ref)
        _, zm = conv_inputs(meta_ref[...])
        zmeta_ref[...] = zm[N_META - SUBLANES:N_META, :]

    @pl.when(i % x_tiles == 0)
    def _():
        zs_ref[0:SUBLANES, :] = zmeta_ref[...]

    @pl.when(i == n_x_tiles)
    def _():
        zs_ref[0:SUBLANES, :] = jnp.zeros((SUBLANES, D), F32)

    h = jnp.where(i < n_x_tiles, x_ref[...], meta_ref[...])
    bcu, z = conv_inputs(h)
    zs_ref[SUBLANES:SUBLANES + TM, :] = z
    cw = conv_ref[...]
    conv = (cw[0:1, :] * zs_ref[SUBLANES - 2:SUBLANES - 2 + TM, :]
            + cw[1:2, :] * zs_ref[SUBLANES - 1:SUBLANES - 1 + TM, :]
            + cw[2:3, :] * z)
    zs_ref[0:SUBLANES, :] = zs_ref[TM:TM + SUBLANES, :]
    mix = jnp.dot((bcu[:, 0:D] * conv).astype(BF16), wout_ref[...], preferred_element_type=F32)
    h1 = h + mix
    h1_ref[...] = h1
    _route_tail(h1, fg_ref, wrt_ref, br_ref, tri_ref, xtt_ref, ri_ref, cnt_ref, run_ref)


def _mixer_a(x2, metapad, g, w_in, conv_w, w_out, fg, wrt, br, tri, x_tiles):
    n_x_tiles = x2.shape[0] // TM
    tp = (n_x_tiles + 1) * TM
    return pl.pallas_call(
        functools.partial(_mixer_a_kernel, x_tiles, n_x_tiles),
        grid=(tp // TM,),
        in_specs=[pl.BlockSpec((TM, D), lambda i: (jnp.minimum(i, n_x_tiles - 1), 0)),
                  pl.BlockSpec((TM, D), lambda i: (0, 0)),
                  pl.BlockSpec((1, D), lambda i: (0, 0)),
                  pl.BlockSpec((D, 3 * D), lambda i: (0, 0)),
                  pl.BlockSpec((3, D), lambda i: (0, 0)),
                  pl.BlockSpec((D, D), lambda i: (0, 0))] + _route_in_specs(),
        out_specs=(pl.BlockSpec((TM, D), lambda i: (i, 0)),) + _route_out_specs(),
        out_shape=(jax.ShapeDtypeStruct((tp, D), F32),) + _route_out_shapes(tp),
        scratch_shapes=[pltpu.VMEM((TM + SUBLANES, D), F32),
                        pltpu.VMEM((SUBLANES, D), F32),
                        pltpu.VMEM((BUCKET_ROWS, LANES), F32)],
        compiler_params=pltpu.CompilerParams(dimension_semantics=("arbitrary",),
                                             vmem_limit_bytes=VMEM_LIMIT),
        name="mixer_a",
    )(x2, metapad, g, w_in, conv_w, w_out, fg, wrt, br, tri)


def _run_token_copies(copy):
    def start(g, c):
        for u in range(PERM_UNROLL):
            copy(g * PERM_UNROLL + u).start()
        return c

    def wait(g, c):
        for u in range(PERM_UNROLL):
            copy(g * PERM_UNROLL + u).wait()
        return c

    lax.fori_loop(0, TPERM // PERM_UNROLL, start, 0)
    lax.fori_loop(0, TPERM // PERM_UNROLL, wait, 0)


def _tok_rows(ref, t):
    return ref.at[pl.ds(pl.multiple_of(t * TOK_ROWS, TOK_ROWS), TOK_ROWS), :]


def _scatter_kernel(n_dst_tiles, ztile_ref, total_ref, b_ref, src_ref, dst_ref, zero_ref, zsem, sem):
    @pl.when(pl.program_id(0) == 0)
    def _():
        zero_ref[...] = jnp.zeros_like(zero_ref)

        def zero_tile(t):
            rows = TMM * TOK_ROWS
            return pltpu.make_async_copy(
                zero_ref, dst_ref.at[pl.ds(pl.multiple_of(t * rows, rows), rows), :], zsem)

        def each_zero_tile(fn):
            for b in range(N_BUCKETS):
                @pl.when(ztile_ref[b] >= 0)
                def _():
                    fn(zero_tile(ztile_ref[b]))

            def tail(t, c):
                fn(zero_tile(t))
                return c

            lax.fori_loop(total_ref[0], n_dst_tiles, tail, 0)

        each_zero_tile(lambda c: c.start())
        each_zero_tile(lambda c: c.wait())

    _run_token_copies(lambda k: pltpu.make_async_copy(
        _tok_rows(src_ref, k), _tok_rows(dst_ref, b_ref[0, 0, k]), sem))


def _scatter_tokens(src, b_idx, ztile, total, n_dst_tiles):
    steps = b_idx.shape[0] // TPERM
    grid_spec = pltpu.PrefetchScalarGridSpec(
        num_scalar_prefetch=2,
        grid=(steps,),
        in_specs=[pl.BlockSpec((1, 1, TPERM), lambda i, *_: (i, 0, 0), memory_space=pltpu.SMEM),
                  pl.BlockSpec((TPERM * TOK_ROWS, LANES), lambda i, *_: (i, 0))],
        out_specs=pl.BlockSpec(memory_space=pl.ANY),
        scratch_shapes=[pltpu.VMEM((TMM * TOK_ROWS, LANES), src.dtype),
                        pltpu.SemaphoreType.DMA, pltpu.SemaphoreType.DMA],
    )
    return pl.pallas_call(
        functools.partial(_scatter_kernel, n_dst_tiles),
        grid_spec=grid_spec,
        out_shape=jax.ShapeDtypeStruct((n_dst_tiles * TMM * TOK_ROWS, LANES), src.dtype),
        compiler_params=pltpu.CompilerParams(dimension_semantics=("arbitrary",)),
        name="scatter_tokens",
    )(ztile, total.reshape(1), b_idx.reshape(steps, 1, TPERM), src)


def _moe_kernel(blk_ref, ea_ref, eb_ref, nrow_ref, newa_ref, newb_ref,
                xs_ref, wga_f32, wua_f32, wda_f32, wgb_f32, wub_f32, wdb_f32, ys_ref,
                wga_ref, wua_ref, wda_ref, wgb_ref, wub_ref, wdb_ref):
    j = pl.program_id(0)
    nrow = nrow_ref[j]

    @pl.when(newa_ref[j] == 1)
    def _():
        for dst, src in ((wga_ref, wga_f32), (wua_ref, wua_f32), (wda_ref, wda_f32)):
            dst[...] = src[...].astype(BF16)

    @pl.when(newb_ref[j] == 1)
    def _():
        for dst, src in ((wgb_ref, wgb_f32), (wub_ref, wub_f32), (wdb_ref, wdb_f32)):
            dst[...] = src[...].astype(BF16)

    @pl.when(nrow > 0)
    def _():
        n_pk = (D // 2) // LANES
        words = _load_token_tiled(xs_ref, TMM, n_pk + 1)
        lo = [lax.bitcast_convert_type(w << 16, F32) for w in words[:n_pk]]
        hi = [lax.bitcast_convert_type(w & jnp.uint32(0xFFFF0000), F32) for w in words[:n_pk]]
        x = jnp.concatenate(lo + hi, axis=1).astype(BF16)
        wrow = lax.bitcast_convert_type(words[n_pk], F32)

        def expert(wg_ref, wu_ref, wd_ref, c):
            g = jnp.dot(x, wg_ref[...], preferred_element_type=F32)
            u = jnp.dot(x, wu_ref[...], preferred_element_type=F32)
            hmid = (jax.nn.silu(g) * u) * c
            return jnp.dot(hmid.astype(BF16), wd_ref[...], preferred_element_type=F32)

        y = expert(wga_ref, wua_ref, wda_ref, wrow[:, 0:1]) + expert(wgb_ref, wub_ref, wdb_ref, wrow[:, 1:2])
        _store_token_tiled(ys_ref, [y[:, r * LANES:(r + 1) * LANES] for r in range(XCHUNKS)], TMM)

    @pl.when(nrow == 0)
    def _():
        ys_ref[...] = jnp.zeros_like(ys_ref)


def _moe(xs, w_gate, w_up, w_down, layer, blk, ea, eb, nrow):
    n_tiles = blk.shape[0]
    p_tok = xs.shape[0] // TOK_ROWS
    first = jnp.ones((1,), I32)
    newa = jnp.concatenate([first, (ea[1:] != ea[:-1]).astype(I32)])
    newb = jnp.concatenate([first, (eb[1:] != eb[:-1]).astype(I32)])
    tok_spec = pl.BlockSpec((TMM * TOK_ROWS, LANES), lambda j, blk, *_: (blk[j], 0))

    def wspec(shape, which):
        if which == 0:
            return pl.BlockSpec((None, None) + shape, lambda j, blk, ea, eb, *_: (layer, ea[j], 0, 0))
        return pl.BlockSpec((None, None) + shape, lambda j, blk, ea, eb, *_: (layer, eb[j], 0, 0))

    w_in, w_out = (D, D_EXPERT), (D_EXPERT, D)
    grid_spec = pltpu.PrefetchScalarGridSpec(
        num_scalar_prefetch=6,
        grid=(n_tiles,),
        in_specs=[tok_spec, wspec(w_in, 0), wspec(w_in, 0), wspec(w_out, 0),
                  wspec(w_in, 1), wspec(w_in, 1), wspec(w_out, 1)],
        out_specs=pl.BlockSpec((TMM * TOK_ROWS, LANES), lambda j, *_: (j, 0)),
        scratch_shapes=[pltpu.VMEM(w_in, BF16), pltpu.VMEM(w_in, BF16), pltpu.VMEM(w_out, BF16)] * 2,
    )
    return pl.pallas_call(
        _moe_kernel,
        grid_spec=grid_spec,
        out_shape=jax.ShapeDtypeStruct((p_tok * TOK_ROWS, LANES), F32),
        compiler_params=pltpu.CompilerParams(dimension_semantics=("arbitrary",),
                                             vmem_limit_bytes=VMEM_LIMIT),
        name="moe_experts",
    )(blk, ea, eb, nrow, newa, newb, xs, w_gate, w_up, w_down, w_gate, w_up, w_down)


def _count_le(ends, v):
    return jnp.sum((ends[None, :] <= v[:, None]).astype(I32), axis=1)


def _lookup(table, idx):
    hit = idx[:, None] == jnp.arange(table.shape[0], dtype=I32)[None, :]
    return jnp.sum(jnp.where(hit, table[None, :], 0), axis=1)


def _moe_layer(xtt, ri, cnt, w_gate, w_up, w_down, layer, tp):
    counts = cnt[:N_BUCKETS, 0].astype(I32)
    ntile = (counts + TMM - 1) // TMM
    tend = jnp.cumsum(ntile)
    tstart = tend - ntile
    total = tend[-1]
    pos = _lookup(tstart * TMM, ri[0]) + ri[1]

    n_tiles = tp // TMM + N_BUCKETS
    j = jnp.arange(n_tiles, dtype=I32)
    blk = jnp.minimum(j, total - 1)
    tb = jnp.minimum(_count_le(tend, blk), N_BUCKETS - 1)
    grp = tb // N_PAIRS
    pair = tb % N_PAIRS
    ea = grp * EPG + _lookup(jnp.asarray(PAIR_LO, I32), pair)
    eb = grp * EPG + _lookup(jnp.asarray(PAIR_HI, I32), pair)
    nrow = jnp.clip(_lookup(counts, tb) - (j - _lookup(tstart, tb)) * TMM, 0, TMM)
    nrow = jnp.where(j < total, nrow, 0).astype(I32)
    ztile = jnp.where(ntile > 0, tend - 1, -1)

    xs = _scatter_tokens(xtt, pos, ztile, total, n_tiles)
    ys = _moe(xs, w_gate, w_up, w_down, layer, blk, ea, eb, nrow)
    return ys, pos.reshape(tp // TM, 1, TM)


GATHER_AHEAD = 2
GATHER_BUFS = GATHER_AHEAD + 1


def _gather_expert_rows(pos_refs, ys_ref, buf_ref, sem):
    i = pl.program_id(0)
    n = pl.num_programs(0)

    def copies(idx_ref, s):
        return [pltpu.make_async_copy(_tok_rows(ys_ref, idx_ref[0, 0, k]),
                                      buf_ref.at[s, pl.ds(k * TOK_ROWS, TOK_ROWS), :], sem.at[s])
                for k in range(TM)]

    @pl.when(i == 0)
    def _():
        for d in range(GATHER_AHEAD):
            for c in copies(pos_refs[d], d):
                c.start()

    slot = i % GATHER_BUFS
    for c in copies(pos_refs[0], slot):
        c.wait()
    y = jnp.concatenate([buf_ref[slot, pl.ds(r, TM, stride=TOK_ROWS), :] for r in range(XCHUNKS)], axis=1)
    for c in copies(pos_refs[GATHER_AHEAD], (i + GATHER_AHEAD) % GATHER_BUFS):
        c.start()

    def drain():
        @pl.when(i == n - 1)
        def _():
            for d in range(1, GATHER_BUFS):
                for c in copies(pos_refs[0], (i + d) % GATHER_BUFS):
                    c.wait()

    return y, drain


def _gather_specs(n_steps):
    assert n_steps > GATHER_AHEAD
    ahead = lambda d: pl.BlockSpec((1, 1, TM), lambda i: (jnp.minimum(i + d, n_steps - 1), 0, 0),
                                   memory_space=pltpu.SMEM)
    return [ahead(d) for d in range(GATHER_BUFS)] + [pl.BlockSpec(memory_space=pl.ANY)]


GATHER_SCRATCH = [pltpu.VMEM((GATHER_BUFS, TM * TOK_ROWS, LANES), F32),
                  pltpu.SemaphoreType.DMA((GATHER_BUFS,))]


def _attn_proj_kernel(h_ref, pos0_ref, pos1_ref, pos2_ref, ys_ref, gkv_ref, gq_ref, wk_ref, wvt_ref,
                      wqt_ref, h2_ref, qt_ref, k_ref, vt_ref, buf_ref, sem):
    y, drain = _gather_expert_rows((pos0_ref, pos1_ref, pos2_ref), ys_ref, buf_ref, sem)
    h2 = h_ref[...] + y
    h2_ref[...] = h2
    ms = jnp.mean(h2 * h2, axis=-1, keepdims=True)
    xhat = h2 * lax.rsqrt(ms + RMS_EPS)
    xkv = (xhat * gkv_ref[...]).astype(BF16)
    xq = (xhat * gq_ref[...]).astype(BF16)
    nt = (((1,), (1,)), ((), ()))
    k_ref[...] = jnp.dot(xkv, wk_ref[...], preferred_element_type=F32).astype(BF16)
    vt_ref[...] = lax.dot_general(wvt_ref[...], xkv, nt, preferred_element_type=F32).astype(BF16)
    qt = lax.dot_general(wqt_ref[...], xq, nt, preferred_element_type=F32)
    qt_ref[...] = (qt * (LOG2E * HEAD_DIM ** -0.5)).astype(BF16)
    drain()


def _attn_proj(h, pos3, ys, gkv, gq, wk, wvt, wqt):
    tp = h.shape[0]
    steps = tp // TM
    row = pl.BlockSpec((TM, D), lambda i: (i, 0))
    col = pl.BlockSpec((D, TM), lambda i: (0, i))
    vec = pl.BlockSpec((1, D), lambda i: (0, 0))
    mat = pl.BlockSpec((D, D), lambda i: (0, 0))
    return pl.pallas_call(
        _attn_proj_kernel,
        grid=(steps,),
        in_specs=[row] + _gather_specs(steps) + [vec, vec, mat, mat, mat],
        out_specs=(row, col, row, col),
        out_shape=(jax.ShapeDtypeStruct((tp, D), F32), jax.ShapeDtypeStruct((D, tp), BF16),
                   jax.ShapeDtypeStruct((tp, D), BF16), jax.ShapeDtypeStruct((D, tp), BF16)),
        scratch_shapes=GATHER_SCRATCH,
        compiler_params=pltpu.CompilerParams(dimension_semantics=("arbitrary",),
                                             vmem_limit_bytes=VMEM_LIMIT),
        name="attn_proj",
    )(h, *([pos3] * GATHER_BUFS), ys, gkv, gq, wk, wvt, wqt)


def _attn_kernel(lambda_init, qt_ref, k_ref, km_ref, vt_ref, vtm_ref, lam_ref, sg_ref, o_ref,
                 s_ref, m_ref, l_ref, acc_ref):
    hw = 2 * HEAD_DIM
    n_x = o_ref.shape[0]
    half = QB // 2
    assert QB == KB and n_x % QB == 0 and N_META <= LANES
    lam = lam_ref[...]
    lam_full = (jnp.exp(jnp.sum(lam[0:1, :] * lam[1:2, :], axis=1, keepdims=True))
                - jnp.exp(jnp.sum(lam[2:3, :] * lam[3:4, :], axis=1, keepdims=True)) + lambda_init)

    def query_block(q_start, n_prev, qw, diag):
        qt = qt_ref[:, pl.ds(q_start, qw)]
        frow = lax.broadcasted_iota(I32, (hw, qw), 0)
        zero = jnp.zeros_like(qt)
        qc = (jnp.where(frow < HEAD_DIM, qt, zero), jnp.where(frow >= HEAD_DIM, qt, zero))
        m_ref[...] = jnp.full(m_ref.shape, -jnp.inf, F32)
        l_ref[...] = jnp.zeros(l_ref.shape, F32)
        acc_ref[...] = jnp.zeros(acc_ref.shape, F32)

        def scores(k0, kw, qo, qn, visible, meta=False):
            kblk = km_ref[0:kw, :] if meta else k_ref[pl.ds(k0, kw), :]
            k0 = n_x if meta else k0
            for c in range(2):
                s = jnp.dot(kblk, qc[c][:, qo:qo + qn], preferred_element_type=F32)
                if visible is not None:
                    krow = lax.broadcasted_iota(I32, (kw, qn), 0)
                    qcol = lax.broadcasted_iota(I32, (kw, qn), 1)
                    s = jnp.where(visible(krow, qcol), s, -jnp.inf)
                s_ref[c, pl.ds(k0, kw), qo:qo + qn] = s
                smax = jnp.max(s.reshape(kw // SUBLANES, SUBLANES, qn), axis=0)
                m_ref[c, :, qo:qo + qn] = jnp.maximum(m_ref[c, :, qo:qo + qn], smax)

        def diag_and_meta(fn):
            for ko, kw, qo, qn in diag:
                fn(q_start + ko, kw, qo, qn, lambda krow, qcol, d=ko - qo: krow + d <= qcol)
            fn(0, LANES, 0, qw, lambda krow, qcol: krow < N_META, meta=True)

        for kb in range(n_prev):
            scores(kb * KB, KB, 0, qw, None)
        diag_and_meta(scores)
        m = [jnp.max(m_ref[c, :, 0:qw], axis=0, keepdims=True) for c in range(2)]

        def weighted_values(k0, kw, qo, qn, visible, meta=False):
            del visible
            vblk = vtm_ref[:, 0:kw] if meta else vt_ref[:, pl.ds(k0, kw)]
            k0 = n_x if meta else k0
            for c in range(2):
                p = jnp.exp2(s_ref[c, pl.ds(k0, kw), qo:qo + qn] - m[c][:, qo:qo + qn])
                acc_ref[c, :, qo:qo + qn] += jnp.dot(vblk, p.astype(BF16), preferred_element_type=F32)
                l_ref[c, :, qo:qo + qn] += jnp.sum(p.reshape(kw // SUBLANES, SUBLANES, qn), axis=0)

        for kb in range(n_prev):
            weighted_values(kb * KB, KB, 0, qw, None)
        diag_and_meta(weighted_values)
        l = [jnp.sum(l_ref[c, :, 0:qw], axis=0, keepdims=True) for c in range(2)]
        ot = acc_ref[0, :, 0:qw] / l[0] - lam_full * (acc_ref[1, :, 0:qw] / l[1])
        ms = jnp.mean(ot * ot, axis=0, keepdims=True)
        y = ot * lax.rsqrt(ms + SUBLN_EPS) * sg_ref[...] * (1.0 - lambda_init)
        o_ref[pl.ds(q_start, qw), :] = y.T.astype(BF16)

    x_diag = ((0, half, 0, QB), (half, half, half, half))

    for r in range(n_x // QB):
        query_block(r * QB, r, QB, x_diag)


def _attention(qt, k, vt, lam, sg, lambda_init, bsz, n_x):
    hw = 2 * HEAD_DIM
    meta_blk = bsz * n_x // TM
    fmaj = pl.BlockSpec((hw, n_x), lambda b, h: (h, b))
    tmaj = pl.BlockSpec((n_x, hw), lambda b, h: (b, h))
    return pl.pallas_call(
        functools.partial(_attn_kernel, lambda_init),
        grid=(bsz, N_HEADS),
        in_specs=[fmaj, tmaj, pl.BlockSpec((TM, hw), lambda b, h: (meta_blk, h)),
                  fmaj, pl.BlockSpec((hw, TM), lambda b, h: (h, meta_blk)),
                  pl.BlockSpec((4, HEAD_DIM), lambda b, h: (0, 0)),
                  pl.BlockSpec((hw, 1), lambda b, h: (0, 0))],
        out_specs=tmaj,
        out_shape=jax.ShapeDtypeStruct((bsz * n_x, D), BF16),
        scratch_shapes=[pltpu.VMEM((2, n_x + LANES, QB), F32), pltpu.VMEM((2, SUBLANES, QB), F32),
                        pltpu.VMEM((2, SUBLANES, QB), F32), pltpu.VMEM((2, hw, QB), F32)],
        compiler_params=pltpu.CompilerParams(dimension_semantics=("arbitrary",) * 2,
                                             vmem_limit_bytes=VMEM_LIMIT),
        name="diff_attention",
    )(qt, k, k, vt, vt, lam, sg)


def _attn_out_kernel(h_ref, o_ref, wo_ref, fg_ref, wrt_ref, br_ref, tri_ref,
                     h3_ref, xtt_ref, ri_ref, cnt_ref, run_ref):
    @pl.when(pl.program_id(0) == 0)
    def _():
        run_ref[...] = jnp.zeros_like(run_ref)

    h3 = h_ref[...] + jnp.dot(o_ref[...], wo_ref[...], preferred_element_type=F32)
    h3_ref[...] = h3
    _route_tail(h3, fg_ref, wrt_ref, br_ref, tri_ref, xtt_ref, ri_ref, cnt_ref, run_ref)


def _attn_out(h, o, wo, fg, wrt, br, tri):
    tp = h.shape[0]
    last_o = o.shape[0] // TM - 1
    row = pl.BlockSpec((TM, D), lambda i: (i, 0))
    return pl.pallas_call(
        _attn_out_kernel,
        grid=(tp // TM,),
        in_specs=[row, pl.BlockSpec((TM, D), lambda i: (jnp.minimum(i, last_o), 0)),
                  pl.BlockSpec((D, D), lambda i: (0, 0))] + _route_in_specs(),
        out_specs=(row,) + _route_out_specs(),
        out_shape=(jax.ShapeDtypeStruct((tp, D), F32),) + _route_out_shapes(tp),
        scratch_shapes=[pltpu.VMEM((BUCKET_ROWS, LANES), F32)],
        compiler_params=pltpu.CompilerParams(dimension_semantics=("arbitrary",),
                                             vmem_limit_bytes=VMEM_LIMIT),
        name="attn_out",
    )(h, o, wo, fg, wrt, br, tri)


def _final_kernel(h_ref, pos0_ref, pos1_ref, pos2_ref, ys_ref, g_ref, o_ref, buf_ref, sem):
    y, drain = _gather_expert_rows((pos0_ref, pos1_ref, pos2_ref), ys_ref, buf_ref, sem)
    o_ref[...] = _rms(h_ref[...] + y, g_ref[...], RMS_EPS)
    drain()


def _final(h, pos3, ys, g, n_x_tiles):
    row = pl.BlockSpec((TM, D), lambda i: (i, 0))
    return pl.pallas_call(
        _final_kernel,
        grid=(n_x_tiles,),
        in_specs=[row] + _gather_specs(n_x_tiles) + [pl.BlockSpec((1, D), lambda i: (0, 0))],
        out_specs=row,
        out_shape=jax.ShapeDtypeStruct((n_x_tiles * TM, D), F32),
        scratch_shapes=GATHER_SCRATCH,
        compiler_params=pltpu.CompilerParams(dimension_semantics=("arbitrary",),
                                             vmem_limit_bytes=VMEM_LIMIT),
        name="final_norm",
    )(h, *([pos3] * GATHER_BUFS), ys, g)


def _router_params(w_rg, b_rg, w_re, b_re):
    wr = jnp.concatenate([w_rg, w_re], axis=1)
    wrt = jnp.zeros((BUCKET_ROWS, D), F32).at[:N_GROUPS + N_EXPERTS].set(wr.T).astype(BF16)
    br = jnp.zeros((BUCKET_ROWS, 1), F32).at[:N_GROUPS + N_EXPERTS, 0].set(jnp.concatenate([b_rg, b_re]))
    return wrt, br


def kernel(x, meta_tokens, a_norm, a_w_in, a_conv, a_w_out, kv_norm, w_kv, b_norm, b_w_q, b_lambda, b_subln, b_w_o, ffn_norm, router_group_w, router_group_b, router_expert_w, router_expert_b, expert_w_gate, expert_w_up, expert_w_down, final_norm):
    bsz, seq, d = x.shape
    assert d == D and a_norm.shape[0] == 1 and b_norm.shape[0] == 1
    assert meta_tokens.shape[0] == N_META and seq % TM == 0
    x_tiles = seq // TM
    n_x_tiles = bsz * x_tiles
    tp = (n_x_tiles + 1) * TM
    assert tp % TPERM == 0 and tp % TMM == 0

    metapad = jnp.concatenate([meta_tokens.astype(x.dtype), jnp.zeros((TM - N_META, D), x.dtype)])
    tri = jnp.triu(jnp.ones((TM, TM), F32)).astype(BF16)
    bf = lambda w: w.astype(BF16)

    wrt, br = _router_params(router_group_w[0], router_group_b[0], router_expert_w[0], router_expert_b[0])
    h, xtt, ri, cnt = _mixer_a(x.reshape(bsz * seq, D), metapad, a_norm[0][None], bf(a_w_in[0]), a_conv[0],
                               bf(a_w_out[0]), ffn_norm[0][None], wrt, br, tri, x_tiles)
    ys, pos3 = _moe_layer(xtt, ri, cnt, expert_w_gate, expert_w_up, expert_w_down, 0, tp)

    h, qt, k, vt = _attn_proj(h, pos3, ys, kv_norm[None], b_norm[0][None],
                              bf(w_kv[:, :D]), bf(w_kv[:, D:].T), bf(b_w_q[0].T))
    o = _attention(qt, k, vt, b_lambda[0], b_subln[0][:, None], _lambda_init(1), bsz, seq)
    wrt, br = _router_params(router_group_w[1], router_group_b[1], router_expert_w[1], router_expert_b[1])
    h, xtt, ri, cnt = _attn_out(h, o, bf(b_w_o[0]), ffn_norm[1][None], wrt, br, tri)
    ys, pos3 = _moe_layer(xtt, ri, cnt, expert_w_gate, expert_w_up, expert_w_down, 1, tp)

    out = _final(h, pos3, ys, final_norm[None], n_x_tiles)
    return out.reshape(bsz, seq, D)
```

```python
import functools
import math

import jax
import jax.numpy as jnp
from jax import lax
from jax.experimental import pallas as pl
from jax.experimental.pallas import tpu as pltpu

F32 = jnp.float32
BF16 = jnp.bfloat16
I32 = jnp.int32
U32 = jnp.uint32

D = 1024
N_META = 16
Q_BLOCK = 128
HEAD_DIM = 64
N_HEADS = D // (2 * HEAD_DIM)
N_GROUPS = 4
EPG = 4
N_EXPERTS = N_GROUPS * EPG
D_EXPERT = D // 2
RMS_EPS = 1e-6
SUBLN_EPS = 1e-5
LOG2E = math.log2(math.e)

LANES = 128
SUBLANES = 8
TOK_ROWS = SUBLANES
XCHUNKS = D // LANES

TM = 256
QB = 512
KB = 512
TMM = 512
TPERM = 768
PERM_UNROLL = 8
N_PAIRS = 6
N_BUCKETS = N_GROUPS * N_PAIRS
BUCKET_ROWS = 32
PAIR_LO = (0, 0, 0, 1, 1, 2)
PAIR_HI = (1, 2, 3, 2, 3, 3)
VMEM_LIMIT = 56 * 1024 * 1024


def _lambda_init(layer_idx):
    return 0.8 - 0.6 * math.exp(-0.3 * layer_idx)


def _rms(x, g, eps):
    ms = jnp.mean(x * x, axis=-1, keepdims=True)
    return x * lax.rsqrt(ms + eps) * g


def _load_token_tiled(ref, n_tok, n_chunks):
    return [ref[pl.ds(r, n_tok, stride=TOK_ROWS), :] for r in range(n_chunks)]


def _store_token_tiled(ref, chunks, n_tok):
    for r, c in enumerate(chunks):
        ref[pl.ds(r, n_tok, stride=TOK_ROWS), :] = c


def _route_tail(h, fg_ref, wrt_ref, br_ref, tri_ref, xtt_ref, ri_ref, cnt_ref, run_ref):
    tm = h.shape[0]
    xn = _rms(h, fg_ref[...], RMS_EPS)
    xb = xn.astype(BF16)

    lt = lax.dot_general(wrt_ref[...], xb, (((1,), (1,)), ((), ())), preferred_element_type=F32)
    lt = lt + br_ref[...]
    lg = [lt[k:k + 1, :] for k in range(N_GROUPS)]
    m = jnp.maximum(jnp.maximum(lg[0], lg[1]), jnp.maximum(lg[2], lg[3]))
    gidx = jnp.where(lg[0] == m, 0, jnp.where(lg[1] == m, 1, jnp.where(lg[2] == m, 2, 3))).astype(I32)
    se = jnp.exp(lg[0] - m) + jnp.exp(lg[1] - m) + jnp.exp(lg[2] - m) + jnp.exp(lg[3] - m)
    p_sel = 1.0 / se

    def le_row(g, j):
        r = N_GROUPS + g * EPG + j
        return lt[r:r + 1, :]

    sel = [jnp.where(gidx == 0, le_row(0, j),
                     jnp.where(gidx == 1, le_row(1, j),
                               jnp.where(gidx == 2, le_row(2, j), le_row(3, j)))) for j in range(EPG)]

    def first_argmax(vals):
        v = jnp.maximum(jnp.maximum(vals[0], vals[1]), jnp.maximum(vals[2], vals[3]))
        i = jnp.where(vals[0] == v, 0, jnp.where(vals[1] == v, 1, jnp.where(vals[2] == v, 2, 3))).astype(I32)
        return v, i

    v1, i1 = first_argmax(sel)
    sel2 = [jnp.where(i1 == j, -jnp.inf, sel[j]) for j in range(EPG)]
    v2, i2 = first_argmax(sel2)
    e2 = jnp.exp(v2 - v1)
    den = 1.0 + e2
    w1 = (1.0 / den) * p_sel
    w2 = (e2 / den) * p_sel
    lo = jnp.minimum(i1, i2)
    hi = jnp.maximum(i1, i2)
    first_is_lo = i1 < i2
    w_lo = jnp.where(first_is_lo, w1, w2)
    w_hi = jnp.where(first_is_lo, w2, w1)
    pair = jnp.where(lo == 0, hi - 1, jnp.where(lo == 1, hi + 1, 5))
    bucket = gidx * N_PAIRS + pair

    rows = lax.broadcasted_iota(I32, (BUCKET_ROWS, tm), 0)
    ohf = (rows == bucket).astype(F32)
    cum = jnp.dot(ohf.astype(BF16), tri_ref[...], preferred_element_type=F32)
    run = run_ref[:, 0:1]
    rank = jnp.sum(ohf * (cum - 1.0 + run), axis=0, keepdims=True)
    run_ref[...] = run_ref[...] + jnp.sum(ohf, axis=1, keepdims=True)
    cnt_ref[...] = run_ref[...]

    ri_ref[0:1, :] = bucket
    ri_ref[1:2, :] = rank.astype(I32)
    ri_ref[2:SUBLANES, :] = jnp.zeros((SUBLANES - 2, tm), I32)

    wt = jnp.concatenate([w_lo, w_hi, jnp.zeros((LANES - 2, tm), F32)], axis=0).T

    bits = lax.bitcast_convert_type(xb.astype(F32), U32)
    half = D // 2
    packed = (bits[:, :half] >> 16) | (bits[:, half:] & jnp.uint32(0xFFFF0000))
    n_pk = half // LANES
    chunks = [packed[:, r * LANES:(r + 1) * LANES] for r in range(n_pk)]
    chunks.append(lax.bitcast_convert_type(wt, U32))
    chunks += [jnp.zeros((tm, LANES), U32)] * (TOK_ROWS - n_pk - 1)
    _store_token_tiled(xtt_ref, chunks, tm)


def _route_out_shapes(tp):
    return (jax.ShapeDtypeStruct((tp * TOK_ROWS, LANES), U32),
            jax.ShapeDtypeStruct((SUBLANES, tp), I32),
            jax.ShapeDtypeStruct((BUCKET_ROWS, LANES), F32))


def _route_out_specs():
    return (pl.BlockSpec((TM * TOK_ROWS, LANES), lambda i: (i, 0)),
            pl.BlockSpec((SUBLANES, TM), lambda i: (0, i)),
            pl.BlockSpec((BUCKET_ROWS, LANES), lambda i: (0, 0)))


def _route_in_specs():
    return [pl.BlockSpec((1, D), lambda i: (0, 0)),
            pl.BlockSpec((BUCKET_ROWS, D), lambda i: (0, 0)),
            pl.BlockSpec((BUCKET_ROWS, 1), lambda i: (0, 0)),
            pl.BlockSpec((TM, TM), lambda i: (0, 0))]


def _mixer_a_kernel(x_tiles, n_x_tiles, x_ref, meta_ref, g_ref, win_ref, conv_ref, wout_ref,
                    fg_ref, wrt_ref, br_ref, tri_ref,
                    h1_ref, xtt_ref, ri_ref, cnt_ref, zs_ref, zmeta_ref, run_ref):
    i = pl.program_id(0)

    def conv_inputs(h):
        xn = _rms(h, g_ref[...], RMS_EPS).astype(BF16)
        bcu = jnp.dot(xn, win_ref[...], preferred_element_type=F32)
        return bcu, bcu[:, D:2 * D] * bcu[:, 2 * D:3 * D]

    @pl.when(i == 0)
    def _():
        run_ref[...] = jnp.zeros_like(run_ref)
        _, zm = conv_inputs(meta_ref[...])
        zmeta_ref[...] = zm[N_META - SUBLANES:N_META, :]

    @pl.when(i % x_tiles == 0)
    def _():
        zs_ref[0:SUBLANES, :] = zmeta_ref[...]

    @pl.when(i == n_x_tiles)
    def _():
        zs_ref[0:SUBLANES, :] = jnp.zeros((SUBLANES, D), F32)

    h = jnp.where(i < n_x_tiles, x_ref[...], meta_ref[...])
    bcu, z = conv_inputs(h)
    zs_ref[SUBLANES:SUBLANES + TM, :] = z
    cw = conv_ref[...]
    conv = (cw[0:1, :] * zs_ref[SUBLANES - 2:SUBLANES - 2 + TM, :]
            + cw[1:2, :] * zs_ref[SUBLANES - 1:SUBLANES - 1 + TM, :]
            + cw[2:3, :] * z)
    zs_ref[0:SUBLANES, :] = zs_ref[TM:TM + SUBLANES, :]
    mix = jnp.dot((bcu[:, 0:D] * conv).astype(BF16), wout_ref[...], preferred_element_type=F32)
    h1 = h + mix
    h1_ref[...] = h1
    _route_tail(h1, fg_ref, wrt_ref, br_ref, tri_ref, xtt_ref, ri_ref, cnt_ref, run_ref)


def _mixer_a(x2, metapad, g, w_in, conv_w, w_out, fg, wrt, br, tri, x_tiles):
    n_x_tiles = x2.shape[0] // TM
    tp = (n_x_tiles + 1) * TM
    return pl.pallas_call(
        functools.partial(_mixer_a_kernel, x_tiles, n_x_tiles),
        grid=(tp // TM,),
        in_specs=[pl.BlockSpec((TM, D), lambda i: (jnp.minimum(i, n_x_tiles - 1), 0)),
                  pl.BlockSpec((TM, D), lambda i: (0, 0)),
                  pl.BlockSpec((1, D), lambda i: (0, 0)),
                  pl.BlockSpec((D, 3 * D), lambda i: (0, 0)),
                  pl.BlockSpec((3, D), lambda i: (0, 0)),
                  pl.BlockSpec((D, D), lambda i: (0, 0))] + _route_in_specs(),
        out_specs=(pl.BlockSpec((TM, D), lambda i: (i, 0)),) + _route_out_specs(),
        out_shape=(jax.ShapeDtypeStruct((tp, D), F32),) + _route_out_shapes(tp),
        scratch_shapes=[pltpu.VMEM((TM + SUBLANES, D), F32),
                        pltpu.VMEM((SUBLANES, D), F32),
                        pltpu.VMEM((BUCKET_ROWS, LANES), F32)],
        compiler_params=pltpu.CompilerParams(dimension_semantics=("arbitrary",),
                                             vmem_limit_bytes=VMEM_LIMIT),
        name="mixer_a",
    )(x2, metapad, g, w_in, conv_w, w_out, fg, wrt, br, tri)


def _run_token_copies(copy):
    def start(g, c):
        for u in range(PERM_UNROLL):
            copy(g * PERM_UNROLL + u).start()
        return c

    def wait(g, c):
        for u in range(PERM_UNROLL):
            copy(g * PERM_UNROLL + u).wait()
        return c

    lax.fori_loop(0, TPERM // PERM_UNROLL, start, 0)
    lax.fori_loop(0, TPERM // PERM_UNROLL, wait, 0)


def _tok_rows(ref, t):
    return ref.at[pl.ds(pl.multiple_of(t * TOK_ROWS, TOK_ROWS), TOK_ROWS), :]


def _scatter_kernel(n_dst_tiles, ztile_ref, total_ref, b_ref, src_ref, dst_ref, zero_ref, zsem, sem):
    @pl.when(pl.program_id(0) == 0)
    def _():
        zero_ref[...] = jnp.zeros_like(zero_ref)

        def zero_tile(t):
            rows = TMM * TOK_ROWS
            return pltpu.make_async_copy(
                zero_ref, dst_ref.at[pl.ds(pl.multiple_of(t * rows, rows), rows), :], zsem)

        def each_zero_tile(fn):
            for b in range(N_BUCKETS):
                @pl.when(ztile_ref[b] >= 0)
                def _():
                    fn(zero_tile(ztile_ref[b]))

            def tail(t, c):
                fn(zero_tile(t))
                return c

            lax.fori_loop(total_ref[0], n_dst_tiles, tail, 0)

        each_zero_tile(lambda c: c.start())
        each_zero_tile(lambda c: c.wait())

    _run_token_copies(lambda k: pltpu.make_async_copy(
        _tok_rows(src_ref, k), _tok_rows(dst_ref, b_ref[0, 0, k]), sem))


def _scatter_tokens(src, b_idx, ztile, total, n_dst_tiles):
    steps = b_idx.shape[0] // TPERM
    grid_spec = pltpu.PrefetchScalarGridSpec(
        num_scalar_prefetch=2,
        grid=(steps,),
        in_specs=[pl.BlockSpec((1, 1, TPERM), lambda i, *_: (i, 0, 0), memory_space=pltpu.SMEM),
                  pl.BlockSpec((TPERM * TOK_ROWS, LANES), lambda i, *_: (i, 0))],
        out_specs=pl.BlockSpec(memory_space=pl.ANY),
        scratch_shapes=[pltpu.VMEM((TMM * TOK_ROWS, LANES), src.dtype),
                        pltpu.SemaphoreType.DMA, pltpu.SemaphoreType.DMA],
    )
    return pl.pallas_call(
        functools.partial(_scatter_kernel, n_dst_tiles),
        grid_spec=grid_spec,
        out_shape=jax.ShapeDtypeStruct((n_dst_tiles * TMM * TOK_ROWS, LANES), src.dtype),
        compiler_params=pltpu.CompilerParams(dimension_semantics=("arbitrary",)),
        name="scatter_tokens",
    )(ztile, total.reshape(1), b_idx.reshape(steps, 1, TPERM), src)


def _moe_kernel(blk_ref, ea_ref, eb_ref, nrow_ref, newa_ref, newb_ref,
                xs_ref, wga_f32, wua_f32, wda_f32, wgb_f32, wub_f32, wdb_f32, ys_ref,
                wga_ref, wua_ref, wda_ref, wgb_ref, wub_ref, wdb_ref):
    j = pl.program_id(0)
    nrow = nrow_ref[j]

    @pl.when(newa_ref[j] == 1)
    def _():
        for dst, src in ((wga_ref, wga_f32), (wua_ref, wua_f32), (wda_ref, wda_f32)):
            dst[...] = src[...].astype(BF16)

    @pl.when(newb_ref[j] == 1)
    def _():
        for dst, src in ((wgb_ref, wgb_f32), (wub_ref, wub_f32), (wdb_ref, wdb_f32)):
            dst[...] = src[...].astype(BF16)

    @pl.when(nrow > 0)
    def _():
        n_pk = (D // 2) // LANES
        words = _load_token_tiled(xs_ref, TMM, n_pk + 1)
        lo = [lax.bitcast_convert_type(w << 16, F32) for w in words[:n_pk]]
        hi = [lax.bitcast_convert_type(w & jnp.uint32(0xFFFF0000), F32) for w in words[:n_pk]]
        x = jnp.concatenate(lo + hi, axis=1).astype(BF16)
        wrow = lax.bitcast_convert_type(words[n_pk], F32)

        def expert(wg_ref, wu_ref, wd_ref, c):
            g = jnp.dot(x, wg_ref[...], preferred_element_type=F32)
            u = jnp.dot(x, wu_ref[...], preferred_element_type=F32)
            hmid = (jax.nn.silu(g) * u) * c
            return jnp.dot(hmid.astype(BF16), wd_ref[...], preferred_element_type=F32)

        y = expert(wga_ref, wua_ref, wda_ref, wrow[:, 0:1]) + expert(wgb_ref, wub_ref, wdb_ref, wrow[:, 1:2])
        _store_token_tiled(ys_ref, [y[:, r * LANES:(r + 1) * LANES] for r in range(XCHUNKS)], TMM)

    @pl.when(nrow == 0)
    def _():
        ys_ref[...] = jnp.zeros_like(ys_ref)


def _moe(xs, w_gate, w_up, w_down, layer, blk, ea, eb, nrow):
    n_tiles = blk.shape[0]
    p_tok = xs.shape[0] // TOK_ROWS
    first = jnp.ones((1,), I32)
    newa = jnp.concatenate([first, (ea[1:] != ea[:-1]).astype(I32)])
    newb = jnp.concatenate([first, (eb[1:] != eb[:-1]).astype(I32)])
    tok_spec = pl.BlockSpec((TMM * TOK_ROWS, LANES), lambda j, blk, *_: (blk[j], 0))

    def wspec(shape, which):
        if which == 0:
            return pl.BlockSpec((None, None) + shape, lambda j, blk, ea, eb, *_: (layer, ea[j], 0, 0))
        return pl.BlockSpec((None, None) + shape, lambda j, blk, ea, eb, *_: (layer, eb[j], 0, 0))

    w_in, w_out = (D, D_EXPERT), (D_EXPERT, D)
    grid_spec = pltpu.PrefetchScalarGridSpec(
        num_scalar_prefetch=6,
        grid=(n_tiles,),
        in_specs=[tok_spec, wspec(w_in, 0), wspec(w_in, 0), wspec(w_out, 0),
                  wspec(w_in, 1), wspec(w_in, 1), wspec(w_out, 1)],
        out_specs=pl.BlockSpec((TMM * TOK_ROWS, LANES), lambda j, *_: (j, 0)),
        scratch_shapes=[pltpu.VMEM(w_in, BF16), pltpu.VMEM(w_in, BF16), pltpu.VMEM(w_out, BF16)] * 2,
    )
    return pl.pallas_call(
        _moe_kernel,
        grid_spec=grid_spec,
        out_shape=jax.ShapeDtypeStruct((p_tok * TOK_ROWS, LANES), F32),
        compiler_params=pltpu.CompilerParams(dimension_semantics=("arbitrary",),
                                             vmem_limit_bytes=VMEM_LIMIT),
        name="moe_experts",
    )(blk, ea, eb, nrow, newa, newb, xs, w_gate, w_up, w_down, w_gate, w_up, w_down)


def _count_le(ends, v):
    return jnp.sum((ends[None, :] <= v[:, None]).astype(I32), axis=1)


def _lookup(table, idx):
    hit = idx[:, None] == jnp.arange(table.shape[0], dtype=I32)[None, :]
    return jnp.sum(jnp.where(hit, table[None, :], 0), axis=1)


def _moe_layer(xtt, ri, cnt, w_gate, w_up, w_down, layer, tp):
    counts = cnt[:N_BUCKETS, 0].astype(I32)
    ntile = (counts + TMM - 1) // TMM
    tend = jnp.cumsum(ntile)
    tstart = tend - ntile
    total = tend[-1]
    pos = _lookup(tstart * TMM, ri[0]) + ri[1]

    n_tiles = -(-tp // TMM) + N_BUCKETS
    j = jnp.arange(n_tiles, dtype=I32)
    blk = jnp.minimum(j, total - 1)
    tb = jnp.minimum(_count_le(tend, blk), N_BUCKETS - 1)
    grp = tb // N_PAIRS
    pair = tb % N_PAIRS
    ea = grp * EPG + _lookup(jnp.asarray(PAIR_LO, I32), pair)
    eb = grp * EPG + _lookup(jnp.asarray(PAIR_HI, I32), pair)
    nrow = jnp.clip(_lookup(counts, tb) - (j - _lookup(tstart, tb)) * TMM, 0, TMM)
    nrow = jnp.where(j < total, nrow, 0).astype(I32)
    ztile = jnp.where(ntile > 0, tend - 1, -1)

    xs = _scatter_tokens(xtt, pos, ztile, total, n_tiles)
    ys = _moe(xs, w_gate, w_up, w_down, layer, blk, ea, eb, nrow)
    return ys, pos.reshape(tp // TM, 1, TM)


GATHER_AHEAD = 2
GATHER_BUFS = GATHER_AHEAD + 1


def _gather_expert_rows(pos_refs, ys_ref, buf_ref, sem):
    i = pl.program_id(0)
    n = pl.num_programs(0)

    def copies(idx_ref, s):
        return [pltpu.make_async_copy(_tok_rows(ys_ref, idx_ref[0, 0, k]),
                                      buf_ref.at[s, pl.ds(k * TOK_ROWS, TOK_ROWS), :], sem.at[s])
                for k in range(TM)]

    @pl.when(i == 0)
    def _():
        for d in range(GATHER_AHEAD):
            for c in copies(pos_refs[d], d):
                c.start()

    slot = i % GATHER_BUFS
    for c in copies(pos_refs[0], slot):
        c.wait()
    y = jnp.concatenate([buf_ref[slot, pl.ds(r, TM, stride=TOK_ROWS), :] for r in range(XCHUNKS)], axis=1)
    for c in copies(pos_refs[GATHER_AHEAD], (i + GATHER_AHEAD) % GATHER_BUFS):
        c.start()

    def drain():
        @pl.when(i == n - 1)
        def _():
            for d in range(1, GATHER_BUFS):
                for c in copies(pos_refs[0], (i + d) % GATHER_BUFS):
                    c.wait()

    return y, drain


def _gather_specs(n_steps):
    assert n_steps > GATHER_AHEAD
    ahead = lambda d: pl.BlockSpec((1, 1, TM), lambda i: (jnp.minimum(i + d, n_steps - 1), 0, 0),
                                   memory_space=pltpu.SMEM)
    return [ahead(d) for d in range(GATHER_BUFS)] + [pl.BlockSpec(memory_space=pl.ANY)]


GATHER_SCRATCH = [pltpu.VMEM((GATHER_BUFS, TM * TOK_ROWS, LANES), F32),
                  pltpu.SemaphoreType.DMA((GATHER_BUFS,))]


def _attn_proj_kernel(h_ref, pos0_ref, pos1_ref, pos2_ref, ys_ref, gkv_ref, gq_ref, wk_ref, wvt_ref,
                      wqt_ref, h2_ref, qt_ref, k_ref, vt_ref, buf_ref, sem):
    y, drain = _gather_expert_rows((pos0_ref, pos1_ref, pos2_ref), ys_ref, buf_ref, sem)
    h2 = h_ref[...] + y
    h2_ref[...] = h2
    ms = jnp.mean(h2 * h2, axis=-1, keepdims=True)
    xhat = h2 * lax.rsqrt(ms + RMS_EPS)
    xkv = (xhat * gkv_ref[...]).astype(BF16)
    xq = (xhat * gq_ref[...]).astype(BF16)
    nt = (((1,), (1,)), ((), ()))
    k_ref[...] = jnp.dot(xkv, wk_ref[...], preferred_element_type=F32).astype(BF16)
    vt_ref[...] = lax.dot_general(wvt_ref[...], xkv, nt, preferred_element_type=F32).astype(BF16)
    qt = lax.dot_general(wqt_ref[...], xq, nt, preferred_element_type=F32)
    qt_ref[...] = (qt * (LOG2E * HEAD_DIM ** -0.5)).astype(BF16)
    drain()


def _attn_proj(h, pos3, ys, gkv, gq, wk, wvt, wqt):
    tp = h.shape[0]
    steps = tp // TM
    row = pl.BlockSpec((TM, D), lambda i: (i, 0))
    col = pl.BlockSpec((D, TM), lambda i: (0, i))
    vec = pl.BlockSpec((1, D), lambda i: (0, 0))
    mat = pl.BlockSpec((D, D), lambda i: (0, 0))
    return pl.pallas_call(
        _attn_proj_kernel,
        grid=(steps,),
        in_specs=[row] + _gather_specs(steps) + [vec, vec, mat, mat, mat],
        out_specs=(row, col, row, col),
        out_shape=(jax.ShapeDtypeStruct((tp, D), F32), jax.ShapeDtypeStruct((D, tp), BF16),
                   jax.ShapeDtypeStruct((tp, D), BF16), jax.ShapeDtypeStruct((D, tp), BF16)),
        scratch_shapes=GATHER_SCRATCH,
        compiler_params=pltpu.CompilerParams(dimension_semantics=("arbitrary",),
                                             vmem_limit_bytes=VMEM_LIMIT),
        name="attn_proj",
    )(h, *([pos3] * GATHER_BUFS), ys, gkv, gq, wk, wvt, wqt)


def _attn_kernel(lambda_init, qt_ref, k_ref, km_ref, vt_ref, vtm_ref, lam_ref, sg_ref, o_ref,
                 s_ref, m_ref, l_ref, acc_ref):
    hw = 2 * HEAD_DIM
    n_x = o_ref.shape[0]
    half = QB // 2
    assert QB == KB and n_x % QB == 0 and N_META <= LANES
    lam = lam_ref[...]
    lam_full = (jnp.exp(jnp.sum(lam[0:1, :] * lam[1:2, :], axis=1, keepdims=True))
                - jnp.exp(jnp.sum(lam[2:3, :] * lam[3:4, :], axis=1, keepdims=True)) + lambda_init)

    diag = ((0, half, 0, QB), (half, half, half, half))
    qw = QB

    def blocks(r, fn):
        for kb in range(r):
            fn(kb * KB, KB, 0, qw, None)
        for ko, kw, qo, qn in diag:
            fn(r * QB + ko, kw, qo, qn, lambda krow, qcol, d=ko - qo: krow + d <= qcol)
        fn(0, LANES, 0, qw, lambda krow, qcol: krow < N_META, meta=True)

    def score_pass(r):
        s_par, m_par = s_ref.at[r % 2], m_ref.at[r % 2]
        qt = qt_ref[:, r * QB:(r + 1) * QB]
        frow = lax.broadcasted_iota(I32, (hw, qw), 0)
        zero = jnp.zeros_like(qt)
        qc = (jnp.where(frow < HEAD_DIM, qt, zero), jnp.where(frow >= HEAD_DIM, qt, zero))
        m_par[...] = jnp.full(m_par.shape, -jnp.inf, F32)

        def scores(k0, kw, qo, qn, visible, meta=False):
            kblk = km_ref[0:kw, :] if meta else k_ref[pl.ds(k0, kw), :]
            k0 = n_x if meta else k0
            for c in range(2):
                s = jnp.dot(kblk, qc[c][:, qo:qo + qn], preferred_element_type=F32)
                if visible is not None:
                    krow = lax.broadcasted_iota(I32, (kw, qn), 0)
                    qcol = lax.broadcasted_iota(I32, (kw, qn), 1)
                    s = jnp.where(visible(krow, qcol), s, -jnp.inf)
                s_par[c, pl.ds(k0, kw), qo:qo + qn] = s
                smax = jnp.max(s.reshape(kw // SUBLANES, SUBLANES, qn), axis=0)
                m_par[c, :, qo:qo + qn] = jnp.maximum(m_par[c, :, qo:qo + qn], smax)

        blocks(r, scores)

    def value_pass(r):
        s_par, m_par = s_ref.at[r % 2], m_ref.at[r % 2]
        m = [jnp.max(m_par[c], axis=0, keepdims=True) for c in range(2)]
        l_ref[...] = jnp.zeros(l_ref.shape, F32)
        acc_ref[...] = jnp.zeros(acc_ref.shape, F32)

        def weighted_values(k0, kw, qo, qn, visible, meta=False):
            del visible
            vblk = vtm_ref[:, 0:kw] if meta else vt_ref[:, pl.ds(k0, kw)]
            k0 = n_x if meta else k0
            for c in range(2):
                p = jnp.exp2(s_par[c, pl.ds(k0, kw), qo:qo + qn] - m[c][:, qo:qo + qn])
                acc_ref[c, :, qo:qo + qn] += jnp.dot(vblk, p.astype(BF16), preferred_element_type=F32)
                l_ref[c, :, qo:qo + qn] += jnp.sum(p.reshape(kw // SUBLANES, SUBLANES, qn), axis=0)

        blocks(r, weighted_values)
        l = [jnp.sum(l_ref[c], axis=0, keepdims=True) for c in range(2)]
        ot = acc_ref[0] / l[0] - lam_full * (acc_ref[1] / l[1])
        ms = jnp.mean(ot * ot, axis=0, keepdims=True)
        y = ot * lax.rsqrt(ms + SUBLN_EPS) * sg_ref[...] * (1.0 - lambda_init)
        o_ref[r * QB:(r + 1) * QB, :] = y.T.astype(BF16)

    n_blocks = n_x // QB
    score_pass(0)
    for r in range(n_blocks):
        if r + 1 < n_blocks:
            score_pass(r + 1)
        value_pass(r)


def _attention(qt, k, vt, lam, sg, lambda_init, bsz, n_x):
    hw = 2 * HEAD_DIM
    meta_blk = bsz * n_x // TM
    fmaj = pl.BlockSpec((hw, n_x), lambda b, h: (h, b))
    tmaj = pl.BlockSpec((n_x, hw), lambda b, h: (b, h))
    return pl.pallas_call(
        functools.partial(_attn_kernel, lambda_init),
        grid=(bsz, N_HEADS),
        in_specs=[fmaj, tmaj, pl.BlockSpec((TM, hw), lambda b, h: (meta_blk, h)),
                  fmaj, pl.BlockSpec((hw, TM), lambda b, h: (h, meta_blk)),
                  pl.BlockSpec((4, HEAD_DIM), lambda b, h: (0, 0)),
                  pl.BlockSpec((hw, 1), lambda b, h: (0, 0))],
        out_specs=tmaj,
        out_shape=jax.ShapeDtypeStruct((bsz * n_x, D), BF16),
        scratch_shapes=[pltpu.VMEM((2, 2, n_x + LANES, QB), F32), pltpu.VMEM((2, 2, SUBLANES, QB), F32),
                        pltpu.VMEM((2, SUBLANES, QB), F32), pltpu.VMEM((2, hw, QB), F32)],
        compiler_params=pltpu.CompilerParams(dimension_semantics=("arbitrary",) * 2,
                                             vmem_limit_bytes=VMEM_LIMIT),
        name="diff_attention",
    )(qt, k, k, vt, vt, lam, sg)


def _attn_out_kernel(h_ref, o_ref, wo_ref, fg_ref, wrt_ref, br_ref, tri_ref,
                     h3_ref, xtt_ref, ri_ref, cnt_ref, run_ref):
    @pl.when(pl.program_id(0) == 0)
    def _():
        run_ref[...] = jnp.zeros_like(run_ref)

    h3 = h_ref[...] + jnp.dot(o_ref[...], wo_ref[...], preferred_element_type=F32)
    h3_ref[...] = h3
    _route_tail(h3, fg_ref, wrt_ref, br_ref, tri_ref, xtt_ref, ri_ref, cnt_ref, run_ref)


def _attn_out(h, o, wo, fg, wrt, br, tri):
    tp = h.shape[0]
    last_o = o.shape[0] // TM - 1
    row = pl.BlockSpec((TM, D), lambda i: (i, 0))
    return pl.pallas_call(
        _attn_out_kernel,
        grid=(tp // TM,),
        in_specs=[row, pl.BlockSpec((TM, D), lambda i: (jnp.minimum(i, last_o), 0)),
                  pl.BlockSpec((D, D), lambda i: (0, 0))] + _route_in_specs(),
        out_specs=(row,) + _route_out_specs(),
        out_shape=(jax.ShapeDtypeStruct((tp, D), F32),) + _route_out_shapes(tp),
        scratch_shapes=[pltpu.VMEM((BUCKET_ROWS, LANES), F32)],
        compiler_params=pltpu.CompilerParams(dimension_semantics=("arbitrary",),
                                             vmem_limit_bytes=VMEM_LIMIT),
        name="attn_out",
    )(h, o, wo, fg, wrt, br, tri)


def _final_kernel(h_ref, pos0_ref, pos1_ref, pos2_ref, ys_ref, g_ref, o_ref, buf_ref, sem):
    y, drain = _gather_expert_rows((pos0_ref, pos1_ref, pos2_ref), ys_ref, buf_ref, sem)
    o_ref[...] = _rms(h_ref[...] + y, g_ref[...], RMS_EPS)
    drain()


def _final(h, pos3, ys, g, n_x_tiles):
    row = pl.BlockSpec((TM, D), lambda i: (i, 0))
    return pl.pallas_call(
        _final_kernel,
        grid=(n_x_tiles,),
        in_specs=[row] + _gather_specs(n_x_tiles) + [pl.BlockSpec((1, D), lambda i: (0, 0))],
        out_specs=row,
        out_shape=jax.ShapeDtypeStruct((n_x_tiles * TM, D), F32),
        scratch_shapes=GATHER_SCRATCH,
        compiler_params=pltpu.CompilerParams(dimension_semantics=("arbitrary",),
                                             vmem_limit_bytes=VMEM_LIMIT),
        name="final_norm",
    )(h, *([pos3] * GATHER_BUFS), ys, g)


def _router_params(w_rg, b_rg, w_re, b_re):
    wr = jnp.concatenate([w_rg, w_re], axis=1)
    wrt = jnp.zeros((BUCKET_ROWS, D), F32).at[:N_GROUPS + N_EXPERTS].set(wr.T).astype(BF16)
    br = jnp.zeros((BUCKET_ROWS, 1), F32).at[:N_GROUPS + N_EXPERTS, 0].set(jnp.concatenate([b_rg, b_re]))
    return wrt, br


def kernel(x, meta_tokens, a_norm, a_w_in, a_conv, a_w_out, kv_norm, w_kv, b_norm, b_w_q, b_lambda, b_subln, b_w_o, ffn_norm, router_group_w, router_group_b, router_expert_w, router_expert_b, expert_w_gate, expert_w_up, expert_w_down, final_norm):
    bsz, seq, d = x.shape
    assert d == D and a_norm.shape[0] == 1 and b_norm.shape[0] == 1
    assert meta_tokens.shape[0] == N_META and seq % TM == 0
    x_tiles = seq // TM
    n_x_tiles = bsz * x_tiles
    tp = (n_x_tiles + 1) * TM
    assert tp % TPERM == 0

    metapad = jnp.concatenate([meta_tokens.astype(x.dtype), jnp.zeros((TM - N_META, D), x.dtype)])
    tri = jnp.triu(jnp.ones((TM, TM), F32)).astype(BF16)
    bf = lambda w: w.astype(BF16)

    wrt, br = _router_params(router_group_w[0], router_group_b[0], router_expert_w[0], router_expert_b[0])
    h, xtt, ri, cnt = _mixer_a(x.reshape(bsz * seq, D), metapad, a_norm[0][None], bf(a_w_in[0]), a_conv[0],
                               bf(a_w_out[0]), ffn_norm[0][None], wrt, br, tri, x_tiles)
    ys, pos3 = _moe_layer(xtt, ri, cnt, expert_w_gate, expert_w_up, expert_w_down, 0, tp)

    h, qt, k, vt = _attn_proj(h, pos3, ys, kv_norm[None], b_norm[0][None],
                              bf(w_kv[:, :D]), bf(w_kv[:, D:].T), bf(b_w_q[0].T))
    o = _attention(qt, k, vt, b_lambda[0], b_subln[0][:, None], _lambda_init(1), bsz, seq)
    wrt, br = _router_params(router_group_w[1], router_group_b[1], router_expert_w[1], router_expert_b[1])
    h, xtt, ri, cnt = _attn_out(h, o, bf(b_w_o[0]), ffn_norm[1][None], wrt, br, tri)
    ys, pos3 = _moe_layer(xtt, ri, cnt, expert_w_gate, expert_w_up, expert_w_down, 1, tp)

    out = _final(h, pos3, ys, final_norm[None], n_x_tiles)
    return out.reshape(bsz, seq, D)
```

```python
import functools
import math

import jax
import jax.numpy as jnp
from jax import lax
from jax.experimental import pallas as pl
from jax.experimental.pallas import tpu as pltpu

F32 = jnp.float32
BF16 = jnp.bfloat16
I32 = jnp.int32
U32 = jnp.uint32

D = 1024
N_META = 16
Q_BLOCK = 128
HEAD_DIM = 64
N_HEADS = D // (2 * HEAD_DIM)
N_GROUPS = 4
EPG = 4
N_EXPERTS = N_GROUPS * EPG
D_EXPERT = D // 2
RMS_EPS = 1e-6
SUBLN_EPS = 1e-5
LOG2E = math.log2(math.e)

LANES = 128
SUBLANES = 8
TOK_ROWS = SUBLANES
XCHUNKS = D // LANES

TM = 512
QB = 512
KB = 512
TMM = 512
PERM_UNROLL = 8
N_PAIRS = 6
N_BUCKETS = N_GROUPS * N_PAIRS
BUCKET_ROWS = 32
PAIR_LO = (0, 0, 0, 1, 1, 2)
PAIR_HI = (1, 2, 3, 2, 3, 3)
VMEM_LIMIT = 56 * 1024 * 1024


def _lambda_init(layer_idx):
    return 0.8 - 0.6 * math.exp(-0.3 * layer_idx)


def _rms(x, g, eps):
    ms = jnp.mean(x * x, axis=-1, keepdims=True)
    return x * lax.rsqrt(ms + eps) * g


def _load_token_tiled(ref, n_tok, n_chunks):
    return [ref[pl.ds(r, n_tok, stride=TOK_ROWS), :] for r in range(n_chunks)]


def _store_token_tiled(ref, chunks, n_tok):
    for r, c in enumerate(chunks):
        ref[pl.ds(r, n_tok, stride=TOK_ROWS), :] = c


def _route_tail(h, fg_ref, wrt_ref, br_ref, tri_ref, xtt_ref, ri_ref, cnt_ref, run_ref):
    tm = h.shape[0]
    xn = _rms(h, fg_ref[...], RMS_EPS)
    xb = xn.astype(BF16)

    lt = lax.dot_general(wrt_ref[...], xb, (((1,), (1,)), ((), ())), preferred_element_type=F32)
    lt = lt + br_ref[...]
    lg = [lt[k:k + 1, :] for k in range(N_GROUPS)]
    m = jnp.maximum(jnp.maximum(lg[0], lg[1]), jnp.maximum(lg[2], lg[3]))
    gidx = jnp.where(lg[0] == m, 0, jnp.where(lg[1] == m, 1, jnp.where(lg[2] == m, 2, 3))).astype(I32)
    se = jnp.exp(lg[0] - m) + jnp.exp(lg[1] - m) + jnp.exp(lg[2] - m) + jnp.exp(lg[3] - m)
    p_sel = 1.0 / se

    def le_row(g, j):
        r = N_GROUPS + g * EPG + j
        return lt[r:r + 1, :]

    sel = [jnp.where(gidx == 0, le_row(0, j),
                     jnp.where(gidx == 1, le_row(1, j),
                               jnp.where(gidx == 2, le_row(2, j), le_row(3, j)))) for j in range(EPG)]

    def first_argmax(vals):
        v = jnp.maximum(jnp.maximum(vals[0], vals[1]), jnp.maximum(vals[2], vals[3]))
        i = jnp.where(vals[0] == v, 0, jnp.where(vals[1] == v, 1, jnp.where(vals[2] == v, 2, 3))).astype(I32)
        return v, i

    v1, i1 = first_argmax(sel)
    sel2 = [jnp.where(i1 == j, -jnp.inf, sel[j]) for j in range(EPG)]
    v2, i2 = first_argmax(sel2)
    e2 = jnp.exp(v2 - v1)
    den = 1.0 + e2
    w1 = (1.0 / den) * p_sel
    w2 = (e2 / den) * p_sel
    lo = jnp.minimum(i1, i2)
    hi = jnp.maximum(i1, i2)
    first_is_lo = i1 < i2
    w_lo = jnp.where(first_is_lo, w1, w2)
    w_hi = jnp.where(first_is_lo, w2, w1)
    pair = jnp.where(lo == 0, hi - 1, jnp.where(lo == 1, hi + 1, 5))
    bucket = gidx * N_PAIRS + pair

    rows = lax.broadcasted_iota(I32, (BUCKET_ROWS, tm), 0)
    ohf = (rows == bucket).astype(F32)
    cum = jnp.dot(ohf.astype(BF16), tri_ref[...], preferred_element_type=F32)
    run = run_ref[:, 0:1]
    rank = jnp.sum(ohf * (cum - 1.0 + run), axis=0, keepdims=True)
    run_ref[...] = run_ref[...] + jnp.sum(ohf, axis=1, keepdims=True)
    cnt_ref[...] = run_ref[...]

    ri_ref[0:1, :] = bucket
    ri_ref[1:2, :] = rank.astype(I32)
    ri_ref[2:SUBLANES, :] = jnp.zeros((SUBLANES - 2, tm), I32)

    wt = jnp.concatenate([w_lo, w_hi, jnp.zeros((LANES - 2, tm), F32)], axis=0).T

    bits = lax.bitcast_convert_type(xb.astype(F32), U32)
    half = D // 2
    packed = (bits[:, :half] >> 16) | (bits[:, half:] & jnp.uint32(0xFFFF0000))
    n_pk = half // LANES
    chunks = [packed[:, r * LANES:(r + 1) * LANES] for r in range(n_pk)]
    chunks.append(lax.bitcast_convert_type(wt, U32))
    chunks += [jnp.zeros((tm, LANES), U32)] * (TOK_ROWS - n_pk - 1)
    _store_token_tiled(xtt_ref, chunks, tm)


def _route_out_shapes(tp):
    return (jax.ShapeDtypeStruct((tp * TOK_ROWS, LANES), U32),
            jax.ShapeDtypeStruct((SUBLANES, tp), I32),
            jax.ShapeDtypeStruct((BUCKET_ROWS, LANES), F32))


def _route_out_specs():
    return (pl.BlockSpec((TM * TOK_ROWS, LANES), lambda i: (i, 0)),
            pl.BlockSpec((SUBLANES, TM), lambda i: (0, i)),
            pl.BlockSpec((BUCKET_ROWS, LANES), lambda i: (0, 0)))


def _route_in_specs():
    return [pl.BlockSpec((1, D), lambda i: (0, 0)),
            pl.BlockSpec((BUCKET_ROWS, D), lambda i: (0, 0)),
            pl.BlockSpec((BUCKET_ROWS, 1), lambda i: (0, 0)),
            pl.BlockSpec((TM, TM), lambda i: (0, 0))]


def _mixer_a_kernel(x_tiles, n_x_tiles, x_ref, meta_ref, g_ref, win_ref, conv_ref, wout_ref,
                    fg_ref, wrt_ref, br_ref, tri_ref,
                    h1_ref, xtt_ref, ri_ref, cnt_ref, zs_ref, zmeta_ref, run_ref):
    i = pl.program_id(0)

    def conv_inputs(h):
        xn = _rms(h, g_ref[...], RMS_EPS).astype(BF16)
        bcu = jnp.dot(xn, win_ref[...], preferred_element_type=F32)
        return bcu, bcu[:, D:2 * D] * bcu[:, 2 * D:3 * D]

    @pl.when(i == 0)
    def _():
        run_ref[...] = jnp.zeros_like(run_ref)
        _, zm = conv_inputs(meta_ref[...])
        zmeta_ref[...] = zm[N_META - SUBLANES:N_META, :]

    @pl.when(i % x_tiles == 0)
    def _():
        zs_ref[0:SUBLANES, :] = zmeta_ref[...]

    @pl.when(i == n_x_tiles)
    def _():
        zs_ref[0:SUBLANES, :] = jnp.zeros((SUBLANES, D), F32)

    h = jnp.where(i < n_x_tiles, x_ref[...], meta_ref[...])
    bcu, z = conv_inputs(h)
    zs_ref[SUBLANES:SUBLANES + TM, :] = z
    cw = conv_ref[...]
    conv = (cw[0:1, :] * zs_ref[SUBLANES - 2:SUBLANES - 2 + TM, :]
            + cw[1:2, :] * zs_ref[SUBLANES - 1:SUBLANES - 1 + TM, :]
            + cw[2:3, :] * z)
    zs_ref[0:SUBLANES, :] = zs_ref[TM:TM + SUBLANES, :]
    mix = jnp.dot((bcu[:, 0:D] * conv).astype(BF16), wout_ref[...], preferred_element_type=F32)
    h1 = h + mix
    h1_ref[...] = h1
    _route_tail(h1, fg_ref, wrt_ref, br_ref, tri_ref, xtt_ref, ri_ref, cnt_ref, run_ref)


def _mixer_a(x2, metapad, g, w_in, conv_w, w_out, fg, wrt, br, tri, x_tiles):
    n_x_tiles = x2.shape[0] // TM
    tp = (n_x_tiles + 1) * TM
    return pl.pallas_call(
        functools.partial(_mixer_a_kernel, x_tiles, n_x_tiles),
        grid=(tp // TM,),
        in_specs=[pl.BlockSpec((TM, D), lambda i: (jnp.minimum(i, n_x_tiles - 1), 0)),
                  pl.BlockSpec((TM, D), lambda i: (0, 0)),
                  pl.BlockSpec((1, D), lambda i: (0, 0)),
                  pl.BlockSpec((D, 3 * D), lambda i: (0, 0)),
                  pl.BlockSpec((3, D), lambda i: (0, 0)),
                  pl.BlockSpec((D, D), lambda i: (0, 0))] + _route_in_specs(),
        out_specs=(pl.BlockSpec((TM, D), lambda i: (i, 0)),) + _route_out_specs(),
        out_shape=(jax.ShapeDtypeStruct((tp, D), F32),) + _route_out_shapes(tp),
        scratch_shapes=[pltpu.VMEM((TM + SUBLANES, D), F32),
                        pltpu.VMEM((SUBLANES, D), F32),
                        pltpu.VMEM((BUCKET_ROWS, LANES), F32)],
        compiler_params=pltpu.CompilerParams(dimension_semantics=("arbitrary",),
                                             vmem_limit_bytes=VMEM_LIMIT),
        name="mixer_a",
    )(x2, metapad, g, w_in, conv_w, w_out, fg, wrt, br, tri)


GATHER_AHEAD = 2
GATHER_BUFS = GATHER_AHEAD + 1


def _tok_rows(ref, t):
    return ref.at[pl.ds(pl.multiple_of(t * TOK_ROWS, TOK_ROWS), TOK_ROWS), :]


def _gather_token_tile(idx_refs, src_ref, buf_ref, sem, n_tok, n_chunks, last_step, split_queues):
    i = pl.program_id(0)

    def copies(idx_ref, s):
        return [pltpu.make_async_copy(_tok_rows(src_ref, idx_ref[0, 0, k]),
                                      buf_ref.at[s, pl.ds(k * TOK_ROWS, TOK_ROWS), :], sem.at[s])
                for k in range(n_tok)]

    def start_all(cs):
        for k, c in enumerate(cs):
            c.start(priority=k % 2 if split_queues else 0)

    @pl.when(i == 0)
    def _():
        for d in range(GATHER_AHEAD):
            start_all(copies(idx_refs[d], d))

    slot = i % GATHER_BUFS
    for c in copies(idx_refs[0], slot):
        c.wait()
    chunks = [buf_ref[slot, pl.ds(r, n_tok, stride=TOK_ROWS), :] for r in range(n_chunks)]
    start_all(copies(idx_refs[GATHER_AHEAD], (i + GATHER_AHEAD) % GATHER_BUFS))

    def drain():
        @pl.when(i == last_step)
        def _():
            for d in range(1, GATHER_BUFS):
                for c in copies(idx_refs[0], (i + d) % GATHER_BUFS):
                    c.wait()

    return chunks, drain


def _gather_scratch(n_tok, dtype):
    return [pltpu.VMEM((GATHER_BUFS, n_tok * TOK_ROWS, LANES), dtype),
            pltpu.SemaphoreType.DMA((GATHER_BUFS,))]


def _slot_token_kernel(pos_ref, inv_ref):
    i = pl.program_id(0)

    @pl.when(i == 0)
    def _():
        def clear(g, c):
            for u in range(PERM_UNROLL):
                inv_ref[g * PERM_UNROLL + u] = 0
            return c

        lax.fori_loop(0, inv_ref.shape[0] // PERM_UNROLL, clear, 0)

    def place(g, c):
        for u in range(PERM_UNROLL):
            k = g * PERM_UNROLL + u
            inv_ref[pos_ref[0, 0, k]] = i * TM + k
        return c

    lax.fori_loop(0, TM // PERM_UNROLL, place, 0)


def _slot_tokens(pos3, n_slots):
    assert n_slots % PERM_UNROLL == 0
    return pl.pallas_call(
        _slot_token_kernel,
        grid=(pos3.shape[0],),
        in_specs=[pl.BlockSpec((1, 1, TM), lambda i: (i, 0, 0), memory_space=pltpu.SMEM)],
        out_specs=pl.BlockSpec(memory_space=pltpu.SMEM),
        out_shape=jax.ShapeDtypeStruct((n_slots,), I32),
        compiler_params=pltpu.CompilerParams(dimension_semantics=("arbitrary",)),
        name="slot_tokens",
    )(pos3)


def _moe_kernel(ea_ref, eb_ref, nrow_ref, newa_ref, newb_ref, total_ref,
                inv0_ref, inv1_ref, inv2_ref, xtt_ref,
                wga_f32, wua_f32, wda_f32, wgb_f32, wub_f32, wdb_f32, ys_ref,
                buf_ref, sem, wga_ref, wua_ref, wda_ref, wgb_ref, wub_ref, wdb_ref):
    j = pl.program_id(0)
    nrow = nrow_ref[j]

    @pl.when(newa_ref[j] == 1)
    def _():
        for dst, src in ((wga_ref, wga_f32), (wua_ref, wua_f32), (wda_ref, wda_f32)):
            dst[...] = src[...].astype(BF16)

    @pl.when(newb_ref[j] == 1)
    def _():
        for dst, src in ((wgb_ref, wgb_f32), (wub_ref, wub_f32), (wdb_ref, wdb_f32)):
            dst[...] = src[...].astype(BF16)

    @pl.when(nrow > 0)
    def _():
        n_pk = (D // 2) // LANES
        words, drain = _gather_token_tile((inv0_ref, inv1_ref, inv2_ref), xtt_ref, buf_ref, sem,
                                          TMM, n_pk + 1, total_ref[0] - 1, False)
        row_ok = lax.broadcasted_iota(I32, (TMM, 1), 0) < nrow
        lo = [lax.bitcast_convert_type(w << 16, F32) for w in words[:n_pk]]
        hi = [lax.bitcast_convert_type(w & jnp.uint32(0xFFFF0000), F32) for w in words[:n_pk]]
        x = jnp.where(row_ok, jnp.concatenate(lo + hi, axis=1), 0.0).astype(BF16)
        wrow = jnp.where(row_ok, lax.bitcast_convert_type(words[n_pk], F32), 0.0)

        def expert(wg_ref, wu_ref, wd_ref, c):
            g = jnp.dot(x, wg_ref[...], preferred_element_type=F32)
            u = jnp.dot(x, wu_ref[...], preferred_element_type=F32)
            hmid = (jax.nn.silu(g) * u) * c
            return jnp.dot(hmid.astype(BF16), wd_ref[...], preferred_element_type=F32)

        y = expert(wga_ref, wua_ref, wda_ref, wrow[:, 0:1]) + expert(wgb_ref, wub_ref, wdb_ref, wrow[:, 1:2])
        _store_token_tiled(ys_ref, [y[:, r * LANES:(r + 1) * LANES] for r in range(XCHUNKS)], TMM)
        drain()

    @pl.when(nrow == 0)
    def _():
        ys_ref[...] = jnp.zeros_like(ys_ref)


def _moe(xtt, inv3, w_gate, w_up, w_down, layer, ea, eb, nrow, total):
    n_tiles = ea.shape[0]
    first = jnp.ones((1,), I32)
    newa = jnp.concatenate([first, (ea[1:] != ea[:-1]).astype(I32)])
    newb = jnp.concatenate([first, (eb[1:] != eb[:-1]).astype(I32)])

    def ahead(d):
        return pl.BlockSpec((1, 1, TMM), lambda j, ea, eb, nr, na, nb, tot: (jnp.minimum(j + d, tot[0] - 1), 0, 0),
                            memory_space=pltpu.SMEM)

    def wspec(shape, which):
        if which == 0:
            return pl.BlockSpec((None, None) + shape, lambda j, ea, *_: (layer, ea[j], 0, 0))
        return pl.BlockSpec((None, None) + shape, lambda j, ea, eb, *_: (layer, eb[j], 0, 0))

    w_in, w_out = (D, D_EXPERT), (D_EXPERT, D)
    grid_spec = pltpu.PrefetchScalarGridSpec(
        num_scalar_prefetch=6,
        grid=(n_tiles,),
        in_specs=[ahead(d) for d in range(GATHER_BUFS)] + [pl.BlockSpec(memory_space=pl.ANY)]
                 + [wspec(w_in, 0), wspec(w_in, 0), wspec(w_out, 0),
                    wspec(w_in, 1), wspec(w_in, 1), wspec(w_out, 1)],
        out_specs=pl.BlockSpec((TMM * TOK_ROWS, LANES), lambda j, *_: (j, 0)),
        scratch_shapes=_gather_scratch(TMM, U32)
                       + [pltpu.VMEM(w_in, BF16), pltpu.VMEM(w_in, BF16), pltpu.VMEM(w_out, BF16)] * 2,
    )
    return pl.pallas_call(
        _moe_kernel,
        grid_spec=grid_spec,
        out_shape=jax.ShapeDtypeStruct((n_tiles * TMM * TOK_ROWS, LANES), F32),
        compiler_params=pltpu.CompilerParams(dimension_semantics=("arbitrary",),
                                             vmem_limit_bytes=VMEM_LIMIT),
        name="moe_experts",
    )(ea, eb, nrow, newa, newb, total.reshape(1), inv3, inv3, inv3, xtt,
      w_gate, w_up, w_down, w_gate, w_up, w_down)


def _count_le(ends, v):
    return jnp.sum((ends[None, :] <= v[:, None]).astype(I32), axis=1)


def _lookup(table, idx):
    hit = idx[:, None] == jnp.arange(table.shape[0], dtype=I32)[None, :]
    return jnp.sum(jnp.where(hit, table[None, :], 0), axis=1)


def _moe_layer(xtt, ri, cnt, w_gate, w_up, w_down, layer, tp):
    counts = cnt[:N_BUCKETS, 0].astype(I32)
    ntile = (counts + TMM - 1) // TMM
    tend = jnp.cumsum(ntile)
    tstart = tend - ntile
    total = tend[-1]
    pos = _lookup(tstart * TMM, ri[0]) + ri[1]
    pos3 = pos.reshape(tp // TM, 1, TM)

    n_tiles = -(-tp // TMM) + N_BUCKETS
    j = jnp.arange(n_tiles, dtype=I32)
    tb = jnp.minimum(_count_le(tend, jnp.minimum(j, total - 1)), N_BUCKETS - 1)
    grp = tb // N_PAIRS
    pair = tb % N_PAIRS
    ea = grp * EPG + _lookup(jnp.asarray(PAIR_LO, I32), pair)
    eb = grp * EPG + _lookup(jnp.asarray(PAIR_HI, I32), pair)
    nrow = jnp.clip(_lookup(counts, tb) - (j - _lookup(tstart, tb)) * TMM, 0, TMM)
    nrow = jnp.where(j < total, nrow, 0).astype(I32)

    inv3 = _slot_tokens(pos3, n_tiles * TMM).reshape(n_tiles, 1, TMM)
    ys = _moe(xtt, inv3, w_gate, w_up, w_down, layer, ea, eb, nrow, total)
    return ys, pos3


def _expert_rows(pos_refs, ys_ref, buf_ref, sem, split_queues):
    chunks, drain = _gather_token_tile(pos_refs, ys_ref, buf_ref, sem, TM, XCHUNKS,
                                       pl.num_programs(0) - 1, split_queues)
    return jnp.concatenate(chunks, axis=1), drain


def _gather_specs(n_steps):
    assert n_steps > GATHER_AHEAD
    ahead = lambda d: pl.BlockSpec((1, 1, TM), lambda i: (jnp.minimum(i + d, n_steps - 1), 0, 0),
                                   memory_space=pltpu.SMEM)
    return [ahead(d) for d in range(GATHER_BUFS)] + [pl.BlockSpec(memory_space=pl.ANY)]


def _attn_proj_kernel(h_ref, pos0_ref, pos1_ref, pos2_ref, ys_ref, gkv_ref, gq_ref, wk_ref, wvt_ref,
                      wqt_ref, h2_ref, qt_ref, k_ref, vt_ref, buf_ref, sem):
    y, drain = _expert_rows((pos0_ref, pos1_ref, pos2_ref), ys_ref, buf_ref, sem, False)
    h2 = h_ref[...] + y
    h2_ref[...] = h2
    ms = jnp.mean(h2 * h2, axis=-1, keepdims=True)
    xhat = h2 * lax.rsqrt(ms + RMS_EPS)
    xkv = (xhat * gkv_ref[...]).astype(BF16)
    xq = (xhat * gq_ref[...]).astype(BF16)
    nt = (((1,), (1,)), ((), ()))
    k_ref[...] = jnp.dot(xkv, wk_ref[...], preferred_element_type=F32).astype(BF16)
    vt_ref[...] = lax.dot_general(wvt_ref[...], xkv, nt, preferred_element_type=F32).astype(BF16)
    qt = lax.dot_general(wqt_ref[...], xq, nt, preferred_element_type=F32)
    qt_ref[...] = (qt * (LOG2E * HEAD_DIM ** -0.5)).astype(BF16)
    drain()


def _attn_proj(h, pos3, ys, gkv, gq, wk, wvt, wqt):
    tp = h.shape[0]
    steps = tp // TM
    row = pl.BlockSpec((TM, D), lambda i: (i, 0))
    col = pl.BlockSpec((D, TM), lambda i: (0, i))
    vec = pl.BlockSpec((1, D), lambda i: (0, 0))
    mat = pl.BlockSpec((D, D), lambda i: (0, 0))
    return pl.pallas_call(
        _attn_proj_kernel,
        grid=(steps,),
        in_specs=[row] + _gather_specs(steps) + [vec, vec, mat, mat, mat],
        out_specs=(row, col, row, col),
        out_shape=(jax.ShapeDtypeStruct((tp, D), F32), jax.ShapeDtypeStruct((D, tp), BF16),
                   jax.ShapeDtypeStruct((tp, D), BF16), jax.ShapeDtypeStruct((D, tp), BF16)),
        scratch_shapes=_gather_scratch(TM, F32),
        compiler_params=pltpu.CompilerParams(dimension_semantics=("arbitrary",),
                                             vmem_limit_bytes=VMEM_LIMIT),
        name="attn_proj",
    )(h, *([pos3] * GATHER_BUFS), ys, gkv, gq, wk, wvt, wqt)


def _attn_kernel(lambda_init, qt_ref, k_ref, km_ref, vt_ref, vtm_ref, lam_ref, sg_ref, o_ref,
                 s_ref, m_ref, l_ref, acc_ref):
    hw = 2 * HEAD_DIM
    n_x = o_ref.shape[0]
    half = QB // 2
    assert QB == KB and n_x % QB == 0 and N_META <= LANES
    lam = lam_ref[...]
    lam_full = (jnp.exp(jnp.sum(lam[0:1, :] * lam[1:2, :], axis=1, keepdims=True))
                - jnp.exp(jnp.sum(lam[2:3, :] * lam[3:4, :], axis=1, keepdims=True)) + lambda_init)

    diag = ((0, half, 0, QB), (half, half, half, half))
    qw = QB

    def blocks(r, fn):
        for kb in range(r):
            fn(kb * KB, KB, 0, qw, None)
        for ko, kw, qo, qn in diag:
            fn(r * QB + ko, kw, qo, qn, lambda krow, qcol, d=ko - qo: krow + d <= qcol)
        fn(0, LANES, 0, qw, lambda krow, qcol: krow < N_META, meta=True)

    def score_pass(r):
        s_par, m_par = s_ref.at[r % 2], m_ref.at[r % 2]
        qt = qt_ref[:, r * QB:(r + 1) * QB]
        frow = lax.broadcasted_iota(I32, (hw, qw), 0)
        zero = jnp.zeros_like(qt)
        qc = (jnp.where(frow < HEAD_DIM, qt, zero), jnp.where(frow >= HEAD_DIM, qt, zero))
        m_par[...] = jnp.full(m_par.shape, -jnp.inf, F32)

        def scores(k0, kw, qo, qn, visible, meta=False):
            kblk = km_ref[0:kw, :] if meta else k_ref[pl.ds(k0, kw), :]
            k0 = n_x if meta else k0
            for c in range(2):
                s = jnp.dot(kblk, qc[c][:, qo:qo + qn], preferred_element_type=F32)
                if visible is not None:
                    krow = lax.broadcasted_iota(I32, (kw, qn), 0)
                    qcol = lax.broadcasted_iota(I32, (kw, qn), 1)
                    s = jnp.where(visible(krow, qcol), s, -jnp.inf)
                s_par[c, pl.ds(k0, kw), qo:qo + qn] = s
                smax = jnp.max(s.reshape(kw // SUBLANES, SUBLANES, qn), axis=0)
                m_par[c, :, qo:qo + qn] = jnp.maximum(m_par[c, :, qo:qo + qn], smax)

        blocks(r, scores)

    def value_pass(r):
        s_par, m_par = s_ref.at[r % 2], m_ref.at[r % 2]
        m = [jnp.max(m_par[c], axis=0, keepdims=True) for c in range(2)]
        l_ref[...] = jnp.zeros(l_ref.shape, F32)
        acc_ref[...] = jnp.zeros(acc_ref.shape, F32)

        def weighted_values(k0, kw, qo, qn, visible, meta=False):
            del visible
            vblk = vtm_ref[:, 0:kw] if meta else vt_ref[:, pl.ds(k0, kw)]
            k0 = n_x if meta else k0
            for c in range(2):
                p = jnp.exp2(s_par[c, pl.ds(k0, kw), qo:qo + qn] - m[c][:, qo:qo + qn])
                acc_ref[c, :, qo:qo + qn] += jnp.dot(vblk, p.astype(BF16), preferred_element_type=F32)
                l_ref[c, :, qo:qo + qn] += jnp.sum(p.reshape(kw // SUBLANES, SUBLANES, qn), axis=0)

        blocks(r, weighted_values)
        l = [jnp.sum(l_ref[c], axis=0, keepdims=True) for c in range(2)]
        ot = acc_ref[0] / l[0] - lam_full * (acc_ref[1] / l[1])
        ms = jnp.mean(ot * ot, axis=0, keepdims=True)
        y = ot * lax.rsqrt(ms + SUBLN_EPS) * sg_ref[...] * (1.0 - lambda_init)
        o_ref[r * QB:(r + 1) * QB, :] = y.T.astype(BF16)

    n_blocks = n_x // QB
    score_pass(0)
    for r in range(n_blocks):
        if r + 1 < n_blocks:
            score_pass(r + 1)
        value_pass(r)


def _attention(qt, k, vt, lam, sg, lambda_init, bsz, n_x):
    hw = 2 * HEAD_DIM
    meta_blk = bsz * n_x // TM
    fmaj = pl.BlockSpec((hw, n_x), lambda b, h: (h, b))
    tmaj = pl.BlockSpec((n_x, hw), lambda b, h: (b, h))
    return pl.pallas_call(
        functools.partial(_attn_kernel, lambda_init),
        grid=(bsz, N_HEADS),
        in_specs=[fmaj, tmaj, pl.BlockSpec((TM, hw), lambda b, h: (meta_blk, h)),
                  fmaj, pl.BlockSpec((hw, TM), lambda b, h: (h, meta_blk)),
                  pl.BlockSpec((4, HEAD_DIM), lambda b, h: (0, 0)),
                  pl.BlockSpec((hw, 1), lambda b, h: (0, 0))],
        out_specs=tmaj,
        out_shape=jax.ShapeDtypeStruct((bsz * n_x, D), BF16),
        scratch_shapes=[pltpu.VMEM((2, 2, n_x + LANES, QB), F32), pltpu.VMEM((2, 2, SUBLANES, QB), F32),
                        pltpu.VMEM((2, SUBLANES, QB), F32), pltpu.VMEM((2, hw, QB), F32)],
        compiler_params=pltpu.CompilerParams(dimension_semantics=("arbitrary",) * 2,
                                             vmem_limit_bytes=VMEM_LIMIT),
        name="diff_attention",
    )(qt, k, k, vt, vt, lam, sg)


def _attn_out_kernel(h_ref, o_ref, wo_ref, fg_ref, wrt_ref, br_ref, tri_ref,
                     h3_ref, xtt_ref, ri_ref, cnt_ref, run_ref):
    @pl.when(pl.program_id(0) == 0)
    def _():
        run_ref[...] = jnp.zeros_like(run_ref)

    h3 = h_ref[...] + jnp.dot(o_ref[...], wo_ref[...], preferred_element_type=F32)
    h3_ref[...] = h3
    _route_tail(h3, fg_ref, wrt_ref, br_ref, tri_ref, xtt_ref, ri_ref, cnt_ref, run_ref)


def _attn_out(h, o, wo, fg, wrt, br, tri):
    tp = h.shape[0]
    last_o = o.shape[0] // TM - 1
    row = pl.BlockSpec((TM, D), lambda i: (i, 0))
    return pl.pallas_call(
        _attn_out_kernel,
        grid=(tp // TM,),
        in_specs=[row, pl.BlockSpec((TM, D), lambda i: (jnp.minimum(i, last_o), 0)),
                  pl.BlockSpec((D, D), lambda i: (0, 0))] + _route_in_specs(),
        out_specs=(row,) + _route_out_specs(),
        out_shape=(jax.ShapeDtypeStruct((tp, D), F32),) + _route_out_shapes(tp),
        scratch_shapes=[pltpu.VMEM((BUCKET_ROWS, LANES), F32)],
        compiler_params=pltpu.CompilerParams(dimension_semantics=("arbitrary",),
                                             vmem_limit_bytes=VMEM_LIMIT),
        name="attn_out",
    )(h, o, wo, fg, wrt, br, tri)


def _final_kernel(h_ref, pos0_ref, pos1_ref, pos2_ref, ys_ref, g_ref, o_ref, buf_ref, sem):
    y, drain = _expert_rows((pos0_ref, pos1_ref, pos2_ref), ys_ref, buf_ref, sem, True)
    o_ref[...] = _rms(h_ref[...] + y, g_ref[...], RMS_EPS)
    drain()


def _final(h, pos3, ys, g, n_x_tiles):
    row = pl.BlockSpec((TM, D), lambda i: (i, 0))
    return pl.pallas_call(
        _final_kernel,
        grid=(n_x_tiles,),
        in_specs=[row] + _gather_specs(n_x_tiles) + [pl.BlockSpec((1, D), lambda i: (0, 0))],
        out_specs=row,
        out_shape=jax.ShapeDtypeStruct((n_x_tiles * TM, D), F32),
        scratch_shapes=_gather_scratch(TM, F32),
        compiler_params=pltpu.CompilerParams(dimension_semantics=("arbitrary",),
                                             vmem_limit_bytes=VMEM_LIMIT),
        name="final_norm",
    )(h, *([pos3] * GATHER_BUFS), ys, g)


def _router_params(w_rg, b_rg, w_re, b_re):
    wr = jnp.concatenate([w_rg, w_re], axis=1)
    wrt = jnp.zeros((BUCKET_ROWS, D), F32).at[:N_GROUPS + N_EXPERTS].set(wr.T).astype(BF16)
    br = jnp.zeros((BUCKET_ROWS, 1), F32).at[:N_GROUPS + N_EXPERTS, 0].set(jnp.concatenate([b_rg, b_re]))
    return wrt, br


def kernel(x, meta_tokens, a_norm, a_w_in, a_conv, a_w_out, kv_norm, w_kv, b_norm, b_w_q, b_lambda, b_subln, b_w_o, ffn_norm, router_group_w, router_group_b, router_expert_w, router_expert_b, expert_w_gate, expert_w_up, expert_w_down, final_norm):
    bsz, seq, d = x.shape
    assert d == D and a_norm.shape[0] == 1 and b_norm.shape[0] == 1
    assert meta_tokens.shape[0] == N_META and seq % TM == 0
    x_tiles = seq // TM
    n_x_tiles = bsz * x_tiles
    tp = (n_x_tiles + 1) * TM

    metapad = jnp.concatenate([meta_tokens.astype(x.dtype), jnp.zeros((TM - N_META, D), x.dtype)])
    tri = jnp.triu(jnp.ones((TM, TM), F32)).astype(BF16)
    bf = lambda w: w.astype(BF16)

    wrt, br = _router_params(router_group_w[0], router_group_b[0], router_expert_w[0], router_expert_b[0])
    h, xtt, ri, cnt = _mixer_a(x.reshape(bsz * seq, D), metapad, a_norm[0][None], bf(a_w_in[0]), a_conv[0],
                               bf(a_w_out[0]), ffn_norm[0][None], wrt, br, tri, x_tiles)
    ys, pos3 = _moe_layer(xtt, ri, cnt, expert_w_gate, expert_w_up, expert_w_down, 0, tp)

    h, qt, k, vt = _attn_proj(h, pos3, ys, kv_norm[None], b_norm[0][None],
                              bf(w_kv[:, :D]), bf(w_kv[:, D:].T), bf(b_w_q[0].T))
    o = _attention(qt, k, vt, b_lambda[0], b_subln[0][:, None], _lambda_init(1), bsz, seq)
    wrt, br = _router_params(router_group_w[1], router_group_b[1], router_expert_w[1], router_expert_b[1])
    h, xtt, ri, cnt = _attn_out(h, o, bf(b_w_o[0]), ffn_norm[1][None], wrt, br, tri)
    ys, pos3 = _moe_layer(xtt, ri, cnt, expert_w_gate, expert_w_up, expert_w_down, 1, tp)

    out = _final(h, pos3, ys, final_norm[None], n_x_tiles)
    return out.reshape(bsz, seq, D)
```

```python
import functools
import math

import jax
import jax.numpy as jnp
from jax import lax
from jax.experimental import pallas as pl
from jax.experimental.pallas import tpu as pltpu

F32 = jnp.float32
BF16 = jnp.bfloat16
I32 = jnp.int32
U32 = jnp.uint32

D = 1024
N_META = 16
Q_BLOCK = 128
HEAD_DIM = 64
N_HEADS = D // (2 * HEAD_DIM)
N_GROUPS = 4
EPG = 4
N_EXPERTS = N_GROUPS * EPG
D_EXPERT = D // 2
RMS_EPS = 1e-6
SUBLN_EPS = 1e-5
LOG2E = math.log2(math.e)

LANES = 128
SUBLANES = 8
TOK_ROWS = SUBLANES
XCHUNKS = D // LANES
XT_ROWS = 2 * SUBLANES

TM = 512
QB = 512
KB = 512
TMM = 512
TPERM = 512
PERM_UNROLL = 8
N_PAIRS = 6
N_BUCKETS = N_GROUPS * N_PAIRS
BUCKET_ROWS = 32
PAIR_LO = (0, 0, 0, 1, 1, 2)
PAIR_HI = (1, 2, 3, 2, 3, 3)
VMEM_LIMIT = 56 * 1024 * 1024


def _lambda_init(layer_idx):
    return 0.8 - 0.6 * math.exp(-0.3 * layer_idx)


def _rms(x, g, eps):
    ms = jnp.mean(x * x, axis=-1, keepdims=True)
    return x * lax.rsqrt(ms + eps) * g


def _load_token_tiled(ref, n_tok, n_chunks, rows=TOK_ROWS):
    return [ref[pl.ds(r, n_tok, stride=rows), :] for r in range(n_chunks)]


def _store_token_tiled(ref, chunks, n_tok, rows=TOK_ROWS):
    for r, c in enumerate(chunks):
        ref[pl.ds(r, n_tok, stride=rows), :] = c


def _route_tail(h, fg_ref, wrt_ref, br_ref, tri_ref, xtt_ref, ri_ref, cnt_ref, run_ref):
    tm = h.shape[0]
    xn = _rms(h, fg_ref[...], RMS_EPS)
    xb = xn.astype(BF16)

    lt = lax.dot_general(wrt_ref[...], xb, (((1,), (1,)), ((), ())), preferred_element_type=F32)
    lt = lt + br_ref[...]
    lg = [lt[k:k + 1, :] for k in range(N_GROUPS)]
    m = jnp.maximum(jnp.maximum(lg[0], lg[1]), jnp.maximum(lg[2], lg[3]))
    gidx = jnp.where(lg[0] == m, 0, jnp.where(lg[1] == m, 1, jnp.where(lg[2] == m, 2, 3))).astype(I32)
    se = jnp.exp(lg[0] - m) + jnp.exp(lg[1] - m) + jnp.exp(lg[2] - m) + jnp.exp(lg[3] - m)
    p_sel = 1.0 / se

    def le_row(g, j):
        r = N_GROUPS + g * EPG + j
        return lt[r:r + 1, :]

    sel = [jnp.where(gidx == 0, le_row(0, j),
                     jnp.where(gidx == 1, le_row(1, j),
                               jnp.where(gidx == 2, le_row(2, j), le_row(3, j)))) for j in range(EPG)]

    def first_argmax(vals):
        v = jnp.maximum(jnp.maximum(vals[0], vals[1]), jnp.maximum(vals[2], vals[3]))
        i = jnp.where(vals[0] == v, 0, jnp.where(vals[1] == v, 1, jnp.where(vals[2] == v, 2, 3))).astype(I32)
        return v, i

    v1, i1 = first_argmax(sel)
    sel2 = [jnp.where(i1 == j, -jnp.inf, sel[j]) for j in range(EPG)]
    v2, i2 = first_argmax(sel2)
    e2 = jnp.exp(v2 - v1)
    den = 1.0 + e2
    w1 = (1.0 / den) * p_sel
    w2 = (e2 / den) * p_sel
    lo = jnp.minimum(i1, i2)
    hi = jnp.maximum(i1, i2)
    first_is_lo = i1 < i2
    w_lo = jnp.where(first_is_lo, w1, w2)
    w_hi = jnp.where(first_is_lo, w2, w1)
    pair = jnp.where(lo == 0, hi - 1, jnp.where(lo == 1, hi + 1, 5))
    bucket = gidx * N_PAIRS + pair

    rows = lax.broadcasted_iota(I32, (BUCKET_ROWS, tm), 0)
    ohf = (rows == bucket).astype(F32)
    cum = jnp.dot(ohf.astype(BF16), tri_ref[...], preferred_element_type=F32)
    run = run_ref[:, 0:1]
    rank = jnp.sum(ohf * (cum - 1.0 + run), axis=0, keepdims=True)
    run_ref[...] = run_ref[...] + jnp.sum(ohf, axis=1, keepdims=True)
    cnt_ref[...] = run_ref[...]

    ri_ref[0:1, :] = bucket
    ri_ref[1:2, :] = rank.astype(I32)
    ri_ref[2:SUBLANES, :] = jnp.zeros((SUBLANES - 2, tm), I32)

    wt = jnp.concatenate([w_lo, w_hi, jnp.zeros((LANES - 2, tm), F32)], axis=0).T

    xtt_ref[...] = jnp.zeros(xtt_ref.shape, F32)
    chunks = [xn[:, r * LANES:(r + 1) * LANES] for r in range(XCHUNKS)] + [wt]
    _store_token_tiled(xtt_ref, chunks, tm, XT_ROWS)


def _route_out_shapes(tp):
    return (jax.ShapeDtypeStruct((tp * XT_ROWS, LANES), F32),
            jax.ShapeDtypeStruct((SUBLANES, tp), I32),
            jax.ShapeDtypeStruct((BUCKET_ROWS, LANES), F32))


def _route_out_specs():
    return (pl.BlockSpec((TM * XT_ROWS, LANES), lambda i: (i, 0)),
            pl.BlockSpec((SUBLANES, TM), lambda i: (0, i)),
            pl.BlockSpec((BUCKET_ROWS, LANES), lambda i: (0, 0)))


def _route_in_specs():
    return [pl.BlockSpec((1, D), lambda i: (0, 0)),
            pl.BlockSpec((BUCKET_ROWS, D), lambda i: (0, 0)),
            pl.BlockSpec((BUCKET_ROWS, 1), lambda i: (0, 0)),
            pl.BlockSpec((TM, TM), lambda i: (0, 0))]


def _mixer_a_kernel(x_tiles, n_x_tiles, x_ref, meta_ref, g_ref, win_ref, conv_ref, wout_ref,
                    fg_ref, wrt_ref, br_ref, tri_ref,
                    h1_ref, xtt_ref, ri_ref, cnt_ref, zs_ref, zmeta_ref, run_ref):
    i = pl.program_id(0)

    def conv_inputs(h):
        xn = _rms(h, g_ref[...], RMS_EPS).astype(BF16)
        bcu = jnp.dot(xn, win_ref[...], preferred_element_type=F32)
        return bcu, bcu[:, D:2 * D] * bcu[:, 2 * D:3 * D]

    @pl.when(i == 0)
    def _():
        run_ref[...] = jnp.zeros_like(run_ref)
        _, zm = conv_inputs(meta_ref[...])
        zmeta_ref[...] = zm[N_META - SUBLANES:N_META, :]

    @pl.when(i % x_tiles == 0)
    def _():
        zs_ref[0:SUBLANES, :] = zmeta_ref[...]

    @pl.when(i == n_x_tiles)
    def _():
        zs_ref[0:SUBLANES, :] = jnp.zeros((SUBLANES, D), F32)

    h = jnp.where(i < n_x_tiles, x_ref[...], meta_ref[...])
    bcu, z = conv_inputs(h)
    zs_ref[SUBLANES:SUBLANES + TM, :] = z
    cw = conv_ref[...]
    conv = (cw[0:1, :] * zs_ref[SUBLANES - 2:SUBLANES - 2 + TM, :]
            + cw[1:2, :] * zs_ref[SUBLANES - 1:SUBLANES - 1 + TM, :]
            + cw[2:3, :] * z)
    zs_ref[0:SUBLANES, :] = zs_ref[TM:TM + SUBLANES, :]
    mix = jnp.dot((bcu[:, 0:D] * conv).astype(BF16), wout_ref[...], preferred_element_type=F32)
    h1 = h + mix
    h1_ref[...] = h1
    _route_tail(h1, fg_ref, wrt_ref, br_ref, tri_ref, xtt_ref, ri_ref, cnt_ref, run_ref)


def _mixer_a(x2, metapad, g, w_in, conv_w, w_out, fg, wrt, br, tri, x_tiles):
    n_x_tiles = x2.shape[0] // TM
    tp = (n_x_tiles + 1) * TM
    return pl.pallas_call(
        functools.partial(_mixer_a_kernel, x_tiles, n_x_tiles),
        grid=(tp // TM,),
        in_specs=[pl.BlockSpec((TM, D), lambda i: (jnp.minimum(i, n_x_tiles - 1), 0)),
                  pl.BlockSpec((TM, D), lambda i: (0, 0)),
                  pl.BlockSpec((1, D), lambda i: (0, 0)),
                  pl.BlockSpec((D, 3 * D), lambda i: (0, 0)),
                  pl.BlockSpec((3, D), lambda i: (0, 0)),
                  pl.BlockSpec((D, D), lambda i: (0, 0))] + _route_in_specs(),
        out_specs=(pl.BlockSpec((TM, D), lambda i: (i, 0)),) + _route_out_specs(),
        out_shape=(jax.ShapeDtypeStruct((tp, D), F32),) + _route_out_shapes(tp),
        scratch_shapes=[pltpu.VMEM((TM + SUBLANES, D), F32),
                        pltpu.VMEM((SUBLANES, D), F32),
                        pltpu.VMEM((BUCKET_ROWS, LANES), F32)],
        compiler_params=pltpu.CompilerParams(dimension_semantics=("arbitrary",),
                                             vmem_limit_bytes=VMEM_LIMIT),
        name="mixer_a",
    )(x2, metapad, g, w_in, conv_w, w_out, fg, wrt, br, tri)


def _run_token_copies(copy):
    def start(g, c):
        for u in range(PERM_UNROLL):
            copy(g * PERM_UNROLL + u).start(priority=u % 2)
        return c

    def wait(g, c):
        for u in range(PERM_UNROLL):
            copy(g * PERM_UNROLL + u).wait()
        return c

    lax.fori_loop(0, TPERM // PERM_UNROLL, start, 0)
    lax.fori_loop(0, TPERM // PERM_UNROLL, wait, 0)


def _tok_rows(ref, t, rows=TOK_ROWS):
    return ref.at[pl.ds(pl.multiple_of(t * rows, rows), rows), :]


def _scatter_kernel(n_dst_tiles, ztile_ref, total_ref, b_ref, src_ref, dst_ref, zero_ref, zsem, sem):
    @pl.when(pl.program_id(0) == 0)
    def _():
        zero_ref[...] = jnp.zeros_like(zero_ref)

        def zero_tile(t):
            rows = TMM * XT_ROWS
            return pltpu.make_async_copy(
                zero_ref, dst_ref.at[pl.ds(pl.multiple_of(t * rows, rows), rows), :], zsem)

        def each_zero_tile(fn):
            for b in range(N_BUCKETS):
                @pl.when(ztile_ref[b] >= 0)
                def _():
                    fn(zero_tile(ztile_ref[b]))

            def tail(t, c):
                fn(zero_tile(t))
                return c

            lax.fori_loop(total_ref[0], n_dst_tiles, tail, 0)

        each_zero_tile(lambda c: c.start())
        each_zero_tile(lambda c: c.wait())

    _run_token_copies(lambda k: pltpu.make_async_copy(
        _tok_rows(src_ref, k, XT_ROWS), _tok_rows(dst_ref, b_ref[0, 0, k], XT_ROWS), sem))


def _scatter_tokens(src, b_idx, ztile, total, n_dst_tiles):
    steps = b_idx.shape[0] // TPERM
    grid_spec = pltpu.PrefetchScalarGridSpec(
        num_scalar_prefetch=2,
        grid=(steps,),
        in_specs=[pl.BlockSpec((1, 1, TPERM), lambda i, *_: (i, 0, 0), memory_space=pltpu.SMEM),
                  pl.BlockSpec((TPERM * XT_ROWS, LANES), lambda i, *_: (i, 0))],
        out_specs=pl.BlockSpec(memory_space=pl.ANY),
        scratch_shapes=[pltpu.VMEM((TMM * XT_ROWS, LANES), src.dtype),
                        pltpu.SemaphoreType.DMA, pltpu.SemaphoreType.DMA],
    )
    return pl.pallas_call(
        functools.partial(_scatter_kernel, n_dst_tiles),
        grid_spec=grid_spec,
        out_shape=jax.ShapeDtypeStruct((n_dst_tiles * TMM * XT_ROWS, LANES), src.dtype),
        compiler_params=pltpu.CompilerParams(dimension_semantics=("arbitrary",),
                                             vmem_limit_bytes=VMEM_LIMIT),
        name="scatter_tokens",
    )(ztile, total.reshape(1), b_idx.reshape(steps, 1, TPERM), src)


def _moe_kernel(blk_ref, ea_ref, eb_ref, nrow_ref, newa_ref, newb_ref,
                xs_ref, wga_f32, wua_f32, wda_f32, wgb_f32, wub_f32, wdb_f32, ys_ref,
                wga_ref, wua_ref, wda_ref, wgb_ref, wub_ref, wdb_ref):
    j = pl.program_id(0)
    nrow = nrow_ref[j]

    @pl.when(newa_ref[j] == 1)
    def _():
        for dst, src in ((wga_ref, wga_f32), (wua_ref, wua_f32), (wda_ref, wda_f32)):
            dst[...] = src[...].astype(BF16)

    @pl.when(newb_ref[j] == 1)
    def _():
        for dst, src in ((wgb_ref, wgb_f32), (wub_ref, wub_f32), (wdb_ref, wdb_f32)):
            dst[...] = src[...].astype(BF16)

    @pl.when(nrow > 0)
    def _():
        rows = _load_token_tiled(xs_ref, TMM, XCHUNKS + 1, XT_ROWS)
        x = jnp.concatenate(rows[:XCHUNKS], axis=1).astype(BF16)
        wrow = rows[XCHUNKS]

        def expert(wg_ref, wu_ref, wd_ref, c):
            g = jnp.dot(x, wg_ref[...], preferred_element_type=F32)
            u = jnp.dot(x, wu_ref[...], preferred_element_type=F32)
            hmid = (jax.nn.silu(g) * u) * c
            return jnp.dot(hmid.astype(BF16), wd_ref[...], preferred_element_type=F32)

        y = expert(wga_ref, wua_ref, wda_ref, wrow[:, 0:1]) + expert(wgb_ref, wub_ref, wdb_ref, wrow[:, 1:2])
        _store_token_tiled(ys_ref, [y[:, r * LANES:(r + 1) * LANES] for r in range(XCHUNKS)], TMM)

    @pl.when(nrow == 0)
    def _():
        ys_ref[...] = jnp.zeros_like(ys_ref)


def _moe(xs, w_gate, w_up, w_down, layer, blk, ea, eb, nrow):
    n_tiles = blk.shape[0]
    p_tok = xs.shape[0] // XT_ROWS
    first = jnp.ones((1,), I32)
    newa = jnp.concatenate([first, (ea[1:] != ea[:-1]).astype(I32)])
    newb = jnp.concatenate([first, (eb[1:] != eb[:-1]).astype(I32)])
    tok_spec = pl.BlockSpec((TMM * XT_ROWS, LANES), lambda j, blk, *_: (blk[j], 0))

    def wspec(shape, which):
        if which == 0:
            return pl.BlockSpec((None, None) + shape, lambda j, blk, ea, eb, *_: (layer, ea[j], 0, 0))
        return pl.BlockSpec((None, None) + shape, lambda j, blk, ea, eb, *_: (layer, eb[j], 0, 0))

    w_in, w_out = (D, D_EXPERT), (D_EXPERT, D)
    grid_spec = pltpu.PrefetchScalarGridSpec(
        num_scalar_prefetch=6,
        grid=(n_tiles,),
        in_specs=[tok_spec, wspec(w_in, 0), wspec(w_in, 0), wspec(w_out, 0),
                  wspec(w_in, 1), wspec(w_in, 1), wspec(w_out, 1)],
        out_specs=pl.BlockSpec((TMM * TOK_ROWS, LANES), lambda j, *_: (j, 0)),
        scratch_shapes=[pltpu.VMEM(w_in, BF16), pltpu.VMEM(w_in, BF16), pltpu.VMEM(w_out, BF16)] * 2,
    )
    return pl.pallas_call(
        _moe_kernel,
        grid_spec=grid_spec,
        out_shape=jax.ShapeDtypeStruct((p_tok * TOK_ROWS, LANES), F32),
        compiler_params=pltpu.CompilerParams(dimension_semantics=("arbitrary",),
                                             vmem_limit_bytes=VMEM_LIMIT),
        name="moe_experts",
    )(blk, ea, eb, nrow, newa, newb, xs, w_gate, w_up, w_down, w_gate, w_up, w_down)


def _count_le(ends, v):
    return jnp.sum((ends[None, :] <= v[:, None]).astype(I32), axis=1)


def _lookup(table, idx):
    hit = idx[:, None] == jnp.arange(table.shape[0], dtype=I32)[None, :]
    return jnp.sum(jnp.where(hit, table[None, :], 0), axis=1)


def _moe_layer(xtt, ri, cnt, w_gate, w_up, w_down, layer, tp):
    counts = cnt[:N_BUCKETS, 0].astype(I32)
    ntile = (counts + TMM - 1) // TMM
    tend = jnp.cumsum(ntile)
    tstart = tend - ntile
    total = tend[-1]
    pos = _lookup(tstart * TMM, ri[0]) + ri[1]

    n_tiles = -(-tp // TMM) + N_BUCKETS
    j = jnp.arange(n_tiles, dtype=I32)
    blk = jnp.minimum(j, total - 1)
    tb = jnp.minimum(_count_le(tend, blk), N_BUCKETS - 1)
    grp = tb // N_PAIRS
    pair = tb % N_PAIRS
    ea = grp * EPG + _lookup(jnp.asarray(PAIR_LO, I32), pair)
    eb = grp * EPG + _lookup(jnp.asarray(PAIR_HI, I32), pair)
    nrow = jnp.clip(_lookup(counts, tb) - (j - _lookup(tstart, tb)) * TMM, 0, TMM)
    nrow = jnp.where(j < total, nrow, 0).astype(I32)
    ztile = jnp.where(ntile > 0, tend - 1, -1)

    xs = _scatter_tokens(xtt, pos, ztile, total, n_tiles)
    ys = _moe(xs, w_gate, w_up, w_down, layer, blk, ea, eb, nrow)
    return ys, pos.reshape(tp // TM, 1, TM)


GATHER_AHEAD = 2
GATHER_BUFS = GATHER_AHEAD + 1


def _gather_expert_rows(pos_refs, ys_ref, buf_ref, sem, split_queues):
    i = pl.program_id(0)
    n = pl.num_programs(0)

    def copies(idx_ref, s):
        return [pltpu.make_async_copy(_tok_rows(ys_ref, idx_ref[0, 0, k]),
                                      buf_ref.at[s, pl.ds(k * TOK_ROWS, TOK_ROWS), :], sem.at[s])
                for k in range(TM)]

    def start_all(cs):
        for k, c in enumerate(cs):
            c.start(priority=k % 2 if split_queues else 0)

    @pl.when(i == 0)
    def _():
        for d in range(GATHER_AHEAD):
            start_all(copies(pos_refs[d], d))

    slot = i % GATHER_BUFS
    for c in copies(pos_refs[0], slot):
        c.wait()
    y = jnp.concatenate([buf_ref[slot, pl.ds(r, TM, stride=TOK_ROWS), :] for r in range(XCHUNKS)], axis=1)
    start_all(copies(pos_refs[GATHER_AHEAD], (i + GATHER_AHEAD) % GATHER_BUFS))

    def drain():
        @pl.when(i == n - 1)
        def _():
            for d in range(1, GATHER_BUFS):
                for c in copies(pos_refs[0], (i + d) % GATHER_BUFS):
                    c.wait()

    return y, drain


def _gather_specs(n_steps):
    assert n_steps > GATHER_AHEAD
    ahead = lambda d: pl.BlockSpec((1, 1, TM), lambda i: (jnp.minimum(i + d, n_steps - 1), 0, 0),
                                   memory_space=pltpu.SMEM)
    return [ahead(d) for d in range(GATHER_BUFS)] + [pl.BlockSpec(memory_space=pl.ANY)]


GATHER_SCRATCH = [pltpu.VMEM((GATHER_BUFS, TM * TOK_ROWS, LANES), F32),
                  pltpu.SemaphoreType.DMA((GATHER_BUFS,))]


def _attn_proj_kernel(h_ref, pos0_ref, pos1_ref, pos2_ref, ys_ref, gkv_ref, gq_ref, wk_ref, wvt_ref,
                      wqt_ref, h2_ref, qt_ref, k_ref, vt_ref, buf_ref, sem):
    y, drain = _gather_expert_rows((pos0_ref, pos1_ref, pos2_ref), ys_ref, buf_ref, sem, False)
    h2 = h_ref[...] + y
    h2_ref[...] = h2
    ms = jnp.mean(h2 * h2, axis=-1, keepdims=True)
    xhat = h2 * lax.rsqrt(ms + RMS_EPS)
    xkv = (xhat * gkv_ref[...]).astype(BF16)
    xq = (xhat * gq_ref[...]).astype(BF16)
    nt = (((1,), (1,)), ((), ()))
    k_ref[...] = jnp.dot(xkv, wk_ref[...], preferred_element_type=F32).astype(BF16)
    vt_ref[...] = lax.dot_general(wvt_ref[...], xkv, nt, preferred_element_type=F32).astype(BF16)
    qt = lax.dot_general(wqt_ref[...], xq, nt, preferred_element_type=F32)
    qt_ref[...] = (qt * (LOG2E * HEAD_DIM ** -0.5)).astype(BF16)
    drain()


def _attn_proj(h, pos3, ys, gkv, gq, wk, wvt, wqt):
    tp = h.shape[0]
    steps = tp // TM
    row = pl.BlockSpec((TM, D), lambda i: (i, 0))
    col = pl.BlockSpec((D, TM), lambda i: (0, i))
    vec = pl.BlockSpec((1, D), lambda i: (0, 0))
    mat = pl.BlockSpec((D, D), lambda i: (0, 0))
    return pl.pallas_call(
        _attn_proj_kernel,
        grid=(steps,),
        in_specs=[row] + _gather_specs(steps) + [vec, vec, mat, mat, mat],
        out_specs=(row, col, row, col),
        out_shape=(jax.ShapeDtypeStruct((tp, D), F32), jax.ShapeDtypeStruct((D, tp), BF16),
                   jax.ShapeDtypeStruct((tp, D), BF16), jax.ShapeDtypeStruct((D, tp), BF16)),
        scratch_shapes=GATHER_SCRATCH,
        compiler_params=pltpu.CompilerParams(dimension_semantics=("arbitrary",),
                                             vmem_limit_bytes=VMEM_LIMIT),
        name="attn_proj",
    )(h, *([pos3] * GATHER_BUFS), ys, gkv, gq, wk, wvt, wqt)


def _attn_kernel(lambda_init, qt_ref, k_ref, km_ref, vt_ref, vtm_ref, lam_ref, sg_ref, o_ref,
                 s_ref, m_ref, l_ref, acc_ref):
    hw = 2 * HEAD_DIM
    n_x = o_ref.shape[0]
    half = QB // 2
    assert QB == KB and n_x % QB == 0 and N_META <= LANES
    lam = lam_ref[...]
    lam_full = (jnp.exp(jnp.sum(lam[0:1, :] * lam[1:2, :], axis=1, keepdims=True))
                - jnp.exp(jnp.sum(lam[2:3, :] * lam[3:4, :], axis=1, keepdims=True)) + lambda_init)

    diag = ((0, half, 0, QB), (half, half, half, half))
    qw = QB

    def blocks(r, fn):
        for kb in range(r):
            fn(kb * KB, KB, 0, qw, None)
        for ko, kw, qo, qn in diag:
            fn(r * QB + ko, kw, qo, qn, lambda krow, qcol, d=ko - qo: krow + d <= qcol)
        fn(0, LANES, 0, qw, lambda krow, qcol: krow < N_META, meta=True)

    def score_pass(r):
        s_par, m_par = s_ref.at[r % 2], m_ref.at[r % 2]
        qt = qt_ref[:, r * QB:(r + 1) * QB]
        frow = lax.broadcasted_iota(I32, (hw, qw), 0)
        zero = jnp.zeros_like(qt)
        qc = (jnp.where(frow < HEAD_DIM, qt, zero), jnp.where(frow >= HEAD_DIM, qt, zero))
        m_par[...] = jnp.full(m_par.shape, -jnp.inf, F32)

        def scores(k0, kw, qo, qn, visible, meta=False):
            kblk = km_ref[0:kw, :] if meta else k_ref[pl.ds(k0, kw), :]
            k0 = n_x if meta else k0
            for c in range(2):
                s = jnp.dot(kblk, qc[c][:, qo:qo + qn], preferred_element_type=F32)
                if visible is not None:
                    krow = lax.broadcasted_iota(I32, (kw, qn), 0)
                    qcol = lax.broadcasted_iota(I32, (kw, qn), 1)
                    s = jnp.where(visible(krow, qcol), s, -jnp.inf)
                s_par[c, pl.ds(k0, kw), qo:qo + qn] = s
                smax = jnp.max(s.reshape(kw // SUBLANES, SUBLANES, qn), axis=0)
                m_par[c, :, qo:qo + qn] = jnp.maximum(m_par[c, :, qo:qo + qn], smax)

        blocks(r, scores)

    def value_pass(r):
        s_par, m_par = s_ref.at[r % 2], m_ref.at[r % 2]
        m = [jnp.max(m_par[c], axis=0, keepdims=True) for c in range(2)]
        l_ref[...] = jnp.zeros(l_ref.shape, F32)
        acc_ref[...] = jnp.zeros(acc_ref.shape, F32)

        def weighted_values(k0, kw, qo, qn, visible, meta=False):
            del visible
            vblk = vtm_ref[:, 0:kw] if meta else vt_ref[:, pl.ds(k0, kw)]
            k0 = n_x if meta else k0
            for c in range(2):
                p = jnp.exp2(s_par[c, pl.ds(k0, kw), qo:qo + qn] - m[c][:, qo:qo + qn])
                acc_ref[c, :, qo:qo + qn] += jnp.dot(vblk, p.astype(BF16), preferred_element_type=F32)
                l_ref[c, :, qo:qo + qn] += jnp.sum(p.reshape(kw // SUBLANES, SUBLANES, qn), axis=0)

        blocks(r, weighted_values)
        l = [jnp.sum(l_ref[c], axis=0, keepdims=True) for c in range(2)]
        ot = acc_ref[0] / l[0] - lam_full * (acc_ref[1] / l[1])
        ms = jnp.mean(ot * ot, axis=0, keepdims=True)
        y = ot * lax.rsqrt(ms + SUBLN_EPS) * sg_ref[...] * (1.0 - lambda_init)
        o_ref[r * QB:(r + 1) * QB, :] = y.T.astype(BF16)

    n_blocks = n_x // QB
    score_pass(0)
    for r in range(n_blocks):
        if r + 1 < n_blocks:
            score_pass(r + 1)
        value_pass(r)


def _attention(qt, k, vt, lam, sg, lambda_init, bsz, n_x):
    hw = 2 * HEAD_DIM
    meta_blk = bsz * n_x // TM
    fmaj = pl.BlockSpec((hw, n_x), lambda b, h: (h, b))
    tmaj = pl.BlockSpec((n_x, hw), lambda b, h: (b, h))
    return pl.pallas_call(
        functools.partial(_attn_kernel, lambda_init),
        grid=(bsz, N_HEADS),
        in_specs=[fmaj, tmaj, pl.BlockSpec((TM, hw), lambda b, h: (meta_blk, h)),
                  fmaj, pl.BlockSpec((hw, TM), lambda b, h: (h, meta_blk)),
                  pl.BlockSpec((4, HEAD_DIM), lambda b, h: (0, 0)),
                  pl.BlockSpec((hw, 1), lambda b, h: (0, 0))],
        out_specs=tmaj,
        out_shape=jax.ShapeDtypeStruct((bsz * n_x, D), BF16),
        scratch_shapes=[pltpu.VMEM((2, 2, n_x + LANES, QB), F32), pltpu.VMEM((2, 2, SUBLANES, QB), F32),
                        pltpu.VMEM((2, SUBLANES, QB), F32), pltpu.VMEM((2, hw, QB), F32)],
        compiler_params=pltpu.CompilerParams(dimension_semantics=("arbitrary",) * 2,
                                             vmem_limit_bytes=VMEM_LIMIT),
        name="diff_attention",
    )(qt, k, k, vt, vt, lam, sg)


def _attn_out_kernel(h_ref, o_ref, wo_ref, fg_ref, wrt_ref, br_ref, tri_ref,
                     h3_ref, xtt_ref, ri_ref, cnt_ref, run_ref):
    @pl.when(pl.program_id(0) == 0)
    def _():
        run_ref[...] = jnp.zeros_like(run_ref)

    h3 = h_ref[...] + jnp.dot(o_ref[...], wo_ref[...], preferred_element_type=F32)
    h3_ref[...] = h3
    _route_tail(h3, fg_ref, wrt_ref, br_ref, tri_ref, xtt_ref, ri_ref, cnt_ref, run_ref)


def _attn_out(h, o, wo, fg, wrt, br, tri):
    tp = h.shape[0]
    last_o = o.shape[0] // TM - 1
    row = pl.BlockSpec((TM, D), lambda i: (i, 0))
    return pl.pallas_call(
        _attn_out_kernel,
        grid=(tp // TM,),
        in_specs=[row, pl.BlockSpec((TM, D), lambda i: (jnp.minimum(i, last_o), 0)),
                  pl.BlockSpec((D, D), lambda i: (0, 0))] + _route_in_specs(),
        out_specs=(row,) + _route_out_specs(),
        out_shape=(jax.ShapeDtypeStruct((tp, D), F32),) + _route_out_shapes(tp),
        scratch_shapes=[pltpu.VMEM((BUCKET_ROWS, LANES), F32)],
        compiler_params=pltpu.CompilerParams(dimension_semantics=("arbitrary",),
                                             vmem_limit_bytes=VMEM_LIMIT),
        name="attn_out",
    )(h, o, wo, fg, wrt, br, tri)


def _final_kernel(h_ref, pos0_ref, pos1_ref, pos2_ref, ys_ref, g_ref, o_ref, buf_ref, sem):
    y, drain = _gather_expert_rows((pos0_ref, pos1_ref, pos2_ref), ys_ref, buf_ref, sem, True)
    o_ref[...] = _rms(h_ref[...] + y, g_ref[...], RMS_EPS)
    drain()


def _final(h, pos3, ys, g, n_x_tiles):
    row = pl.BlockSpec((TM, D), lambda i: (i, 0))
    return pl.pallas_call(
        _final_kernel,
        grid=(n_x_tiles,),
        in_specs=[row] + _gather_specs(n_x_tiles) + [pl.BlockSpec((1, D), lambda i: (0, 0))],
        out_specs=row,
        out_shape=jax.ShapeDtypeStruct((n_x_tiles * TM, D), F32),
        scratch_shapes=GATHER_SCRATCH,
        compiler_params=pltpu.CompilerParams(dimension_semantics=("arbitrary",),
                                             vmem_limit_bytes=VMEM_LIMIT),
        name="final_norm",
    )(h, *([pos3] * GATHER_BUFS), ys, g)


def _router_params(w_rg, b_rg, w_re, b_re):
    wr = jnp.concatenate([w_rg, w_re], axis=1)
    wrt = jnp.zeros((BUCKET_ROWS, D), F32).at[:N_GROUPS + N_EXPERTS].set(wr.T).astype(BF16)
    br = jnp.zeros((BUCKET_ROWS, 1), F32).at[:N_GROUPS + N_EXPERTS, 0].set(jnp.concatenate([b_rg, b_re]))
    return wrt, br


def kernel(x, meta_tokens, a_norm, a_w_in, a_conv, a_w_out, kv_norm, w_kv, b_norm, b_w_q, b_lambda, b_subln, b_w_o, ffn_norm, router_group_w, router_group_b, router_expert_w, router_expert_b, expert_w_gate, expert_w_up, expert_w_down, final_norm):
    bsz, seq, d = x.shape
    assert d == D and a_norm.shape[0] == 1 and b_norm.shape[0] == 1
    assert meta_tokens.shape[0] == N_META and seq % TM == 0
    x_tiles = seq // TM
    n_x_tiles = bsz * x_tiles
    tp = (n_x_tiles + 1) * TM
    assert tp % TPERM == 0

    metapad = jnp.concatenate([meta_tokens.astype(x.dtype), jnp.zeros((TM - N_META, D), x.dtype)])
    tri = jnp.triu(jnp.ones((TM, TM), F32)).astype(BF16)
    bf = lambda w: w.astype(BF16)

    wrt, br = _router_params(router_group_w[0], router_group_b[0], router_expert_w[0], router_expert_b[0])
    h, xtt, ri, cnt = _mixer_a(x.reshape(bsz * seq, D), metapad, a_norm[0][None], bf(a_w_in[0]), a_conv[0],
                               bf(a_w_out[0]), ffn_norm[0][None], wrt, br, tri, x_tiles)
    ys, pos3 = _moe_layer(xtt, ri, cnt, expert_w_gate, expert_w_up, expert_w_down, 0, tp)

    h, qt, k, vt = _attn_proj(h, pos3, ys, kv_norm[None], b_norm[0][None],
                              bf(w_kv[:, :D]), bf(w_kv[:, D:].T), bf(b_w_q[0].T))
    o = _attention(qt, k, vt, b_lambda[0], b_subln[0][:, None], _lambda_init(1), bsz, seq)
    wrt, br = _router_params(router_group_w[1], router_group_b[1], router_expert_w[1], router_expert_b[1])
    h, xtt, ri, cnt = _attn_out(h, o, bf(b_w_o[0]), ffn_norm[1][None], wrt, br, tri)
    ys, pos3 = _moe_layer(xtt, ri, cnt, expert_w_gate, expert_w_up, expert_w_down, 1, tp)

    out = _final(h, pos3, ys, final_norm[None], n_x_tiles)
    return out.reshape(bsz, seq, D)
```

```python
import functools
import math

import jax
import jax.numpy as jnp
from jax import lax
from jax.experimental import pallas as pl
from jax.experimental.pallas import tpu as pltpu

F32 = jnp.float32
BF16 = jnp.bfloat16
I32 = jnp.int32
U32 = jnp.uint32

D = 1024
N_META = 16
Q_BLOCK = 128
HEAD_DIM = 64
N_HEADS = D // (2 * HEAD_DIM)
N_GROUPS = 4
EPG = 4
N_EXPERTS = N_GROUPS * EPG
D_EXPERT = D // 2
RMS_EPS = 1e-6
SUBLN_EPS = 1e-5
LOG2E = math.log2(math.e)

LANES = 128
SUBLANES = 8
TOK_ROWS = SUBLANES
XCHUNKS = D // LANES
XT_ROWS = 2 * SUBLANES

TM = 512
QB = 512
KB = 512
TMM = 512
TPERM = 512
PERM_UNROLL = 8
N_PAIRS = 6
N_BUCKETS = N_GROUPS * N_PAIRS
BUCKET_ROWS = 32
PAIR_LO = (0, 0, 0, 1, 1, 2)
PAIR_HI = (1, 2, 3, 2, 3, 3)
VMEM_LIMIT = 56 * 1024 * 1024


def _lambda_init(layer_idx):
    return 0.8 - 0.6 * math.exp(-0.3 * layer_idx)


def _rms(x, g, eps):
    ms = jnp.mean(x * x, axis=-1, keepdims=True)
    return x * lax.rsqrt(ms + eps) * g


def _load_token_tiled(ref, n_tok, n_chunks, rows=TOK_ROWS):
    return [ref[pl.ds(r, n_tok, stride=rows), :] for r in range(n_chunks)]


def _store_token_tiled(ref, chunks, n_tok, rows=TOK_ROWS):
    for r, c in enumerate(chunks):
        ref[pl.ds(r, n_tok, stride=rows), :] = c


def _route_tail(h, fg_ref, wrt_ref, br_ref, tri_ref, xtt_ref, ri_ref, cnt_ref, run_ref):
    tm = h.shape[0]
    xn = _rms(h, fg_ref[...], RMS_EPS)
    xb = xn.astype(BF16)

    lt = lax.dot_general(wrt_ref[...], xb, (((1,), (1,)), ((), ())), preferred_element_type=F32)
    lt = lt + br_ref[...]
    lg = [lt[k:k + 1, :] for k in range(N_GROUPS)]
    m = jnp.maximum(jnp.maximum(lg[0], lg[1]), jnp.maximum(lg[2], lg[3]))
    gidx = jnp.where(lg[0] == m, 0, jnp.where(lg[1] == m, 1, jnp.where(lg[2] == m, 2, 3))).astype(I32)
    se = jnp.exp(lg[0] - m) + jnp.exp(lg[1] - m) + jnp.exp(lg[2] - m) + jnp.exp(lg[3] - m)
    p_sel = 1.0 / se

    def le_row(g, j):
        r = N_GROUPS + g * EPG + j
        return lt[r:r + 1, :]

    sel = [jnp.where(gidx == 0, le_row(0, j),
                     jnp.where(gidx == 1, le_row(1, j),
                               jnp.where(gidx == 2, le_row(2, j), le_row(3, j)))) for j in range(EPG)]

    def first_argmax(vals):
        v = jnp.maximum(jnp.maximum(vals[0], vals[1]), jnp.maximum(vals[2], vals[3]))
        i = jnp.where(vals[0] == v, 0, jnp.where(vals[1] == v, 1, jnp.where(vals[2] == v, 2, 3))).astype(I32)
        return v, i

    v1, i1 = first_argmax(sel)
    sel2 = [jnp.where(i1 == j, -jnp.inf, sel[j]) for j in range(EPG)]
    v2, i2 = first_argmax(sel2)
    e2 = jnp.exp(v2 - v1)
    den = 1.0 + e2
    w1 = (1.0 / den) * p_sel
    w2 = (e2 / den) * p_sel
    lo = jnp.minimum(i1, i2)
    hi = jnp.maximum(i1, i2)
    first_is_lo = i1 < i2
    w_lo = jnp.where(first_is_lo, w1, w2)
    w_hi = jnp.where(first_is_lo, w2, w1)
    pair = jnp.where(lo == 0, hi - 1, jnp.where(lo == 1, hi + 1, 5))
    bucket = gidx * N_PAIRS + pair

    rows = lax.broadcasted_iota(I32, (BUCKET_ROWS, tm), 0)
    ohf = (rows == bucket).astype(F32)
    cum = jnp.dot(ohf.astype(BF16), tri_ref[...], preferred_element_type=F32)
    run = run_ref[:, 0:1]
    rank = jnp.sum(ohf * (cum - 1.0 + run), axis=0, keepdims=True)
    run_ref[...] = run_ref[...] + jnp.sum(ohf, axis=1, keepdims=True)
    cnt_ref[...] = run_ref[...]

    ri_ref[0:1, :] = bucket.astype(F32)
    ri_ref[1:2, :] = rank
    ri_ref[2:3, :] = w_lo
    ri_ref[3:4, :] = w_hi
    ri_ref[4:SUBLANES, :] = jnp.zeros((SUBLANES - 4, tm), F32)

    _store_token_tiled(xtt_ref, [xn[:, r * LANES:(r + 1) * LANES] for r in range(XCHUNKS)], tm)


def _route_out_shapes(tp):
    return (jax.ShapeDtypeStruct((tp * TOK_ROWS, LANES), F32),
            jax.ShapeDtypeStruct((SUBLANES, tp), F32),
            jax.ShapeDtypeStruct((BUCKET_ROWS, LANES), F32))


def _route_out_specs():
    return (pl.BlockSpec((TM * TOK_ROWS, LANES), lambda i: (i, 0)),
            pl.BlockSpec((SUBLANES, TM), lambda i: (0, i)),
            pl.BlockSpec((BUCKET_ROWS, LANES), lambda i: (0, 0)))


def _route_in_specs():
    return [pl.BlockSpec((1, D), lambda i: (0, 0)),
            pl.BlockSpec((BUCKET_ROWS, D), lambda i: (0, 0)),
            pl.BlockSpec((BUCKET_ROWS, 1), lambda i: (0, 0)),
            pl.BlockSpec((TM, TM), lambda i: (0, 0))]


def _mixer_a_kernel(x_tiles, n_x_tiles, x_ref, meta_ref, g_ref, win_ref, conv_ref, wout_ref,
                    fg_ref, wrt_ref, br_ref, tri_ref,
                    h1_ref, xtt_ref, ri_ref, cnt_ref, zs_ref, zmeta_ref, run_ref):
    i = pl.program_id(0)

    def conv_inputs(h):
        xn = _rms(h, g_ref[...], RMS_EPS).astype(BF16)
        bcu = jnp.dot(xn, win_ref[...], preferred_element_type=F32)
        return bcu, bcu[:, D:2 * D] * bcu[:, 2 * D:3 * D]

    @pl.when(i == 0)
    def _():
        run_ref[...] = jnp.zeros_like(run_ref)
        _, zm = conv_inputs(meta_ref[...])
        zmeta_ref[...] = zm[N_META - SUBLANES:N_META, :]

    @pl.when(i % x_tiles == 0)
    def _():
        zs_ref[0:SUBLANES, :] = zmeta_ref[...]

    @pl.when(i == n_x_tiles)
    def _():
        zs_ref[0:SUBLANES, :] = jnp.zeros((SUBLANES, D), F32)

    h = jnp.where(i < n_x_tiles, x_ref[...], meta_ref[...])
    bcu, z = conv_inputs(h)
    zs_ref[SUBLANES:SUBLANES + TM, :] = z
    cw = conv_ref[...]
    conv = (cw[0:1, :] * zs_ref[SUBLANES - 2:SUBLANES - 2 + TM, :]
            + cw[1:2, :] * zs_ref[SUBLANES - 1:SUBLANES - 1 + TM, :]
            + cw[2:3, :] * z)
    zs_ref[0:SUBLANES, :] = zs_ref[TM:TM + SUBLANES, :]
    mix = jnp.dot((bcu[:, 0:D] * conv).astype(BF16), wout_ref[...], preferred_element_type=F32)
    h1 = h + mix
    h1_ref[...] = h1
    _route_tail(h1, fg_ref, wrt_ref, br_ref, tri_ref, xtt_ref, ri_ref, cnt_ref, run_ref)


def _mixer_a(x2, metapad, g, w_in, conv_w, w_out, fg, wrt, br, tri, x_tiles):
    n_x_tiles = x2.shape[0] // TM
    tp = (n_x_tiles + 1) * TM
    return pl.pallas_call(
        functools.partial(_mixer_a_kernel, x_tiles, n_x_tiles),
        grid=(tp // TM,),
        in_specs=[pl.BlockSpec((TM, D), lambda i: (jnp.minimum(i, n_x_tiles - 1), 0)),
                  pl.BlockSpec((TM, D), lambda i: (0, 0)),
                  pl.BlockSpec((1, D), lambda i: (0, 0)),
                  pl.BlockSpec((D, 3 * D), lambda i: (0, 0)),
                  pl.BlockSpec((3, D), lambda i: (0, 0)),
                  pl.BlockSpec((D, D), lambda i: (0, 0))] + _route_in_specs(),
        out_specs=(pl.BlockSpec((TM, D), lambda i: (i, 0)),) + _route_out_specs(),
        out_shape=(jax.ShapeDtypeStruct((tp, D), F32),) + _route_out_shapes(tp),
        scratch_shapes=[pltpu.VMEM((TM + SUBLANES, D), F32),
                        pltpu.VMEM((SUBLANES, D), F32),
                        pltpu.VMEM((BUCKET_ROWS, LANES), F32)],
        compiler_params=pltpu.CompilerParams(dimension_semantics=("arbitrary",),
                                             vmem_limit_bytes=VMEM_LIMIT),
        name="mixer_a",
    )(x2, metapad, g, w_in, conv_w, w_out, fg, wrt, br, tri)


def _start_token_copies(copy):
    def start(g, c):
        for u in range(PERM_UNROLL):
            copy(g * PERM_UNROLL + u).start(priority=u % 2)
        return c

    lax.fori_loop(0, TPERM // PERM_UNROLL, start, 0)


def _wait_token_copies(copy):
    def wait(g, c):
        for u in range(PERM_UNROLL):
            copy(g * PERM_UNROLL + u).wait()
        return c

    lax.fori_loop(0, TPERM // PERM_UNROLL, wait, 0)


def _tok_rows(ref, t, rows=TOK_ROWS):
    return ref.at[pl.ds(pl.multiple_of(t * rows, rows), rows), :]


def _scatter_kernel(n_dst_tiles, ztile_ref, total_ref, b_ref, bprev_ref, x_ref, ri_ref, dst_ref,
                    stage_ref, zsem, sem):
    i = pl.program_id(0)
    slot = i % 2

    def token_copy(idx_ref, s):
        return lambda k: pltpu.make_async_copy(
            stage_ref.at[s, pl.ds(k * XT_ROWS, XT_ROWS), :], _tok_rows(dst_ref, idx_ref[0, 0, k], XT_ROWS),
            sem.at[s])

    @pl.when(i == 0)
    def _():
        stage_ref[...] = jnp.zeros_like(stage_ref)

        def zero_tile(t):
            rows = TMM * XT_ROWS
            return pltpu.make_async_copy(
                stage_ref.at[0], dst_ref.at[pl.ds(pl.multiple_of(t * rows, rows), rows), :], zsem)

        def each_zero_tile(fn):
            for b in range(N_BUCKETS):
                @pl.when(ztile_ref[b] >= 0)
                def _():
                    fn(zero_tile(ztile_ref[b]))

            def tail(t, c):
                fn(zero_tile(t))
                return c

            lax.fori_loop(total_ref[0], n_dst_tiles, tail, 0)

        each_zero_tile(lambda c: c.start())
        each_zero_tile(lambda c: c.wait())

    for r in range(XCHUNKS):
        stage_ref[slot, pl.ds(r, TPERM, stride=XT_ROWS), :] = x_ref[pl.ds(r, TPERM, stride=TOK_ROWS), :]
    wt = jnp.concatenate([ri_ref[2:4, :], jnp.zeros((LANES - 2, TPERM), F32)], axis=0).T
    stage_ref[slot, pl.ds(XCHUNKS, TPERM, stride=XT_ROWS), :] = wt

    @pl.when(i > 0)
    def _():
        _wait_token_copies(token_copy(bprev_ref, 1 - slot))

    _start_token_copies(token_copy(b_ref, slot))

    @pl.when(i == pl.num_programs(0) - 1)
    def _():
        _wait_token_copies(token_copy(b_ref, slot))


def _scatter_tokens(xtt, ri, b_idx, ztile, total, n_dst_tiles):
    assert TPERM == TMM
    steps = b_idx.shape[0] // TPERM
    b3 = b_idx.reshape(steps, 1, TPERM)
    grid_spec = pltpu.PrefetchScalarGridSpec(
        num_scalar_prefetch=2,
        grid=(steps,),
        in_specs=[pl.BlockSpec((1, 1, TPERM), lambda i, *_: (i, 0, 0), memory_space=pltpu.SMEM),
                  pl.BlockSpec((1, 1, TPERM), lambda i, *_: (jnp.maximum(i - 1, 0), 0, 0),
                               memory_space=pltpu.SMEM),
                  pl.BlockSpec((TPERM * TOK_ROWS, LANES), lambda i, *_: (i, 0)),
                  pl.BlockSpec((SUBLANES, TPERM), lambda i, *_: (0, i))],
        out_specs=pl.BlockSpec(memory_space=pl.ANY),
        scratch_shapes=[pltpu.VMEM((2, TPERM * XT_ROWS, LANES), F32),
                        pltpu.SemaphoreType.DMA, pltpu.SemaphoreType.DMA((2,))],
    )
    return pl.pallas_call(
        functools.partial(_scatter_kernel, n_dst_tiles),
        grid_spec=grid_spec,
        out_shape=jax.ShapeDtypeStruct((n_dst_tiles * TMM * XT_ROWS, LANES), F32),
        compiler_params=pltpu.CompilerParams(dimension_semantics=("arbitrary",),
                                             vmem_limit_bytes=VMEM_LIMIT),
        name="scatter_tokens",
    )(ztile, total.reshape(1), b3, b3, xtt, ri)


def _moe_kernel(blk_ref, ea_ref, eb_ref, nrow_ref, newa_ref, newb_ref,
                xs_ref, wga_f32, wua_f32, wda_f32, wgb_f32, wub_f32, wdb_f32, ys_ref,
                wga_ref, wua_ref, wda_ref, wgb_ref, wub_ref, wdb_ref):
    j = pl.program_id(0)
    nrow = nrow_ref[j]

    @pl.when(newa_ref[j] == 1)
    def _():
        for dst, src in ((wga_ref, wga_f32), (wua_ref, wua_f32), (wda_ref, wda_f32)):
            dst[...] = src[...].astype(BF16)

    @pl.when(newb_ref[j] == 1)
    def _():
        for dst, src in ((wgb_ref, wgb_f32), (wub_ref, wub_f32), (wdb_ref, wdb_f32)):
            dst[...] = src[...].astype(BF16)

    @pl.when(nrow > 0)
    def _():
        rows = _load_token_tiled(xs_ref, TMM, XCHUNKS + 1, XT_ROWS)
        x = jnp.concatenate(rows[:XCHUNKS], axis=1).astype(BF16)
        wrow = rows[XCHUNKS]

        def expert(wg_ref, wu_ref, wd_ref, c):
            g = jnp.dot(x, wg_ref[...], preferred_element_type=F32)
            u = jnp.dot(x, wu_ref[...], preferred_element_type=F32)
            hmid = (jax.nn.silu(g) * u) * c
            return jnp.dot(hmid.astype(BF16), wd_ref[...], preferred_element_type=F32)

        y = expert(wga_ref, wua_ref, wda_ref, wrow[:, 0:1]) + expert(wgb_ref, wub_ref, wdb_ref, wrow[:, 1:2])
        _store_token_tiled(ys_ref, [y[:, r * LANES:(r + 1) * LANES] for r in range(XCHUNKS)], TMM)

    @pl.when(nrow == 0)
    def _():
        ys_ref[...] = jnp.zeros_like(ys_ref)


def _moe(xs, w_gate, w_up, w_down, layer, blk, ea, eb, nrow):
    n_tiles = blk.shape[0]
    p_tok = xs.shape[0] // XT_ROWS
    first = jnp.ones((1,), I32)
    newa = jnp.concatenate([first, (ea[1:] != ea[:-1]).astype(I32)])
    newb = jnp.concatenate([first, (eb[1:] != eb[:-1]).astype(I32)])
    tok_spec = pl.BlockSpec((TMM * XT_ROWS, LANES), lambda j, blk, *_: (blk[j], 0))

    def wspec(shape, which):
        if which == 0:
            return pl.BlockSpec((None, None) + shape, lambda j, blk, ea, eb, *_: (layer, ea[j], 0, 0))
        return pl.BlockSpec((None, None) + shape, lambda j, blk, ea, eb, *_: (layer, eb[j], 0, 0))

    w_in, w_out = (D, D_EXPERT), (D_EXPERT, D)
    grid_spec = pltpu.PrefetchScalarGridSpec(
        num_scalar_prefetch=6,
        grid=(n_tiles,),
        in_specs=[tok_spec, wspec(w_in, 0), wspec(w_in, 0), wspec(w_out, 0),
                  wspec(w_in, 1), wspec(w_in, 1), wspec(w_out, 1)],
        out_specs=pl.BlockSpec((TMM * TOK_ROWS, LANES), lambda j, *_: (j, 0)),
        scratch_shapes=[pltpu.VMEM(w_in, BF16), pltpu.VMEM(w_in, BF16), pltpu.VMEM(w_out, BF16)] * 2,
    )
    return pl.pallas_call(
        _moe_kernel,
        grid_spec=grid_spec,
        out_shape=jax.ShapeDtypeStruct((p_tok * TOK_ROWS, LANES), F32),
        compiler_params=pltpu.CompilerParams(dimension_semantics=("arbitrary",),
                                             vmem_limit_bytes=VMEM_LIMIT),
        name="moe_experts",
    )(blk, ea, eb, nrow, newa, newb, xs, w_gate, w_up, w_down, w_gate, w_up, w_down)


def _count_le(ends, v):
    return jnp.sum((ends[None, :] <= v[:, None]).astype(I32), axis=1)


def _lookup(table, idx):
    hit = idx[:, None] == jnp.arange(table.shape[0], dtype=I32)[None, :]
    return jnp.sum(jnp.where(hit, table[None, :], 0), axis=1)


def _moe_layer(xtt, ri, cnt, w_gate, w_up, w_down, layer, tp):
    counts = cnt[:N_BUCKETS, 0].astype(I32)
    ntile = (counts + TMM - 1) // TMM
    tend = jnp.cumsum(ntile)
    tstart = tend - ntile
    total = tend[-1]
    bucket, rank = ri[0].astype(I32), ri[1].astype(I32)
    pos = _lookup(tstart * TMM, bucket) + rank

    n_tiles = -(-tp // TMM) + N_BUCKETS
    j = jnp.arange(n_tiles, dtype=I32)
    blk = jnp.minimum(j, total - 1)
    tb = jnp.minimum(_count_le(tend, blk), N_BUCKETS - 1)
    grp = tb // N_PAIRS
    pair = tb % N_PAIRS
    ea = grp * EPG + _lookup(jnp.asarray(PAIR_LO, I32), pair)
    eb = grp * EPG + _lookup(jnp.asarray(PAIR_HI, I32), pair)
    nrow = jnp.clip(_lookup(counts, tb) - (j - _lookup(tstart, tb)) * TMM, 0, TMM)
    nrow = jnp.where(j < total, nrow, 0).astype(I32)
    ztile = jnp.where(ntile > 0, tend - 1, -1)

    xs = _scatter_tokens(xtt, ri, pos, ztile, total, n_tiles)
    ys = _moe(xs, w_gate, w_up, w_down, layer, blk, ea, eb, nrow)
    return ys, pos.reshape(tp // TM, 1, TM)


GATHER_AHEAD = 2
GATHER_BUFS = GATHER_AHEAD + 1


def _gather_expert_rows(pos_refs, ys_ref, buf_ref, sem, split_queues):
    i = pl.program_id(0)
    n = pl.num_programs(0)

    def copies(idx_ref, s):
        return [pltpu.make_async_copy(_tok_rows(ys_ref, idx_ref[0, 0, k]),
                                      buf_ref.at[s, pl.ds(k * TOK_ROWS, TOK_ROWS), :], sem.at[s])
                for k in range(TM)]

    def start_all(cs):
        for k, c in enumerate(cs):
            c.start(priority=k % 2 if split_queues else 0)

    @pl.when(i == 0)
    def _():
        for d in range(GATHER_AHEAD):
            start_all(copies(pos_refs[d], d))

    slot = i % GATHER_BUFS
    for c in copies(pos_refs[0], slot):
        c.wait()
    y = jnp.concatenate([buf_ref[slot, pl.ds(r, TM, stride=TOK_ROWS), :] for r in range(XCHUNKS)], axis=1)
    start_all(copies(pos_refs[GATHER_AHEAD], (i + GATHER_AHEAD) % GATHER_BUFS))

    def drain():
        @pl.when(i == n - 1)
        def _():
            for d in range(1, GATHER_BUFS):
                for c in copies(pos_refs[0], (i + d) % GATHER_BUFS):
                    c.wait()

    return y, drain


def _gather_specs(n_steps):
    assert n_steps > GATHER_AHEAD
    ahead = lambda d: pl.BlockSpec((1, 1, TM), lambda i: (jnp.minimum(i + d, n_steps - 1), 0, 0),
                                   memory_space=pltpu.SMEM)
    return [ahead(d) for d in range(GATHER_BUFS)] + [pl.BlockSpec(memory_space=pl.ANY)]


GATHER_SCRATCH = [pltpu.VMEM((GATHER_BUFS, TM * TOK_ROWS, LANES), F32),
                  pltpu.SemaphoreType.DMA((GATHER_BUFS,))]


def _attn_proj_kernel(h_ref, pos0_ref, pos1_ref, pos2_ref, ys_ref, gkv_ref, gq_ref, wk_ref, wvt_ref,
                      wqt_ref, h2_ref, qt_ref, k_ref, vt_ref, buf_ref, sem):
    y, drain = _gather_expert_rows((pos0_ref, pos1_ref, pos2_ref), ys_ref, buf_ref, sem, False)
    h2 = h_ref[...] + y
    h2_ref[...] = h2
    ms = jnp.mean(h2 * h2, axis=-1, keepdims=True)
    xhat = h2 * lax.rsqrt(ms + RMS_EPS)
    xkv = (xhat * gkv_ref[...]).astype(BF16)
    xq = (xhat * gq_ref[...]).astype(BF16)
    nt = (((1,), (1,)), ((), ()))
    k_ref[...] = jnp.dot(xkv, wk_ref[...], preferred_element_type=F32).astype(BF16)
    vt_ref[...] = lax.dot_general(wvt_ref[...], xkv, nt, preferred_element_type=F32).astype(BF16)
    qt = lax.dot_general(wqt_ref[...], xq, nt, preferred_element_type=F32)
    qt_ref[...] = (qt * (LOG2E * HEAD_DIM ** -0.5)).astype(BF16)
    drain()


def _attn_proj(h, pos3, ys, gkv, gq, wk, wvt, wqt):
    tp = h.shape[0]
    steps = tp // TM
    row = pl.BlockSpec((TM, D), lambda i: (i, 0))
    col = pl.BlockSpec((D, TM), lambda i: (0, i))
    vec = pl.BlockSpec((1, D), lambda i: (0, 0))
    mat = pl.BlockSpec((D, D), lambda i: (0, 0))
    return pl.pallas_call(
        _attn_proj_kernel,
        grid=(steps,),
        in_specs=[row] + _gather_specs(steps) + [vec, vec, mat, mat, mat],
        out_specs=(row, col, row, col),
        out_shape=(jax.ShapeDtypeStruct((tp, D), F32), jax.ShapeDtypeStruct((D, tp), BF16),
                   jax.ShapeDtypeStruct((tp, D), BF16), jax.ShapeDtypeStruct((D, tp), BF16)),
        scratch_shapes=GATHER_SCRATCH,
        compiler_params=pltpu.CompilerParams(dimension_semantics=("arbitrary",),
                                             vmem_limit_bytes=VMEM_LIMIT),
        name="attn_proj",
    )(h, *([pos3] * GATHER_BUFS), ys, gkv, gq, wk, wvt, wqt)


def _attn_kernel(lambda_init, qt_ref, k_ref, km_ref, vt_ref, vtm_ref, lam_ref, sg_ref, o_ref,
                 s_ref, m_ref, l_ref, acc_ref):
    hw = 2 * HEAD_DIM
    n_x = o_ref.shape[0]
    half = QB // 2
    assert QB == KB and n_x % QB == 0 and N_META <= LANES
    lam = lam_ref[...]
    lam_full = (jnp.exp(jnp.sum(lam[0:1, :] * lam[1:2, :], axis=1, keepdims=True))
                - jnp.exp(jnp.sum(lam[2:3, :] * lam[3:4, :], axis=1, keepdims=True)) + lambda_init)

    diag = ((0, half, 0, QB), (half, half, half, half))
    qw = QB

    def blocks(r, fn):
        for kb in range(r):
            fn(kb * KB, KB, 0, qw, None)
        for ko, kw, qo, qn in diag:
            fn(r * QB + ko, kw, qo, qn, lambda krow, qcol, d=ko - qo: krow + d <= qcol)
        fn(0, LANES, 0, qw, lambda krow, qcol: krow < N_META, meta=True)

    def score_pass(r):
        s_par, m_par = s_ref.at[r % 2], m_ref.at[r % 2]
        qt = qt_ref[:, r * QB:(r + 1) * QB]
        frow = lax.broadcasted_iota(I32, (hw, qw), 0)
        zero = jnp.zeros_like(qt)
        qc = (jnp.where(frow < HEAD_DIM, qt, zero), jnp.where(frow >= HEAD_DIM, qt, zero))
        m_par[...] = jnp.full(m_par.shape, -jnp.inf, F32)

        def scores(k0, kw, qo, qn, visible, meta=False):
            kblk = km_ref[0:kw, :] if meta else k_ref[pl.ds(k0, kw), :]
            k0 = n_x if meta else k0
            for c in range(2):
                s = jnp.dot(kblk, qc[c][:, qo:qo + qn], preferred_element_type=F32)
                if visible is not None:
                    krow = lax.broadcasted_iota(I32, (kw, qn), 0)
                    qcol = lax.broadcasted_iota(I32, (kw, qn), 1)
                    s = jnp.where(visible(krow, qcol), s, -jnp.inf)
                s_par[c, pl.ds(k0, kw), qo:qo + qn] = s
                smax = jnp.max(s.reshape(kw // SUBLANES, SUBLANES, qn), axis=0)
                m_par[c, :, qo:qo + qn] = jnp.maximum(m_par[c, :, qo:qo + qn], smax)

        blocks(r, scores)

    def value_pass(r):
        s_par, m_par = s_ref.at[r % 2], m_ref.at[r % 2]
        m = [jnp.max(m_par[c], axis=0, keepdims=True) for c in range(2)]
        l_ref[...] = jnp.zeros(l_ref.shape, F32)
        acc_ref[...] = jnp.zeros(acc_ref.shape, F32)

        def weighted_values(k0, kw, qo, qn, visible, meta=False):
            del visible
            vblk = vtm_ref[:, 0:kw] if meta else vt_ref[:, pl.ds(k0, kw)]
            k0 = n_x if meta else k0
            for c in range(2):
                p = jnp.exp2(s_par[c, pl.ds(k0, kw), qo:qo + qn] - m[c][:, qo:qo + qn])
                acc_ref[c, :, qo:qo + qn] += jnp.dot(vblk, p.astype(BF16), preferred_element_type=F32)
                l_ref[c, :, qo:qo + qn] += jnp.sum(p.reshape(kw // SUBLANES, SUBLANES, qn), axis=0)

        blocks(r, weighted_values)
        l = [jnp.sum(l_ref[c], axis=0, keepdims=True) for c in range(2)]
        ot = acc_ref[0] / l[0] - lam_full * (acc_ref[1] / l[1])
        ms = jnp.mean(ot * ot, axis=0, keepdims=True)
        y = ot * lax.rsqrt(ms + SUBLN_EPS) * sg_ref[...] * (1.0 - lambda_init)
        o_ref[r * QB:(r + 1) * QB, :] = y.T.astype(BF16)

    n_blocks = n_x // QB
    score_pass(0)
    for r in range(n_blocks):
        if r + 1 < n_blocks:
            score_pass(r + 1)
        value_pass(r)


def _attention(qt, k, vt, lam, sg, lambda_init, bsz, n_x):
    hw = 2 * HEAD_DIM
    meta_blk = bsz * n_x // TM
    fmaj = pl.BlockSpec((hw, n_x), lambda b, h: (h, b))
    tmaj = pl.BlockSpec((n_x, hw), lambda b, h: (b, h))
    return pl.pallas_call(
        functools.partial(_attn_kernel, lambda_init),
        grid=(bsz, N_HEADS),
        in_specs=[fmaj, tmaj, pl.BlockSpec((TM, hw), lambda b, h: (meta_blk, h)),
                  fmaj, pl.BlockSpec((hw, TM), lambda b, h: (h, meta_blk)),
                  pl.BlockSpec((4, HEAD_DIM), lambda b, h: (0, 0)),
                  pl.BlockSpec((hw, 1), lambda b, h: (0, 0))],
        out_specs=tmaj,
        out_shape=jax.ShapeDtypeStruct((bsz * n_x, D), BF16),
        scratch_shapes=[pltpu.VMEM((2, 2, n_x + LANES, QB), F32), pltpu.VMEM((2, 2, SUBLANES, QB), F32),
                        pltpu.VMEM((2, SUBLANES, QB), F32), pltpu.VMEM((2, hw, QB), F32)],
        compiler_params=pltpu.CompilerParams(dimension_semantics=("arbitrary",) * 2,
                                             vmem_limit_bytes=VMEM_LIMIT),
        name="diff_attention",
    )(qt, k, k, vt, vt, lam, sg)


def _attn_out_kernel(h_ref, o_ref, wo_ref, fg_ref, wrt_ref, br_ref, tri_ref,
                     h3_ref, xtt_ref, ri_ref, cnt_ref, run_ref):
    @pl.when(pl.program_id(0) == 0)
    def _():
        run_ref[...] = jnp.zeros_like(run_ref)

    h3 = h_ref[...] + jnp.dot(o_ref[...], wo_ref[...], preferred_element_type=F32)
    h3_ref[...] = h3
    _route_tail(h3, fg_ref, wrt_ref, br_ref, tri_ref, xtt_ref, ri_ref, cnt_ref, run_ref)


def _attn_out(h, o, wo, fg, wrt, br, tri):
    tp = h.shape[0]
    last_o = o.shape[0] // TM - 1
    row = pl.BlockSpec((TM, D), lambda i: (i, 0))
    return pl.pallas_call(
        _attn_out_kernel,
        grid=(tp // TM,),
        in_specs=[row, pl.BlockSpec((TM, D), lambda i: (jnp.minimum(i, last_o), 0)),
                  pl.BlockSpec((D, D), lambda i: (0, 0))] + _route_in_specs(),
        out_specs=(row,) + _route_out_specs(),
        out_shape=(jax.ShapeDtypeStruct((tp, D), F32),) + _route_out_shapes(tp),
        scratch_shapes=[pltpu.VMEM((BUCKET_ROWS, LANES), F32)],
        compiler_params=pltpu.CompilerParams(dimension_semantics=("arbitrary",),
                                             vmem_limit_bytes=VMEM_LIMIT),
        name="attn_out",
    )(h, o, wo, fg, wrt, br, tri)


def _final_kernel(h_ref, pos0_ref, pos1_ref, pos2_ref, ys_ref, g_ref, o_ref, buf_ref, sem):
    y, drain = _gather_expert_rows((pos0_ref, pos1_ref, pos2_ref), ys_ref, buf_ref, sem, True)
    o_ref[...] = _rms(h_ref[...] + y, g_ref[...], RMS_EPS)
    drain()


def _final(h, pos3, ys, g, n_x_tiles):
    row = pl.BlockSpec((TM, D), lambda i: (i, 0))
    return pl.pallas_call(
        _final_kernel,
        grid=(n_x_tiles,),
        in_specs=[row] + _gather_specs(n_x_tiles) + [pl.BlockSpec((1, D), lambda i: (0, 0))],
        out_specs=row,
        out_shape=jax.ShapeDtypeStruct((n_x_tiles * TM, D), F32),
        scratch_shapes=GATHER_SCRATCH,
        compiler_params=pltpu.CompilerParams(dimension_semantics=("arbitrary",),
                                             vmem_limit_bytes=VMEM_LIMIT),
        name="final_norm",
    )(h, *([pos3] * GATHER_BUFS), ys, g)


def _router_params(w_rg, b_rg, w_re, b_re):
    wr = jnp.concatenate([w_rg, w_re], axis=1)
    wrt = jnp.zeros((BUCKET_ROWS, D), F32).at[:N_GROUPS + N_EXPERTS].set(wr.T).astype(BF16)
    br = jnp.zeros((BUCKET_ROWS, 1), F32).at[:N_GROUPS + N_EXPERTS, 0].set(jnp.concatenate([b_rg, b_re]))
    return wrt, br


def kernel(x, meta_tokens, a_norm, a_w_in, a_conv, a_w_out, kv_norm, w_kv, b_norm, b_w_q, b_lambda, b_subln, b_w_o, ffn_norm, router_group_w, router_group_b, router_expert_w, router_expert_b, expert_w_gate, expert_w_up, expert_w_down, final_norm):
    bsz, seq, d = x.shape
    assert d == D and a_norm.shape[0] == 1 and b_norm.shape[0] == 1
    assert meta_tokens.shape[0] == N_META and seq % TM == 0
    x_tiles = seq // TM
    n_x_tiles = bsz * x_tiles
    tp = (n_x_tiles + 1) * TM
    assert tp % TPERM == 0

    metapad = jnp.concatenate([meta_tokens.astype(x.dtype), jnp.zeros((TM - N_META, D), x.dtype)])
    tri = jnp.triu(jnp.ones((TM, TM), F32)).astype(BF16)
    bf = lambda w: w.astype(BF16)

    wrt, br = _router_params(router_group_w[0], router_group_b[0], router_expert_w[0], router_expert_b[0])
    h, xtt, ri, cnt = _mixer_a(x.reshape(bsz * seq, D), metapad, a_norm[0][None], bf(a_w_in[0]), a_conv[0],
                               bf(a_w_out[0]), ffn_norm[0][None], wrt, br, tri, x_tiles)
    ys, pos3 = _moe_layer(xtt, ri, cnt, expert_w_gate, expert_w_up, expert_w_down, 0, tp)

    h, qt, k, vt = _attn_proj(h, pos3, ys, kv_norm[None], b_norm[0][None],
                              bf(w_kv[:, :D]), bf(w_kv[:, D:].T), bf(b_w_q[0].T))
    o = _attention(qt, k, vt, b_lambda[0], b_subln[0][:, None], _lambda_init(1), bsz, seq)
    wrt, br = _router_params(router_group_w[1], router_group_b[1], router_expert_w[1], router_expert_b[1])
    h, xtt, ri, cnt = _attn_out(h, o, bf(b_w_o[0]), ffn_norm[1][None], wrt, br, tri)
    ys, pos3 = _moe_layer(xtt, ri, cnt, expert_w_gate, expert_w_up, expert_w_down, 1, tp)

    out = _final(h, pos3, ys, final_norm[None], n_x_tiles)
    return out.reshape(bsz, seq, D)
```

```python
import functools
import math

import jax
import jax.numpy as jnp
from jax import lax
from jax.experimental import pallas as pl
from jax.experimental.pallas import tpu as pltpu

F32 = jnp.float32
BF16 = jnp.bfloat16
I32 = jnp.int32

D = 1024
N_META = 16
Q_BLOCK = 128
HEAD_DIM = 64
N_HEADS = D // (2 * HEAD_DIM)
N_GROUPS = 4
EPG = 4
N_EXPERTS = N_GROUPS * EPG
D_EXPERT = D // 2
RMS_EPS = 1e-6
SUBLN_EPS = 1e-5
LOG2E = math.log2(math.e)

LANES = 128
SUBLANES = 8
TOK_ROWS = SUBLANES
XCHUNKS = D // LANES

TM = 512
QB = 512
KB = 512
TMM = 512
TPERM = 512
PERM_UNROLL = 8
N_PAIRS = 6
N_BUCKETS = N_GROUPS * N_PAIRS
BUCKET_ROWS = 32
PAIR_LO = (0, 0, 0, 1, 1, 2)
PAIR_HI = (1, 2, 3, 2, 3, 3)
VMEM_LIMIT = 56 * 1024 * 1024


def _lambda_init(layer_idx):
    return 0.8 - 0.6 * math.exp(-0.3 * layer_idx)


def _rms(x, g, eps):
    ms = jnp.mean(x * x, axis=-1, keepdims=True)
    return x * lax.rsqrt(ms + eps) * g


def _load_token_tiled(ref, n_tok, n_chunks, rows=TOK_ROWS):
    return [ref[pl.ds(r, n_tok, stride=rows), :] for r in range(n_chunks)]


def _store_token_tiled(ref, chunks, n_tok, rows=TOK_ROWS):
    for r, c in enumerate(chunks):
        ref[pl.ds(r, n_tok, stride=rows), :] = c


def _router_logits(wrt_ref, br_ref, xb):
    lt = lax.dot_general(wrt_ref[...], xb, (((1,), (1,)), ((), ())), preferred_element_type=F32)
    return lt + br_ref[...]


def _route_tail(h, fg_ref, wrt_ref, br_ref, tri_ref, xtt_ref, ri_ref, cnt_ref, run_ref):
    tm = h.shape[0]
    xn = _rms(h, fg_ref[...], RMS_EPS)
    xb = xn.astype(BF16)

    lt = _router_logits(wrt_ref, br_ref, xb)
    lg = [lt[k:k + 1, :] for k in range(N_GROUPS)]
    m = jnp.maximum(jnp.maximum(lg[0], lg[1]), jnp.maximum(lg[2], lg[3]))
    gidx = jnp.where(lg[0] == m, 0, jnp.where(lg[1] == m, 1, jnp.where(lg[2] == m, 2, 3))).astype(I32)

    def le_row(g, j):
        r = N_GROUPS + g * EPG + j
        return lt[r:r + 1, :]

    sel = [jnp.where(gidx == 0, le_row(0, j),
                     jnp.where(gidx == 1, le_row(1, j),
                               jnp.where(gidx == 2, le_row(2, j), le_row(3, j)))) for j in range(EPG)]

    def first_argmax(vals):
        v = jnp.maximum(jnp.maximum(vals[0], vals[1]), jnp.maximum(vals[2], vals[3]))
        i = jnp.where(vals[0] == v, 0, jnp.where(vals[1] == v, 1, jnp.where(vals[2] == v, 2, 3))).astype(I32)
        return v, i

    _, i1 = first_argmax(sel)
    sel2 = [jnp.where(i1 == j, -jnp.inf, sel[j]) for j in range(EPG)]
    _, i2 = first_argmax(sel2)
    lo = jnp.minimum(i1, i2)
    hi = jnp.maximum(i1, i2)
    pair = jnp.where(lo == 0, hi - 1, jnp.where(lo == 1, hi + 1, 5))
    bucket = gidx * N_PAIRS + pair

    rows = lax.broadcasted_iota(I32, (BUCKET_ROWS, tm), 0)
    ohf = (rows == bucket).astype(F32)
    cum = jnp.dot(ohf.astype(BF16), tri_ref[...], preferred_element_type=F32)
    run = run_ref[:, 0:1]
    rank = jnp.sum(ohf * (cum - 1.0 + run), axis=0, keepdims=True)
    run_ref[...] = run_ref[...] + jnp.sum(ohf, axis=1, keepdims=True)
    cnt_ref[...] = run_ref[...]

    ri_ref[0:1, :] = bucket.astype(F32)
    ri_ref[1:2, :] = rank
    ri_ref[2:SUBLANES, :] = jnp.zeros((SUBLANES - 2, tm), F32)

    _store_token_tiled(xtt_ref, [xn[:, r * LANES:(r + 1) * LANES] for r in range(XCHUNKS)], tm)


def _route_out_shapes(tp):
    return (jax.ShapeDtypeStruct((tp * TOK_ROWS, LANES), F32),
            jax.ShapeDtypeStruct((SUBLANES, tp), F32),
            jax.ShapeDtypeStruct((BUCKET_ROWS, LANES), F32))


def _route_out_specs():
    return (pl.BlockSpec((TM * TOK_ROWS, LANES), lambda i: (i, 0)),
            pl.BlockSpec((SUBLANES, TM), lambda i: (0, i)),
            pl.BlockSpec((BUCKET_ROWS, LANES), lambda i: (0, 0)))


def _route_in_specs():
    return [pl.BlockSpec((1, D), lambda i: (0, 0)),
            pl.BlockSpec((BUCKET_ROWS, D), lambda i: (0, 0)),
            pl.BlockSpec((BUCKET_ROWS, 1), lambda i: (0, 0)),
            pl.BlockSpec((TM, TM), lambda i: (0, 0))]


def _mixer_a_kernel(x_tiles, n_x_tiles, x_ref, meta_ref, g_ref, win_ref, conv_ref, wout_ref,
                    fg_ref, wrt_ref, br_ref, tri_ref,
                    h1_ref, xtt_ref, ri_ref, cnt_ref, zs_ref, zmeta_ref, run_ref):
    i = pl.program_id(0)

    def conv_inputs(h):
        xn = _rms(h, g_ref[...], RMS_EPS).astype(BF16)
        bcu = jnp.dot(xn, win_ref[...], preferred_element_type=F32)
        return bcu, bcu[:, D:2 * D] * bcu[:, 2 * D:3 * D]

    @pl.when(i == 0)
    def _():
        run_ref[...] = jnp.zeros_like(run_ref)
        _, zm = conv_inputs(meta_ref[...])
        zmeta_ref[...] = zm[N_META - SUBLANES:N_META, :]

    @pl.when(i % x_tiles == 0)
    def _():
        zs_ref[0:SUBLANES, :] = zmeta_ref[...]

    @pl.when(i == n_x_tiles)
    def _():
        zs_ref[0:SUBLANES, :] = jnp.zeros((SUBLANES, D), F32)

    h = jnp.where(i < n_x_tiles, x_ref[...], meta_ref[...])
    bcu, z = conv_inputs(h)
    zs_ref[SUBLANES:SUBLANES + TM, :] = z
    cw = conv_ref[...]
    conv = (cw[0:1, :] * zs_ref[SUBLANES - 2:SUBLANES - 2 + TM, :]
            + cw[1:2, :] * zs_ref[SUBLANES - 1:SUBLANES - 1 + TM, :]
            + cw[2:3, :] * z)
    zs_ref[0:SUBLANES, :] = zs_ref[TM:TM + SUBLANES, :]
    mix = jnp.dot((bcu[:, 0:D] * conv).astype(BF16), wout_ref[...], preferred_element_type=F32)
    h1 = h + mix
    h1_ref[...] = h1
    _route_tail(h1, fg_ref, wrt_ref, br_ref, tri_ref, xtt_ref, ri_ref, cnt_ref, run_ref)


def _mixer_a(x2, metapad, g, w_in, conv_w, w_out, fg, wrt, br, tri, x_tiles):
    n_x_tiles = x2.shape[0] // TM
    tp = (n_x_tiles + 1) * TM
    return pl.pallas_call(
        functools.partial(_mixer_a_kernel, x_tiles, n_x_tiles),
        grid=(tp // TM,),
        in_specs=[pl.BlockSpec((TM, D), lambda i: (jnp.minimum(i, n_x_tiles - 1), 0)),
                  pl.BlockSpec((TM, D), lambda i: (0, 0)),
                  pl.BlockSpec((1, D), lambda i: (0, 0)),
                  pl.BlockSpec((D, 3 * D), lambda i: (0, 0)),
                  pl.BlockSpec((3, D), lambda i: (0, 0)),
                  pl.BlockSpec((D, D), lambda i: (0, 0))] + _route_in_specs(),
        out_specs=(pl.BlockSpec((TM, D), lambda i: (i, 0)),) + _route_out_specs(),
        out_shape=(jax.ShapeDtypeStruct((tp, D), F32),) + _route_out_shapes(tp),
        scratch_shapes=[pltpu.VMEM((TM + SUBLANES, D), F32),
                        pltpu.VMEM((SUBLANES, D), F32),
                        pltpu.VMEM((BUCKET_ROWS, LANES), F32)],
        compiler_params=pltpu.CompilerParams(dimension_semantics=("arbitrary",),
                                             vmem_limit_bytes=VMEM_LIMIT),
        name="mixer_a",
    )(x2, metapad, g, w_in, conv_w, w_out, fg, wrt, br, tri)


def _start_token_copies(copy):
    def start(g, c):
        for u in range(PERM_UNROLL):
            copy(g * PERM_UNROLL + u).start(priority=u % 2)
        return c

    lax.fori_loop(0, TPERM // PERM_UNROLL, start, 0)


def _wait_token_copies(copy):
    def wait(g, c):
        for u in range(PERM_UNROLL):
            copy(g * PERM_UNROLL + u).wait()
        return c

    lax.fori_loop(0, TPERM // PERM_UNROLL, wait, 0)


def _tok_rows(ref, t, rows=TOK_ROWS):
    return ref.at[pl.ds(pl.multiple_of(t * rows, rows), rows), :]


def _scatter_kernel(n_dst_tiles, ztile_ref, total_ref, b_ref, x_ref, dst_ref, zero_ref, zsem, sem):
    token_copy = lambda k: pltpu.make_async_copy(
        _tok_rows(x_ref, k), _tok_rows(dst_ref, b_ref[0, 0, k]), sem)

    @pl.when(pl.program_id(0) == 0)
    def _():
        zero_ref[...] = jnp.zeros_like(zero_ref)

        def zero_tile(t):
            rows = TMM * TOK_ROWS
            return pltpu.make_async_copy(
                zero_ref, dst_ref.at[pl.ds(pl.multiple_of(t * rows, rows), rows), :], zsem)

        def each_zero_tile(fn):
            for b in range(N_BUCKETS):
                @pl.when(ztile_ref[b] >= 0)
                def _():
                    fn(zero_tile(ztile_ref[b]))

            def tail(t, c):
                fn(zero_tile(t))
                return c

            lax.fori_loop(total_ref[0], n_dst_tiles, tail, 0)

        each_zero_tile(lambda c: c.start())
        each_zero_tile(lambda c: c.wait())

    _start_token_copies(token_copy)
    _wait_token_copies(token_copy)


def _scatter_tokens(xtt, b_idx, ztile, total, n_dst_tiles):
    steps = b_idx.shape[0] // TPERM
    grid_spec = pltpu.PrefetchScalarGridSpec(
        num_scalar_prefetch=2,
        grid=(steps,),
        in_specs=[pl.BlockSpec((1, 1, TPERM), lambda i, *_: (i, 0, 0), memory_space=pltpu.SMEM),
                  pl.BlockSpec((TPERM * TOK_ROWS, LANES), lambda i, *_: (i, 0))],
        out_specs=pl.BlockSpec(memory_space=pl.ANY),
        scratch_shapes=[pltpu.VMEM((TMM * TOK_ROWS, LANES), F32),
                        pltpu.SemaphoreType.DMA, pltpu.SemaphoreType.DMA],
    )
    return pl.pallas_call(
        functools.partial(_scatter_kernel, n_dst_tiles),
        grid_spec=grid_spec,
        out_shape=jax.ShapeDtypeStruct((n_dst_tiles * TMM * TOK_ROWS, LANES), F32),
        compiler_params=pltpu.CompilerParams(dimension_semantics=("arbitrary",)),
        name="scatter_tokens",
    )(ztile, total.reshape(1), b_idx.reshape(steps, 1, TPERM), xtt)


def _moe_kernel(blk_ref, ea_ref, eb_ref, nrow_ref, newa_ref, newb_ref,
                xs_ref, wrt_ref, br_ref, wga_f32, wua_f32, wda_f32, wgb_f32, wub_f32, wdb_f32, ys_ref,
                lt_ref, wga_ref, wua_ref, wda_ref, wgb_ref, wub_ref, wdb_ref):
    j = pl.program_id(0)
    nrow = nrow_ref[j]
    ea, eb = ea_ref[j], eb_ref[j]

    @pl.when(newa_ref[j] == 1)
    def _():
        for dst, src in ((wga_ref, wga_f32), (wua_ref, wua_f32), (wda_ref, wda_f32)):
            dst[...] = src[...].astype(BF16)

    @pl.when(newb_ref[j] == 1)
    def _():
        for dst, src in ((wgb_ref, wgb_f32), (wub_ref, wub_f32), (wdb_ref, wdb_f32)):
            dst[...] = src[...].astype(BF16)

    @pl.when(nrow > 0)
    def _():
        x = jnp.concatenate(_load_token_tiled(xs_ref, TMM, XCHUNKS), axis=1).astype(BF16)

        lt_ref[...] = _router_logits(wrt_ref, br_ref, x)
        lg_sel = lt_ref[pl.ds(ea // EPG, 1), :]
        se = sum(jnp.exp(lt_ref[k:k + 1, :] - lg_sel) for k in range(N_GROUPS))
        va = lt_ref[pl.ds(N_GROUPS + ea, 1), :]
        vb = lt_ref[pl.ds(N_GROUPS + eb, 1), :]
        e2 = jnp.exp(jnp.minimum(va, vb) - jnp.maximum(va, vb))
        w_top = (1.0 / (1.0 + e2)) * (1.0 / se)
        w_2nd = (e2 / (1.0 + e2)) * (1.0 / se)
        a_first = va >= vb
        w_a = jnp.where(a_first, w_top, w_2nd)
        w_b = jnp.where(a_first, w_2nd, w_top)
        wrow = jnp.concatenate([w_a, w_b, jnp.zeros((LANES - 2, TMM), F32)], axis=0).T

        def expert(wg_ref, wu_ref, wd_ref, c):
            g = jnp.dot(x, wg_ref[...], preferred_element_type=F32)
            u = jnp.dot(x, wu_ref[...], preferred_element_type=F32)
            hmid = (jax.nn.silu(g) * u) * c
            return jnp.dot(hmid.astype(BF16), wd_ref[...], preferred_element_type=F32)

        y = expert(wga_ref, wua_ref, wda_ref, wrow[:, 0:1]) + expert(wgb_ref, wub_ref, wdb_ref, wrow[:, 1:2])
        _store_token_tiled(ys_ref, [y[:, r * LANES:(r + 1) * LANES] for r in range(XCHUNKS)], TMM)

    @pl.when(nrow == 0)
    def _():
        ys_ref[...] = jnp.zeros_like(ys_ref)


def _moe(xs, wrt, br, w_gate, w_up, w_down, layer, blk, ea, eb, nrow):
    n_tiles = blk.shape[0]
    p_tok = xs.shape[0] // TOK_ROWS
    first = jnp.ones((1,), I32)
    newa = jnp.concatenate([first, (ea[1:] != ea[:-1]).astype(I32)])
    newb = jnp.concatenate([first, (eb[1:] != eb[:-1]).astype(I32)])
    tok_spec = pl.BlockSpec((TMM * TOK_ROWS, LANES), lambda j, blk, *_: (blk[j], 0))
    fixed = lambda shape: pl.BlockSpec(shape, lambda j, *_: (0, 0))

    def wspec(shape, which):
        if which == 0:
            return pl.BlockSpec((None, None) + shape, lambda j, blk, ea, eb, *_: (layer, ea[j], 0, 0))
        return pl.BlockSpec((None, None) + shape, lambda j, blk, ea, eb, *_: (layer, eb[j], 0, 0))

    w_in, w_out = (D, D_EXPERT), (D_EXPERT, D)
    grid_spec = pltpu.PrefetchScalarGridSpec(
        num_scalar_prefetch=6,
        grid=(n_tiles,),
        in_specs=[tok_spec, fixed((BUCKET_ROWS, D)), fixed((BUCKET_ROWS, 1)),
                  wspec(w_in, 0), wspec(w_in, 0), wspec(w_out, 0),
                  wspec(w_in, 1), wspec(w_in, 1), wspec(w_out, 1)],
        out_specs=pl.BlockSpec((TMM * TOK_ROWS, LANES), lambda j, *_: (j, 0)),
        scratch_shapes=[pltpu.VMEM((BUCKET_ROWS, TMM), F32)]
                       + [pltpu.VMEM(w_in, BF16), pltpu.VMEM(w_in, BF16), pltpu.VMEM(w_out, BF16)] * 2,
    )
    return pl.pallas_call(
        _moe_kernel,
        grid_spec=grid_spec,
        out_shape=jax.ShapeDtypeStruct((p_tok * TOK_ROWS, LANES), F32),
        compiler_params=pltpu.CompilerParams(dimension_semantics=("arbitrary",),
                                             vmem_limit_bytes=VMEM_LIMIT),
        name="moe_experts",
    )(blk, ea, eb, nrow, newa, newb, xs, wrt, br, w_gate, w_up, w_down, w_gate, w_up, w_down)


def _count_le(ends, v):
    return jnp.sum((ends[None, :] <= v[:, None]).astype(I32), axis=1)


def _lookup(table, idx):
    hit = idx[:, None] == jnp.arange(table.shape[0], dtype=I32)[None, :]
    return jnp.sum(jnp.where(hit, table[None, :], 0), axis=1)


def _moe_layer(xtt, ri, cnt, wrt, br, w_gate, w_up, w_down, layer, tp):
    counts = cnt[:N_BUCKETS, 0].astype(I32)
    ntile = (counts + TMM - 1) // TMM
    tend = jnp.cumsum(ntile)
    tstart = tend - ntile
    total = tend[-1]
    bucket, rank = ri[0].astype(I32), ri[1].astype(I32)
    pos = _lookup(tstart * TMM, bucket) + rank

    n_tiles = -(-tp // TMM) + N_BUCKETS
    j = jnp.arange(n_tiles, dtype=I32)
    blk = jnp.minimum(j, total - 1)
    tb = jnp.minimum(_count_le(tend, blk), N_BUCKETS - 1)
    grp = tb // N_PAIRS
    pair = tb % N_PAIRS
    ea = grp * EPG + _lookup(jnp.asarray(PAIR_LO, I32), pair)
    eb = grp * EPG + _lookup(jnp.asarray(PAIR_HI, I32), pair)
    nrow = jnp.clip(_lookup(counts, tb) - (j - _lookup(tstart, tb)) * TMM, 0, TMM)
    nrow = jnp.where(j < total, nrow, 0).astype(I32)
    ztile = jnp.where(ntile > 0, tend - 1, -1)

    xs = _scatter_tokens(xtt, pos, ztile, total, n_tiles)
    ys = _moe(xs, wrt, br, w_gate, w_up, w_down, layer, blk, ea, eb, nrow)
    return ys, pos.reshape(tp // TM, 1, TM)


GATHER_AHEAD = 2
GATHER_BUFS = GATHER_AHEAD + 1


def _gather_expert_rows(pos_refs, ys_ref, buf_ref, sem, split_queues):
    i = pl.program_id(0)
    n = pl.num_programs(0)

    def copies(idx_ref, s):
        return [pltpu.make_async_copy(_tok_rows(ys_ref, idx_ref[0, 0, k]),
                                      buf_ref.at[s, pl.ds(k * TOK_ROWS, TOK_ROWS), :], sem.at[s])
                for k in range(TM)]

    def start_all(cs):
        for k, c in enumerate(cs):
            c.start(priority=k % 2 if split_queues else 0)

    @pl.when(i == 0)
    def _():
        for d in range(GATHER_AHEAD):
            start_all(copies(pos_refs[d], d))

    slot = i % GATHER_BUFS
    for c in copies(pos_refs[0], slot):
        c.wait()
    y = jnp.concatenate([buf_ref[slot, pl.ds(r, TM, stride=TOK_ROWS), :] for r in range(XCHUNKS)], axis=1)
    start_all(copies(pos_refs[GATHER_AHEAD], (i + GATHER_AHEAD) % GATHER_BUFS))

    def drain():
        @pl.when(i == n - 1)
        def _():
            for d in range(1, GATHER_BUFS):
                for c in copies(pos_refs[0], (i + d) % GATHER_BUFS):
                    c.wait()

    return y, drain


def _gather_specs(n_steps):
    assert n_steps > GATHER_AHEAD
    ahead = lambda d: pl.BlockSpec((1, 1, TM), lambda i: (jnp.minimum(i + d, n_steps - 1), 0, 0),
                                   memory_space=pltpu.SMEM)
    return [ahead(d) for d in range(GATHER_BUFS)] + [pl.BlockSpec(memory_space=pl.ANY)]


GATHER_SCRATCH = [pltpu.VMEM((GATHER_BUFS, TM * TOK_ROWS, LANES), F32),
                  pltpu.SemaphoreType.DMA((GATHER_BUFS,))]


def _attn_proj_kernel(h_ref, pos0_ref, pos1_ref, pos2_ref, ys_ref, gkv_ref, gq_ref, wk_ref, wvt_ref,
                      wqt_ref, h2_ref, qt_ref, k_ref, vt_ref, buf_ref, sem):
    y, drain = _gather_expert_rows((pos0_ref, pos1_ref, pos2_ref), ys_ref, buf_ref, sem, False)
    h2 = h_ref[...] + y
    h2_ref[...] = h2
    ms = jnp.mean(h2 * h2, axis=-1, keepdims=True)
    xhat = h2 * lax.rsqrt(ms + RMS_EPS)
    xkv = (xhat * gkv_ref[...]).astype(BF16)
    xq = (xhat * gq_ref[...]).astype(BF16)
    nt = (((1,), (1,)), ((), ()))
    k_ref[...] = jnp.dot(xkv, wk_ref[...], preferred_element_type=F32).astype(BF16)
    vt_ref[...] = lax.dot_general(wvt_ref[...], xkv, nt, preferred_element_type=F32).astype(BF16)
    qt = lax.dot_general(wqt_ref[...], xq, nt, preferred_element_type=F32)
    qt_ref[...] = (qt * (LOG2E * HEAD_DIM ** -0.5)).astype(BF16)
    drain()


def _attn_proj(h, pos3, ys, gkv, gq, wk, wvt, wqt):
    tp = h.shape[0]
    steps = tp // TM
    row = pl.BlockSpec((TM, D), lambda i: (i, 0))
    col = pl.BlockSpec((D, TM), lambda i: (0, i))
    vec = pl.BlockSpec((1, D), lambda i: (0, 0))
    mat = pl.BlockSpec((D, D), lambda i: (0, 0))
    return pl.pallas_call(
        _attn_proj_kernel,
        grid=(steps,),
        in_specs=[row] + _gather_specs(steps) + [vec, vec, mat, mat, mat],
        out_specs=(row, col, row, col),
        out_shape=(jax.ShapeDtypeStruct((tp, D), F32), jax.ShapeDtypeStruct((D, tp), BF16),
                   jax.ShapeDtypeStruct((tp, D), BF16), jax.ShapeDtypeStruct((D, tp), BF16)),
        scratch_shapes=GATHER_SCRATCH,
        compiler_params=pltpu.CompilerParams(dimension_semantics=("arbitrary",),
                                             vmem_limit_bytes=VMEM_LIMIT),
        name="attn_proj",
    )(h, *([pos3] * GATHER_BUFS), ys, gkv, gq, wk, wvt, wqt)


def _attn_kernel(lambda_init, qt_ref, k_ref, km_ref, vt_ref, vtm_ref, lam_ref, sg_ref, o_ref,
                 s_ref, m_ref, l_ref, acc_ref):
    hw = 2 * HEAD_DIM
    n_x = o_ref.shape[0]
    half = QB // 2
    assert QB == KB and n_x % QB == 0 and N_META <= LANES
    lam = lam_ref[...]
    lam_full = (jnp.exp(jnp.sum(lam[0:1, :] * lam[1:2, :], axis=1, keepdims=True))
                - jnp.exp(jnp.sum(lam[2:3, :] * lam[3:4, :], axis=1, keepdims=True)) + lambda_init)

    diag = ((0, half, 0, QB), (half, half, half, half))
    qw = QB

    def blocks(r, fn):
        for kb in range(r):
            fn(kb * KB, KB, 0, qw, None)
        for ko, kw, qo, qn in diag:
            fn(r * QB + ko, kw, qo, qn, lambda krow, qcol, d=ko - qo: krow + d <= qcol)
        fn(0, LANES, 0, qw, lambda krow, qcol: krow < N_META, meta=True)

    def score_pass(r):
        s_par, m_par = s_ref.at[r % 2], m_ref.at[r % 2]
        qt = qt_ref[:, r * QB:(r + 1) * QB]
        frow = lax.broadcasted_iota(I32, (hw, qw), 0)
        zero = jnp.zeros_like(qt)
        qc = (jnp.where(frow < HEAD_DIM, qt, zero), jnp.where(frow >= HEAD_DIM, qt, zero))
        m_par[...] = jnp.full(m_par.shape, -jnp.inf, F32)

        def scores(k0, kw, qo, qn, visible, meta=False):
            kblk = km_ref[0:kw, :] if meta else k_ref[pl.ds(k0, kw), :]
            k0 = n_x if meta else k0
            for c in range(2):
                s = jnp.dot(kblk, qc[c][:, qo:qo + qn], preferred_element_type=F32)
                if visible is not None:
                    krow = lax.broadcasted_iota(I32, (kw, qn), 0)
                    qcol = lax.broadcasted_iota(I32, (kw, qn), 1)
                    s = jnp.where(visible(krow, qcol), s, -jnp.inf)
                s_par[c, pl.ds(k0, kw), qo:qo + qn] = s
                smax = jnp.max(s.reshape(kw // SUBLANES, SUBLANES, qn), axis=0)
                m_par[c, :, qo:qo + qn] = jnp.maximum(m_par[c, :, qo:qo + qn], smax)

        blocks(r, scores)

    def value_pass(r):
        s_par, m_par = s_ref.at[r % 2], m_ref.at[r % 2]
        m = [jnp.max(m_par[c], axis=0, keepdims=True) for c in range(2)]
        l_ref[...] = jnp.zeros(l_ref.shape, F32)
        acc_ref[...] = jnp.zeros(acc_ref.shape, F32)

        def weighted_values(k0, kw, qo, qn, visible, meta=False):
            del visible
            vblk = vtm_ref[:, 0:kw] if meta else vt_ref[:, pl.ds(k0, kw)]
            k0 = n_x if meta else k0
            for c in range(2):
                p = jnp.exp2(s_par[c, pl.ds(k0, kw), qo:qo + qn] - m[c][:, qo:qo + qn])
                acc_ref[c, :, qo:qo + qn] += jnp.dot(vblk, p.astype(BF16), preferred_element_type=F32)
                l_ref[c, :, qo:qo + qn] += jnp.sum(p.reshape(kw // SUBLANES, SUBLANES, qn), axis=0)

        blocks(r, weighted_values)
        l = [jnp.sum(l_ref[c], axis=0, keepdims=True) for c in range(2)]
        ot = acc_ref[0] / l[0] - lam_full * (acc_ref[1] / l[1])
        ms = jnp.mean(ot * ot, axis=0, keepdims=True)
        y = ot * lax.rsqrt(ms + SUBLN_EPS) * sg_ref[...] * (1.0 - lambda_init)
        o_ref[r * QB:(r + 1) * QB, :] = y.T.astype(BF16)

    n_blocks = n_x // QB
    score_pass(0)
    for r in range(n_blocks):
        if r + 1 < n_blocks:
            score_pass(r + 1)
        value_pass(r)


def _attention(qt, k, vt, lam, sg, lambda_init, bsz, n_x):
    hw = 2 * HEAD_DIM
    meta_blk = bsz * n_x // TM
    fmaj = pl.BlockSpec((hw, n_x), lambda b, h: (h, b))
    tmaj = pl.BlockSpec((n_x, hw), lambda b, h: (b, h))
    return pl.pallas_call(
        functools.partial(_attn_kernel, lambda_init),
        grid=(bsz, N_HEADS),
        in_specs=[fmaj, tmaj, pl.BlockSpec((TM, hw), lambda b, h: (meta_blk, h)),
                  fmaj, pl.BlockSpec((hw, TM), lambda b, h: (h, meta_blk)),
                  pl.BlockSpec((4, HEAD_DIM), lambda b, h: (0, 0)),
                  pl.BlockSpec((hw, 1), lambda b, h: (0, 0))],
        out_specs=tmaj,
        out_shape=jax.ShapeDtypeStruct((bsz * n_x, D), BF16),
        scratch_shapes=[pltpu.VMEM((2, 2, n_x + LANES, QB), F32), pltpu.VMEM((2, 2, SUBLANES, QB), F32),
                        pltpu.VMEM((2, SUBLANES, QB), F32), pltpu.VMEM((2, hw, QB), F32)],
        compiler_params=pltpu.CompilerParams(dimension_semantics=("arbitrary",) * 2,
                                             vmem_limit_bytes=VMEM_LIMIT),
        name="diff_attention",
    )(qt, k, k, vt, vt, lam, sg)


def _attn_out_kernel(h_ref, o_ref, wo_ref, fg_ref, wrt_ref, br_ref, tri_ref,
                     h3_ref, xtt_ref, ri_ref, cnt_ref, run_ref):
    @pl.when(pl.program_id(0) == 0)
    def _():
        run_ref[...] = jnp.zeros_like(run_ref)

    h3 = h_ref[...] + jnp.dot(o_ref[...], wo_ref[...], preferred_element_type=F32)
    h3_ref[...] = h3
    _route_tail(h3, fg_ref, wrt_ref, br_ref, tri_ref, xtt_ref, ri_ref, cnt_ref, run_ref)


def _attn_out(h, o, wo, fg, wrt, br, tri):
    tp = h.shape[0]
    last_o = o.shape[0] // TM - 1
    row = pl.BlockSpec((TM, D), lambda i: (i, 0))
    return pl.pallas_call(
        _attn_out_kernel,
        grid=(tp // TM,),
        in_specs=[row, pl.BlockSpec((TM, D), lambda i: (jnp.minimum(i, last_o), 0)),
                  pl.BlockSpec((D, D), lambda i: (0, 0))] + _route_in_specs(),
        out_specs=(row,) + _route_out_specs(),
        out_shape=(jax.ShapeDtypeStruct((tp, D), F32),) + _route_out_shapes(tp),
        scratch_shapes=[pltpu.VMEM((BUCKET_ROWS, LANES), F32)],
        compiler_params=pltpu.CompilerParams(dimension_semantics=("arbitrary",),
                                             vmem_limit_bytes=VMEM_LIMIT),
        name="attn_out",
    )(h, o, wo, fg, wrt, br, tri)


def _final_kernel(h_ref, pos0_ref, pos1_ref, pos2_ref, ys_ref, g_ref, o_ref, buf_ref, sem):
    y, drain = _gather_expert_rows((pos0_ref, pos1_ref, pos2_ref), ys_ref, buf_ref, sem, True)
    o_ref[...] = _rms(h_ref[...] + y, g_ref[...], RMS_EPS)
    drain()


def _final(h, pos3, ys, g, n_x_tiles):
    row = pl.BlockSpec((TM, D), lambda i: (i, 0))
    return pl.pallas_call(
        _final_kernel,
        grid=(n_x_tiles,),
        in_specs=[row] + _gather_specs(n_x_tiles) + [pl.BlockSpec((1, D), lambda i: (0, 0))],
        out_specs=row,
        out_shape=jax.ShapeDtypeStruct((n_x_tiles * TM, D), F32),
        scratch_shapes=GATHER_SCRATCH,
        compiler_params=pltpu.CompilerParams(dimension_semantics=("arbitrary",),
                                             vmem_limit_bytes=VMEM_LIMIT),
        name="final_norm",
    )(h, *([pos3] * GATHER_BUFS), ys, g)


def _router_params(w_rg, b_rg, w_re, b_re):
    wr = jnp.concatenate([w_rg, w_re], axis=1)
    wrt = jnp.zeros((BUCKET_ROWS, D), F32).at[:N_GROUPS + N_EXPERTS].set(wr.T).astype(BF16)
    br = jnp.zeros((BUCKET_ROWS, 1), F32).at[:N_GROUPS + N_EXPERTS, 0].set(jnp.concatenate([b_rg, b_re]))
    return wrt, br


def kernel(x, meta_tokens, a_norm, a_w_in, a_conv, a_w_out, kv_norm, w_kv, b_norm, b_w_q, b_lambda, b_subln, b_w_o, ffn_norm, router_group_w, router_group_b, router_expert_w, router_expert_b, expert_w_gate, expert_w_up, expert_w_down, final_norm):
    bsz, seq, d = x.shape
    assert d == D and a_norm.shape[0] == 1 and b_norm.shape[0] == 1
    assert meta_tokens.shape[0] == N_META and seq % TM == 0
    x_tiles = seq // TM
    n_x_tiles = bsz * x_tiles
    tp = (n_x_tiles + 1) * TM
    assert tp % TPERM == 0

    metapad = jnp.concatenate([meta_tokens.astype(x.dtype), jnp.zeros((TM - N_META, D), x.dtype)])
    tri = jnp.triu(jnp.ones((TM, TM), F32)).astype(BF16)
    bf = lambda w: w.astype(BF16)

    wrt, br = _router_params(router_group_w[0], router_group_b[0], router_expert_w[0], router_expert_b[0])
    h, xtt, ri, cnt = _mixer_a(x.reshape(bsz * seq, D), metapad, a_norm[0][None], bf(a_w_in[0]), a_conv[0],
                               bf(a_w_out[0]), ffn_norm[0][None], wrt, br, tri, x_tiles)
    ys, pos3 = _moe_layer(xtt, ri, cnt, wrt, br, expert_w_gate, expert_w_up, expert_w_down, 0, tp)

    h, qt, k, vt = _attn_proj(h, pos3, ys, kv_norm[None], b_norm[0][None],
                              bf(w_kv[:, :D]), bf(w_kv[:, D:].T), bf(b_w_q[0].T))
    o = _attention(qt, k, vt, b_lambda[0], b_subln[0][:, None], _lambda_init(1), bsz, seq)
    wrt, br = _router_params(router_group_w[1], router_group_b[1], router_expert_w[1], router_expert_b[1])
    h, xtt, ri, cnt = _attn_out(h, o, bf(b_w_o[0]), ffn_norm[1][None], wrt, br, tri)
    ys, pos3 = _moe_layer(xtt, ri, cnt, wrt, br, expert_w_gate, expert_w_up, expert_w_down, 1, tp)

    out = _final(h, pos3, ys, final_norm[None], n_x_tiles)
    return out.reshape(bsz, seq, D)
```

```python
import functools
import math

import jax
import jax.numpy as jnp
from jax import lax
from jax.experimental import pallas as pl
from jax.experimental.pallas import tpu as pltpu

F32 = jnp.float32
BF16 = jnp.bfloat16
I32 = jnp.int32

D = 1024
N_META = 16
Q_BLOCK = 128
HEAD_DIM = 64
N_HEADS = D // (2 * HEAD_DIM)
N_GROUPS = 4
EPG = 4
N_EXPERTS = N_GROUPS * EPG
D_EXPERT = D // 2
RMS_EPS = 1e-6
SUBLN_EPS = 1e-5
LOG2E = math.log2(math.e)

LANES = 128
SUBLANES = 8
TOK_ROWS = SUBLANES
XCHUNKS = D // LANES

TM = 512
QB = 512
KB = 512
TMM = 512
TPERM = 512
PERM_UNROLL = 8
N_PAIRS = 6
N_BUCKETS = N_GROUPS * N_PAIRS
BUCKET_ROWS = 32
PAIR_LO = (0, 0, 0, 1, 1, 2)
PAIR_HI = (1, 2, 3, 2, 3, 3)
VMEM_LIMIT = 56 * 1024 * 1024


def _lambda_init(layer_idx):
    return 0.8 - 0.6 * math.exp(-0.3 * layer_idx)


def _rms(x, g, eps):
    ms = jnp.mean(x * x, axis=-1, keepdims=True)
    return x * lax.rsqrt(ms + eps) * g


def _load_token_tiled(ref, n_tok, n_chunks, rows=TOK_ROWS):
    return [ref[pl.ds(r, n_tok, stride=rows), :] for r in range(n_chunks)]


def _store_token_tiled(ref, chunks, n_tok, rows=TOK_ROWS):
    for r, c in enumerate(chunks):
        ref[pl.ds(r, n_tok, stride=rows), :] = c


def _router_logits(wrt_ref, br_ref, xb):
    lt = lax.dot_general(wrt_ref[...], xb, (((1,), (1,)), ((), ())), preferred_element_type=F32)
    return lt + br_ref[...]


def _route_tail(h, fg_ref, wrt_ref, br_ref, tri_ref, xtt_ref, ri_ref, cnt_ref, run_ref):
    tm = h.shape[0]
    xn = _rms(h, fg_ref[...], RMS_EPS)
    xb = xn.astype(BF16)

    lt = _router_logits(wrt_ref, br_ref, xb)
    lg = [lt[k:k + 1, :] for k in range(N_GROUPS)]
    m = jnp.maximum(jnp.maximum(lg[0], lg[1]), jnp.maximum(lg[2], lg[3]))
    gidx = jnp.where(lg[0] == m, 0, jnp.where(lg[1] == m, 1, jnp.where(lg[2] == m, 2, 3))).astype(I32)

    def le_row(g, j):
        r = N_GROUPS + g * EPG + j
        return lt[r:r + 1, :]

    sel = [jnp.where(gidx == 0, le_row(0, j),
                     jnp.where(gidx == 1, le_row(1, j),
                               jnp.where(gidx == 2, le_row(2, j), le_row(3, j)))) for j in range(EPG)]

    def first_argmax(vals):
        v = jnp.maximum(jnp.maximum(vals[0], vals[1]), jnp.maximum(vals[2], vals[3]))
        i = jnp.where(vals[0] == v, 0, jnp.where(vals[1] == v, 1, jnp.where(vals[2] == v, 2, 3))).astype(I32)
        return v, i

    _, i1 = first_argmax(sel)
    sel2 = [jnp.where(i1 == j, -jnp.inf, sel[j]) for j in range(EPG)]
    _, i2 = first_argmax(sel2)
    lo = jnp.minimum(i1, i2)
    hi = jnp.maximum(i1, i2)
    pair = jnp.where(lo == 0, hi - 1, jnp.where(lo == 1, hi + 1, 5))
    bucket = gidx * N_PAIRS + pair

    rows = lax.broadcasted_iota(I32, (BUCKET_ROWS, tm), 0)
    ohf = (rows == bucket).astype(F32)
    cum = jnp.dot(ohf.astype(BF16), tri_ref[...], preferred_element_type=F32)
    run = run_ref[:, 0:1]
    rank = jnp.sum(ohf * (cum - 1.0 + run), axis=0, keepdims=True)
    run_ref[...] = run_ref[...] + jnp.sum(ohf, axis=1, keepdims=True)
    cnt_ref[...] = run_ref[...]

    ri_ref[0:1, :] = bucket.astype(F32)
    ri_ref[1:2, :] = rank
    ri_ref[2:SUBLANES, :] = jnp.zeros((SUBLANES - 2, tm), F32)

    _store_token_tiled(xtt_ref, [xn[:, r * LANES:(r + 1) * LANES] for r in range(XCHUNKS)], tm)


def _route_out_shapes(tp):
    return (jax.ShapeDtypeStruct((tp * TOK_ROWS, LANES), F32),
            jax.ShapeDtypeStruct((SUBLANES, tp), F32),
            jax.ShapeDtypeStruct((BUCKET_ROWS, LANES), F32))


def _route_out_specs():
    return (pl.BlockSpec((TM * TOK_ROWS, LANES), lambda i: (i, 0)),
            pl.BlockSpec((SUBLANES, TM), lambda i: (0, i)),
            pl.BlockSpec((BUCKET_ROWS, LANES), lambda i: (0, 0)))


def _route_in_specs():
    return [pl.BlockSpec((1, D), lambda i: (0, 0)),
            pl.BlockSpec((BUCKET_ROWS, D), lambda i: (0, 0)),
            pl.BlockSpec((BUCKET_ROWS, 1), lambda i: (0, 0)),
            pl.BlockSpec((TM, TM), lambda i: (0, 0))]


def _mixer_a_kernel(x_tiles, n_x_tiles, x_ref, meta_ref, g_ref, win_ref, conv_ref, wout_ref,
                    fg_ref, wrt_ref, br_ref, tri_ref,
                    h1_ref, xtt_ref, ri_ref, cnt_ref, zs_ref, zmeta_ref, run_ref):
    i = pl.program_id(0)

    def conv_inputs(h):
        xn = _rms(h, g_ref[...], RMS_EPS).astype(BF16)
        bcu = jnp.dot(xn, win_ref[...], preferred_element_type=F32)
        return bcu, bcu[:, D:2 * D] * bcu[:, 2 * D:3 * D]

    @pl.when(i == 0)
    def _():
        run_ref[...] = jnp.zeros_like(run_ref)
        _, zm = conv_inputs(meta_ref[...])
        zmeta_ref[...] = zm[N_META - SUBLANES:N_META, :]

    @pl.when(i % x_tiles == 0)
    def _():
        zs_ref[0:SUBLANES, :] = zmeta_ref[...]

    @pl.when(i == n_x_tiles)
    def _():
        zs_ref[0:SUBLANES, :] = jnp.zeros((SUBLANES, D), F32)

    h = jnp.where(i < n_x_tiles, x_ref[...], meta_ref[...])
    bcu, z = conv_inputs(h)
    zs_ref[SUBLANES:SUBLANES + TM, :] = z
    cw = conv_ref[...]
    conv = (cw[0:1, :] * zs_ref[SUBLANES - 2:SUBLANES - 2 + TM, :]
            + cw[1:2, :] * zs_ref[SUBLANES - 1:SUBLANES - 1 + TM, :]
            + cw[2:3, :] * z)
    zs_ref[0:SUBLANES, :] = zs_ref[TM:TM + SUBLANES, :]
    mix = jnp.dot((bcu[:, 0:D] * conv).astype(BF16), wout_ref[...], preferred_element_type=F32)
    h1 = h + mix
    h1_ref[...] = h1
    _route_tail(h1, fg_ref, wrt_ref, br_ref, tri_ref, xtt_ref, ri_ref, cnt_ref, run_ref)


def _mixer_a(x2, metapad, g, w_in, conv_w, w_out, fg, wrt, br, tri, x_tiles):
    n_x_tiles = x2.shape[0] // TM
    tp = (n_x_tiles + 1) * TM
    return pl.pallas_call(
        functools.partial(_mixer_a_kernel, x_tiles, n_x_tiles),
        grid=(tp // TM,),
        in_specs=[pl.BlockSpec((TM, D), lambda i: (jnp.minimum(i, n_x_tiles - 1), 0)),
                  pl.BlockSpec((TM, D), lambda i: (0, 0)),
                  pl.BlockSpec((1, D), lambda i: (0, 0)),
                  pl.BlockSpec((D, 3 * D), lambda i: (0, 0)),
                  pl.BlockSpec((3, D), lambda i: (0, 0)),
                  pl.BlockSpec((D, D), lambda i: (0, 0))] + _route_in_specs(),
        out_specs=(pl.BlockSpec((TM, D), lambda i: (i, 0)),) + _route_out_specs(),
        out_shape=(jax.ShapeDtypeStruct((tp, D), F32),) + _route_out_shapes(tp),
        scratch_shapes=[pltpu.VMEM((TM + SUBLANES, D), F32),
                        pltpu.VMEM((SUBLANES, D), F32),
                        pltpu.VMEM((BUCKET_ROWS, LANES), F32)],
        compiler_params=pltpu.CompilerParams(dimension_semantics=("arbitrary",),
                                             vmem_limit_bytes=VMEM_LIMIT),
        name="mixer_a",
    )(x2, metapad, g, w_in, conv_w, w_out, fg, wrt, br, tri)


def _start_token_copies(copy):
    def start(g, c):
        for u in range(PERM_UNROLL):
            copy(g * PERM_UNROLL + u).start(priority=u % 2)
        return c

    lax.fori_loop(0, TPERM // PERM_UNROLL, start, 0)


def _wait_token_copies(copy):
    def wait(g, c):
        for u in range(PERM_UNROLL):
            copy(g * PERM_UNROLL + u).wait()
        return c

    lax.fori_loop(0, TPERM // PERM_UNROLL, wait, 0)


def _tok_rows(ref, t, rows=TOK_ROWS):
    return ref.at[pl.ds(pl.multiple_of(t * rows, rows), rows), :]


def _scatter_kernel(n_dst_tiles, ztile_ref, total_ref, b_ref, x_ref, dst_ref, zero_ref, zsem, sem):
    token_copy = lambda k: pltpu.make_async_copy(
        _tok_rows(x_ref, k), _tok_rows(dst_ref, b_ref[0, 0, k]), sem)

    @pl.when(pl.program_id(0) == 0)
    def _():
        zero_ref[...] = jnp.zeros_like(zero_ref)

        def zero_tile(t):
            rows = TMM * TOK_ROWS
            return pltpu.make_async_copy(
                zero_ref, dst_ref.at[pl.ds(pl.multiple_of(t * rows, rows), rows), :], zsem)

        def each_zero_tile(fn):
            for b in range(N_BUCKETS):
                @pl.when(ztile_ref[b] >= 0)
                def _():
                    fn(zero_tile(ztile_ref[b]))

            def tail(t, c):
                fn(zero_tile(t))
                return c

            lax.fori_loop(total_ref[0], n_dst_tiles, tail, 0)

        each_zero_tile(lambda c: c.start())
        each_zero_tile(lambda c: c.wait())

    _start_token_copies(token_copy)
    _wait_token_copies(token_copy)


def _scatter_tokens(xtt, b_idx, ztile, total, n_dst_tiles):
    steps = b_idx.shape[0] // TPERM
    grid_spec = pltpu.PrefetchScalarGridSpec(
        num_scalar_prefetch=2,
        grid=(steps,),
        in_specs=[pl.BlockSpec((1, 1, TPERM), lambda i, *_: (i, 0, 0), memory_space=pltpu.SMEM),
                  pl.BlockSpec((TPERM * TOK_ROWS, LANES), lambda i, *_: (i, 0))],
        out_specs=pl.BlockSpec(memory_space=pl.ANY),
        scratch_shapes=[pltpu.VMEM((TMM * TOK_ROWS, LANES), F32),
                        pltpu.SemaphoreType.DMA, pltpu.SemaphoreType.DMA],
    )
    return pl.pallas_call(
        functools.partial(_scatter_kernel, n_dst_tiles),
        grid_spec=grid_spec,
        out_shape=jax.ShapeDtypeStruct((n_dst_tiles * TMM * TOK_ROWS, LANES), F32),
        compiler_params=pltpu.CompilerParams(dimension_semantics=("arbitrary",)),
        name="scatter_tokens",
    )(ztile, total.reshape(1), b_idx.reshape(steps, 1, TPERM), xtt)


def _moe_kernel(blk_ref, ea_ref, eb_ref, nrow_ref, newa_ref, newb_ref,
                xs_ref, wrt_ref, br_ref, wga_f32, wua_f32, wda_f32, wgb_f32, wub_f32, wdb_f32, ys_ref,
                lt_ref, wga_ref, wua_ref, wda_ref, wgb_ref, wub_ref, wdb_ref):
    j = pl.program_id(0)
    nrow = nrow_ref[j]
    ea, eb = ea_ref[j], eb_ref[j]

    @pl.when(newa_ref[j] == 1)
    def _():
        for dst, src in ((wga_ref, wga_f32), (wua_ref, wua_f32), (wda_ref, wda_f32)):
            dst[...] = src[...].astype(BF16)

    @pl.when(newb_ref[j] == 1)
    def _():
        for dst, src in ((wgb_ref, wgb_f32), (wub_ref, wub_f32), (wdb_ref, wdb_f32)):
            dst[...] = src[...].astype(BF16)

    @pl.when(nrow > 0)
    def _():
        x = jnp.concatenate(_load_token_tiled(xs_ref, TMM, XCHUNKS), axis=1).astype(BF16)

        lt_ref[...] = _router_logits(wrt_ref, br_ref, x)
        lg_sel = lt_ref[pl.ds(ea // EPG, 1), :]
        se = sum(jnp.exp(lt_ref[k:k + 1, :] - lg_sel) for k in range(N_GROUPS))
        va = lt_ref[pl.ds(N_GROUPS + ea, 1), :]
        vb = lt_ref[pl.ds(N_GROUPS + eb, 1), :]
        e2 = jnp.exp(jnp.minimum(va, vb) - jnp.maximum(va, vb))
        w_top = (1.0 / (1.0 + e2)) * (1.0 / se)
        w_2nd = (e2 / (1.0 + e2)) * (1.0 / se)
        a_first = va >= vb
        w_a = jnp.where(a_first, w_top, w_2nd)
        w_b = jnp.where(a_first, w_2nd, w_top)
        wrow = jnp.concatenate([w_a, w_b, jnp.zeros((LANES - 2, TMM), F32)], axis=0).T

        def expert(wg_ref, wu_ref, wd_ref, c):
            g = jnp.dot(x, wg_ref[...], preferred_element_type=F32)
            u = jnp.dot(x, wu_ref[...], preferred_element_type=F32)
            hmid = (jax.nn.silu(g) * u) * c
            return jnp.dot(hmid.astype(BF16), wd_ref[...], preferred_element_type=F32)

        y = expert(wga_ref, wua_ref, wda_ref, wrow[:, 0:1]) + expert(wgb_ref, wub_ref, wdb_ref, wrow[:, 1:2])
        _store_token_tiled(ys_ref, [y[:, r * LANES:(r + 1) * LANES] for r in range(XCHUNKS)], TMM)

    @pl.when(nrow == 0)
    def _():
        ys_ref[...] = jnp.zeros_like(ys_ref)


def _moe(xs, wrt, br, w_gate, w_up, w_down, layer, blk, ea, eb, nrow):
    n_tiles = blk.shape[0]
    p_tok = xs.shape[0] // TOK_ROWS
    first = jnp.ones((1,), I32)
    newa = jnp.concatenate([first, (ea[1:] != ea[:-1]).astype(I32)])
    newb = jnp.concatenate([first, (eb[1:] != eb[:-1]).astype(I32)])
    tok_spec = pl.BlockSpec((TMM * TOK_ROWS, LANES), lambda j, blk, *_: (blk[j], 0))
    fixed = lambda shape: pl.BlockSpec(shape, lambda j, *_: (0, 0))

    def wspec(shape, which):
        if which == 0:
            return pl.BlockSpec((None, None) + shape, lambda j, blk, ea, eb, *_: (layer, ea[j], 0, 0))
        return pl.BlockSpec((None, None) + shape, lambda j, blk, ea, eb, *_: (layer, eb[j], 0, 0))

    w_in, w_out = (D, D_EXPERT), (D_EXPERT, D)
    grid_spec = pltpu.PrefetchScalarGridSpec(
        num_scalar_prefetch=6,
        grid=(n_tiles,),
        in_specs=[tok_spec, fixed((BUCKET_ROWS, D)), fixed((BUCKET_ROWS, 1)),
                  wspec(w_in, 0), wspec(w_in, 0), wspec(w_out, 0),
                  wspec(w_in, 1), wspec(w_in, 1), wspec(w_out, 1)],
        out_specs=pl.BlockSpec((TMM * TOK_ROWS, LANES), lambda j, *_: (j, 0)),
        scratch_shapes=[pltpu.VMEM((BUCKET_ROWS, TMM), F32)]
                       + [pltpu.VMEM(w_in, BF16), pltpu.VMEM(w_in, BF16), pltpu.VMEM(w_out, BF16)] * 2,
    )
    return pl.pallas_call(
        _moe_kernel,
        grid_spec=grid_spec,
        out_shape=jax.ShapeDtypeStruct((p_tok * TOK_ROWS, LANES), F32),
        compiler_params=pltpu.CompilerParams(dimension_semantics=("arbitrary",),
                                             vmem_limit_bytes=VMEM_LIMIT),
        name="moe_experts",
    )(blk, ea, eb, nrow, newa, newb, xs, wrt, br, w_gate, w_up, w_down, w_gate, w_up, w_down)


def _count_le(ends, v):
    return jnp.sum((ends[None, :] <= v[:, None]).astype(I32), axis=1)


def _lookup(table, idx):
    hit = idx[:, None] == jnp.arange(table.shape[0], dtype=I32)[None, :]
    return jnp.sum(jnp.where(hit, table[None, :], 0), axis=1)


def _moe_layer(xtt, ri, cnt, wrt, br, w_gate, w_up, w_down, layer, tp):
    counts = cnt[:N_BUCKETS, 0].astype(I32)
    ntile = (counts + TMM - 1) // TMM
    tend = jnp.cumsum(ntile)
    tstart = tend - ntile
    total = tend[-1]
    bucket, rank = ri[0].astype(I32), ri[1].astype(I32)
    pos = _lookup(tstart * TMM, bucket) + rank

    n_tiles = -(-tp // TMM) + N_BUCKETS
    j = jnp.arange(n_tiles, dtype=I32)
    blk = jnp.minimum(j, total - 1)
    tb = jnp.minimum(_count_le(tend, blk), N_BUCKETS - 1)
    grp = tb // N_PAIRS
    pair = tb % N_PAIRS
    ea = grp * EPG + _lookup(jnp.asarray(PAIR_LO, I32), pair)
    eb = grp * EPG + _lookup(jnp.asarray(PAIR_HI, I32), pair)
    nrow = jnp.clip(_lookup(counts, tb) - (j - _lookup(tstart, tb)) * TMM, 0, TMM)
    nrow = jnp.where(j < total, nrow, 0).astype(I32)
    ztile = jnp.where(ntile > 0, tend - 1, -1)

    xs = _scatter_tokens(xtt, pos, ztile, total, n_tiles)
    ys = _moe(xs, wrt, br, w_gate, w_up, w_down, layer, blk, ea, eb, nrow)
    return ys, pos.reshape(tp // TM, 1, TM)


GATHER_AHEAD = 2
GATHER_BUFS = GATHER_AHEAD + 1


def _gather_expert_rows(pos_refs, ys_ref, buf_ref, sem, split_queues):
    i = pl.program_id(0)
    n = pl.num_programs(0)

    def copies(idx_ref, s):
        return [pltpu.make_async_copy(_tok_rows(ys_ref, idx_ref[0, 0, k]),
                                      buf_ref.at[s, pl.ds(k * TOK_ROWS, TOK_ROWS), :], sem.at[s])
                for k in range(TM)]

    def start_all(cs):
        for k, c in enumerate(cs):
            c.start(priority=k % 2 if split_queues else 0)

    @pl.when(i == 0)
    def _():
        for d in range(GATHER_AHEAD):
            start_all(copies(pos_refs[d], d))

    slot = i % GATHER_BUFS
    for c in copies(pos_refs[0], slot):
        c.wait()
    y = jnp.concatenate([buf_ref[slot, pl.ds(r, TM, stride=TOK_ROWS), :] for r in range(XCHUNKS)], axis=1)
    start_all(copies(pos_refs[GATHER_AHEAD], (i + GATHER_AHEAD) % GATHER_BUFS))

    def drain():
        @pl.when(i == n - 1)
        def _():
            for d in range(1, GATHER_BUFS):
                for c in copies(pos_refs[0], (i + d) % GATHER_BUFS):
                    c.wait()

    return y, drain


def _gather_specs(n_steps):
    assert n_steps > GATHER_AHEAD
    ahead = lambda d: pl.BlockSpec((1, 1, TM), lambda i: (jnp.minimum(i + d, n_steps - 1), 0, 0),
                                   memory_space=pltpu.SMEM)
    return [ahead(d) for d in range(GATHER_BUFS)] + [pl.BlockSpec(memory_space=pl.ANY)]


GATHER_SCRATCH = [pltpu.VMEM((GATHER_BUFS, TM * TOK_ROWS, LANES), F32),
                  pltpu.SemaphoreType.DMA((GATHER_BUFS,))]


def _attn_proj_kernel(h_ref, pos0_ref, pos1_ref, pos2_ref, ys_ref, gkv_ref, gq_ref, wk_ref, wvt_ref,
                      wqt_ref, h2_ref, qt_ref, k_ref, vt_ref, buf_ref, sem):
    y, drain = _gather_expert_rows((pos0_ref, pos1_ref, pos2_ref), ys_ref, buf_ref, sem, True)
    h2 = h_ref[...] + y
    h2_ref[...] = h2
    ms = jnp.mean(h2 * h2, axis=-1, keepdims=True)
    xhat = h2 * lax.rsqrt(ms + RMS_EPS)
    xkv = (xhat * gkv_ref[...]).astype(BF16)
    xq = (xhat * gq_ref[...]).astype(BF16)
    nt = (((1,), (1,)), ((), ()))
    k_ref[...] = jnp.dot(xkv, wk_ref[...], preferred_element_type=F32).astype(BF16)
    vt_ref[...] = lax.dot_general(wvt_ref[...], xkv, nt, preferred_element_type=F32).astype(BF16)
    qt = lax.dot_general(wqt_ref[...], xq, nt, preferred_element_type=F32)
    qt_ref[...] = (qt * (LOG2E * HEAD_DIM ** -0.5)).astype(BF16)
    drain()


def _attn_proj(h, pos3, ys, gkv, gq, wk, wvt, wqt):
    tp = h.shape[0]
    steps = tp // TM
    row = pl.BlockSpec((TM, D), lambda i: (i, 0))
    col = pl.BlockSpec((D, TM), lambda i: (0, i))
    vec = pl.BlockSpec((1, D), lambda i: (0, 0))
    mat = pl.BlockSpec((D, D), lambda i: (0, 0))
    return pl.pallas_call(
        _attn_proj_kernel,
        grid=(steps,),
        in_specs=[row] + _gather_specs(steps) + [vec, vec, mat, mat, mat],
        out_specs=(row, col, row, col),
        out_shape=(jax.ShapeDtypeStruct((tp, D), F32), jax.ShapeDtypeStruct((D, tp), BF16),
                   jax.ShapeDtypeStruct((tp, D), BF16), jax.ShapeDtypeStruct((D, tp), BF16)),
        scratch_shapes=GATHER_SCRATCH,
        compiler_params=pltpu.CompilerParams(dimension_semantics=("arbitrary",),
                                             vmem_limit_bytes=VMEM_LIMIT),
        name="attn_proj",
    )(h, *([pos3] * GATHER_BUFS), ys, gkv, gq, wk, wvt, wqt)


def _attn_kernel(lambda_init, qt_ref, k_ref, km_ref, vt_ref, vtm_ref, lam_ref, sg_ref, o_ref,
                 s_ref, m_ref, l_ref, acc_ref):
    hw = 2 * HEAD_DIM
    n_x = o_ref.shape[0]
    half = QB // 2
    assert QB == KB and n_x % QB == 0 and N_META <= LANES
    lam = lam_ref[...]
    lam_full = (jnp.exp(jnp.sum(lam[0:1, :] * lam[1:2, :], axis=1, keepdims=True))
                - jnp.exp(jnp.sum(lam[2:3, :] * lam[3:4, :], axis=1, keepdims=True)) + lambda_init)

    diag = ((0, half, 0, QB), (half, half, half, half))
    qw = QB

    def blocks(r, fn):
        for kb in range(r):
            fn(kb * KB, KB, 0, qw, None)
        for ko, kw, qo, qn in diag:
            fn(r * QB + ko, kw, qo, qn, lambda krow, qcol, d=ko - qo: krow + d <= qcol)
        fn(0, LANES, 0, qw, lambda krow, qcol: krow < N_META, meta=True)

    def score_pass(r):
        s_par, m_par = s_ref.at[r % 2], m_ref.at[r % 2]
        qt = qt_ref[:, r * QB:(r + 1) * QB]
        frow = lax.broadcasted_iota(I32, (hw, qw), 0)
        zero = jnp.zeros_like(qt)
        qc = (jnp.where(frow < HEAD_DIM, qt, zero), jnp.where(frow >= HEAD_DIM, qt, zero))
        m_par[...] = jnp.full(m_par.shape, -jnp.inf, F32)

        def scores(k0, kw, qo, qn, visible, meta=False):
            kblk = km_ref[0:kw, :] if meta else k_ref[pl.ds(k0, kw), :]
            k0 = n_x if meta else k0
            for c in range(2):
                s = jnp.dot(kblk, qc[c][:, qo:qo + qn], preferred_element_type=F32)
                if visible is not None:
                    krow = lax.broadcasted_iota(I32, (kw, qn), 0)
                    qcol = lax.broadcasted_iota(I32, (kw, qn), 1)
                    s = jnp.where(visible(krow, qcol), s, -jnp.inf)
                s_par[c, pl.ds(k0, kw), qo:qo + qn] = s
                smax = jnp.max(s.reshape(kw // SUBLANES, SUBLANES, qn), axis=0)
                m_par[c, :, qo:qo + qn] = jnp.maximum(m_par[c, :, qo:qo + qn], smax)

        blocks(r, scores)

    def value_pass(r):
        s_par, m_par = s_ref.at[r % 2], m_ref.at[r % 2]
        m = [jnp.max(m_par[c], axis=0, keepdims=True) for c in range(2)]
        l_ref[...] = jnp.zeros(l_ref.shape, F32)
        acc_ref[...] = jnp.zeros(acc_ref.shape, F32)

        def weighted_values(k0, kw, qo, qn, visible, meta=False):
            del visible
            vblk = vtm_ref[:, 0:kw] if meta else vt_ref[:, pl.ds(k0, kw)]
            k0 = n_x if meta else k0
            for c in range(2):
                p = jnp.exp2(s_par[c, pl.ds(k0, kw), qo:qo + qn] - m[c][:, qo:qo + qn])
                acc_ref[c, :, qo:qo + qn] += jnp.dot(vblk, p.astype(BF16), preferred_element_type=F32)
                l_ref[c, :, qo:qo + qn] += jnp.sum(p.reshape(kw // SUBLANES, SUBLANES, qn), axis=0)

        blocks(r, weighted_values)
        l = [jnp.sum(l_ref[c], axis=0, keepdims=True) for c in range(2)]
        ot = acc_ref[0] / l[0] - lam_full * (acc_ref[1] / l[1])
        ms = jnp.mean(ot * ot, axis=0, keepdims=True)
        y = ot * lax.rsqrt(ms + SUBLN_EPS) * sg_ref[...] * (1.0 - lambda_init)
        o_ref[r * QB:(r + 1) * QB, :] = y.T.astype(BF16)

    n_blocks = n_x // QB
    score_pass(0)
    for r in range(n_blocks):
        if r + 1 < n_blocks:
            score_pass(r + 1)
        value_pass(r)


def _attention(qt, k, vt, lam, sg, lambda_init, bsz, n_x):
    hw = 2 * HEAD_DIM
    meta_blk = bsz * n_x // TM
    fmaj = pl.BlockSpec((hw, n_x), lambda b, h: (h, b))
    tmaj = pl.BlockSpec((n_x, hw), lambda b, h: (b, h))
    return pl.pallas_call(
        functools.partial(_attn_kernel, lambda_init),
        grid=(bsz, N_HEADS),
        in_specs=[fmaj, tmaj, pl.BlockSpec((TM, hw), lambda b, h: (meta_blk, h)),
                  fmaj, pl.BlockSpec((hw, TM), lambda b, h: (h, meta_blk)),
                  pl.BlockSpec((4, HEAD_DIM), lambda b, h: (0, 0)),
                  pl.BlockSpec((hw, 1), lambda b, h: (0, 0))],
        out_specs=tmaj,
        out_shape=jax.ShapeDtypeStruct((bsz * n_x, D), BF16),
        scratch_shapes=[pltpu.VMEM((2, 2, n_x + LANES, QB), F32), pltpu.VMEM((2, 2, SUBLANES, QB), F32),
                        pltpu.VMEM((2, SUBLANES, QB), F32), pltpu.VMEM((2, hw, QB), F32)],
        compiler_params=pltpu.CompilerParams(dimension_semantics=("arbitrary",) * 2,
                                             vmem_limit_bytes=VMEM_LIMIT),
        name="diff_attention",
    )(qt, k, k, vt, vt, lam, sg)


def _attn_out_kernel(h_ref, o_ref, wo_ref, fg_ref, wrt_ref, br_ref, tri_ref,
                     h3_ref, xtt_ref, ri_ref, cnt_ref, run_ref):
    @pl.when(pl.program_id(0) == 0)
    def _():
        run_ref[...] = jnp.zeros_like(run_ref)

    h3 = h_ref[...] + jnp.dot(o_ref[...], wo_ref[...], preferred_element_type=F32)
    h3_ref[...] = h3
    _route_tail(h3, fg_ref, wrt_ref, br_ref, tri_ref, xtt_ref, ri_ref, cnt_ref, run_ref)


def _attn_out(h, o, wo, fg, wrt, br, tri):
    tp = h.shape[0]
    last_o = o.shape[0] // TM - 1
    row = pl.BlockSpec((TM, D), lambda i: (i, 0))
    return pl.pallas_call(
        _attn_out_kernel,
        grid=(tp // TM,),
        in_specs=[row, pl.BlockSpec((TM, D), lambda i: (jnp.minimum(i, last_o), 0)),
                  pl.BlockSpec((D, D), lambda i: (0, 0))] + _route_in_specs(),
        out_specs=(row,) + _route_out_specs(),
        out_shape=(jax.ShapeDtypeStruct((tp, D), F32),) + _route_out_shapes(tp),
        scratch_shapes=[pltpu.VMEM((BUCKET_ROWS, LANES), F32)],
        compiler_params=pltpu.CompilerParams(dimension_semantics=("arbitrary",),
                                             vmem_limit_bytes=VMEM_LIMIT),
        name="attn_out",
    )(h, o, wo, fg, wrt, br, tri)


def _final_kernel(h_ref, pos0_ref, pos1_ref, pos2_ref, ys_ref, g_ref, o_ref, buf_ref, sem):
    y, drain = _gather_expert_rows((pos0_ref, pos1_ref, pos2_ref), ys_ref, buf_ref, sem, True)
    o_ref[...] = _rms(h_ref[...] + y, g_ref[...], RMS_EPS)
    drain()


def _final(h, pos3, ys, g, n_x_tiles):
    row = pl.BlockSpec((TM, D), lambda i: (i, 0))
    return pl.pallas_call(
        _final_kernel,
        grid=(n_x_tiles,),
        in_specs=[row] + _gather_specs(n_x_tiles) + [pl.BlockSpec((1, D), lambda i: (0, 0))],
        out_specs=row,
        out_shape=jax.ShapeDtypeStruct((n_x_tiles * TM, D), F32),
        scratch_shapes=GATHER_SCRATCH,
        compiler_params=pltpu.CompilerParams(dimension_semantics=("arbitrary",),
                                             vmem_limit_bytes=VMEM_LIMIT),
        name="final_norm",
    )(h, *([pos3] * GATHER_BUFS), ys, g)


def _router_params(w_rg, b_rg, w_re, b_re):
    wr = jnp.concatenate([w_rg, w_re], axis=1)
    wrt = jnp.zeros((BUCKET_ROWS, D), F32).at[:N_GROUPS + N_EXPERTS].set(wr.T).astype(BF16)
    br = jnp.zeros((BUCKET_ROWS, 1), F32).at[:N_GROUPS + N_EXPERTS, 0].set(jnp.concatenate([b_rg, b_re]))
    return wrt, br


def kernel(x, meta_tokens, a_norm, a_w_in, a_conv, a_w_out, kv_norm, w_kv, b_norm, b_w_q, b_lambda, b_subln, b_w_o, ffn_norm, router_group_w, router_group_b, router_expert_w, router_expert_b, expert_w_gate, expert_w_up, expert_w_down, final_norm):
    bsz, seq, d = x.shape
    assert d == D and a_norm.shape[0] == 1 and b_norm.shape[0] == 1
    assert meta_tokens.shape[0] == N_META and seq % TM == 0
    x_tiles = seq // TM
    n_x_tiles = bsz * x_tiles
    tp = (n_x_tiles + 1) * TM
    assert tp % TPERM == 0

    metapad = jnp.concatenate([meta_tokens.astype(x.dtype), jnp.zeros((TM - N_META, D), x.dtype)])
    tri = jnp.triu(jnp.ones((TM, TM), F32)).astype(BF16)
    bf = lambda w: w.astype(BF16)

    wrt, br = _router_params(router_group_w[0], router_group_b[0], router_expert_w[0], router_expert_b[0])
    h, xtt, ri, cnt = _mixer_a(x.reshape(bsz * seq, D), metapad, a_norm[0][None], bf(a_w_in[0]), a_conv[0],
                               bf(a_w_out[0]), ffn_norm[0][None], wrt, br, tri, x_tiles)
    ys, pos3 = _moe_layer(xtt, ri, cnt, wrt, br, expert_w_gate, expert_w_up, expert_w_down, 0, tp)

    h, qt, k, vt = _attn_proj(h, pos3, ys, kv_norm[None], b_norm[0][None],
                              bf(w_kv[:, :D]), bf(w_kv[:, D:].T), bf(b_w_q[0].T))
    o = _attention(qt, k, vt, b_lambda[0], b_subln[0][:, None], _lambda_init(1), bsz, seq)
    wrt, br = _router_params(router_group_w[1], router_group_b[1], router_expert_w[1], router_expert_b[1])
    h, xtt, ri, cnt = _attn_out(h, o, bf(b_w_o[0]), ffn_norm[1][None], wrt, br, tri)
    ys, pos3 = _moe_layer(xtt, ri, cnt, wrt, br, expert_w_gate, expert_w_up, expert_w_down, 1, tp)

    out = _final(h, pos3, ys, final_norm[None], n_x_tiles)
    return out.reshape(bsz, seq, D)
```

```python
import functools
import math

import jax
import jax.numpy as jnp
from jax import lax
from jax.experimental import pallas as pl
from jax.experimental.pallas import tpu as pltpu

F32 = jnp.float32
BF16 = jnp.bfloat16
I32 = jnp.int32

D = 1024
N_META = 16
Q_BLOCK = 128
HEAD_DIM = 64
N_HEADS = D // (2 * HEAD_DIM)
N_GROUPS = 4
EPG = 4
N_EXPERTS = N_GROUPS * EPG
D_EXPERT = D // 2
RMS_EPS = 1e-6
SUBLN_EPS = 1e-5
LOG2E = math.log2(math.e)

LANES = 128
SUBLANES = 8
TOK_ROWS = SUBLANES
XCHUNKS = D // LANES

TM = 512
QB = 512
KB = 512
TMM = 512
TPERM = 512
PERM_UNROLL = 8
N_PAIRS = 6
N_BUCKETS = N_GROUPS * N_PAIRS
BUCKET_ROWS = 32
PAIR_LO = (0, 0, 0, 1, 1, 2)
PAIR_HI = (1, 2, 3, 2, 3, 3)
VMEM_LIMIT = 56 * 1024 * 1024


def _lambda_init(layer_idx):
    return 0.8 - 0.6 * math.exp(-0.3 * layer_idx)


def _rms(x, g, eps):
    ms = jnp.mean(x * x, axis=-1, keepdims=True)
    return x * lax.rsqrt(ms + eps) * g


def _load_token_tiled(ref, n_tok, n_chunks, rows=TOK_ROWS):
    return [ref[pl.ds(r, n_tok, stride=rows), :] for r in range(n_chunks)]


def _store_token_tiled(ref, chunks, n_tok, rows=TOK_ROWS):
    for r, c in enumerate(chunks):
        ref[pl.ds(r, n_tok, stride=rows), :] = c


def _router_logits(wrt_ref, br_ref, xb):
    lt = lax.dot_general(wrt_ref[...], xb, (((1,), (1,)), ((), ())), preferred_element_type=F32)
    return lt + br_ref[...]


def _route_tail(h, fg_ref, wrt_ref, br_ref, tri_ref, xtt_ref, ri_ref, cnt_ref, run_ref):
    tm = h.shape[0]
    xn = _rms(h, fg_ref[...], RMS_EPS)
    xb = xn.astype(BF16)

    lt = _router_logits(wrt_ref, br_ref, xb)
    lg = [lt[k:k + 1, :] for k in range(N_GROUPS)]
    m = jnp.maximum(jnp.maximum(lg[0], lg[1]), jnp.maximum(lg[2], lg[3]))
    gidx = jnp.where(lg[0] == m, 0, jnp.where(lg[1] == m, 1, jnp.where(lg[2] == m, 2, 3))).astype(I32)

    def le_row(g, j):
        r = N_GROUPS + g * EPG + j
        return lt[r:r + 1, :]

    sel = [jnp.where(gidx == 0, le_row(0, j),
                     jnp.where(gidx == 1, le_row(1, j),
                               jnp.where(gidx == 2, le_row(2, j), le_row(3, j)))) for j in range(EPG)]

    def first_argmax(vals):
        v = jnp.maximum(jnp.maximum(vals[0], vals[1]), jnp.maximum(vals[2], vals[3]))
        i = jnp.where(vals[0] == v, 0, jnp.where(vals[1] == v, 1, jnp.where(vals[2] == v, 2, 3))).astype(I32)
        return v, i

    _, i1 = first_argmax(sel)
    sel2 = [jnp.where(i1 == j, -jnp.inf, sel[j]) for j in range(EPG)]
    _, i2 = first_argmax(sel2)
    lo = jnp.minimum(i1, i2)
    hi = jnp.maximum(i1, i2)
    pair = jnp.where(lo == 0, hi - 1, jnp.where(lo == 1, hi + 1, 5))
    bucket = gidx * N_PAIRS + pair

    rows = lax.broadcasted_iota(I32, (BUCKET_ROWS, tm), 0)
    ohf = (rows == bucket).astype(F32)
    cum = jnp.dot(ohf.astype(BF16), tri_ref[...], preferred_element_type=F32)
    run = run_ref[:, 0:1]
    rank = jnp.sum(ohf * (cum - 1.0 + run), axis=0, keepdims=True)
    run_ref[...] = run_ref[...] + jnp.sum(ohf, axis=1, keepdims=True)
    cnt_ref[...] = run_ref[...]

    ri_ref[0:1, :] = bucket.astype(F32)
    ri_ref[1:2, :] = rank
    ri_ref[2:SUBLANES, :] = jnp.zeros((SUBLANES - 2, tm), F32)

    _store_token_tiled(xtt_ref, [xn[:, r * LANES:(r + 1) * LANES] for r in range(XCHUNKS)], tm)


def _route_out_shapes(tp):
    return (jax.ShapeDtypeStruct((tp * TOK_ROWS, LANES), F32),
            jax.ShapeDtypeStruct((SUBLANES, tp), F32),
            jax.ShapeDtypeStruct((BUCKET_ROWS, LANES), F32))


def _route_out_specs():
    return (pl.BlockSpec((TM * TOK_ROWS, LANES), lambda i: (i, 0)),
            pl.BlockSpec((SUBLANES, TM), lambda i: (0, i)),
            pl.BlockSpec((BUCKET_ROWS, LANES), lambda i: (0, 0)))


def _route_in_specs():
    return [pl.BlockSpec((1, D), lambda i: (0, 0)),
            pl.BlockSpec((BUCKET_ROWS, D), lambda i: (0, 0)),
            pl.BlockSpec((BUCKET_ROWS, 1), lambda i: (0, 0)),
            pl.BlockSpec((TM, TM), lambda i: (0, 0))]


def _mixer_a_kernel(x_tiles, n_x_tiles, x_ref, meta_ref, g_ref, win_ref, conv_ref, wout_ref,
                    fg_ref, wrt_ref, br_ref, tri_ref,
                    h1_ref, xtt_ref, ri_ref, cnt_ref, zs_ref, zmeta_ref, run_ref):
    i = pl.program_id(0)

    def conv_inputs(h):
        xn = _rms(h, g_ref[...], RMS_EPS).astype(BF16)
        bcu = jnp.dot(xn, win_ref[...], preferred_element_type=F32)
        return bcu, bcu[:, D:2 * D] * bcu[:, 2 * D:3 * D]

    @pl.when(i == 0)
    def _():
        run_ref[...] = jnp.zeros_like(run_ref)
        _, zm = conv_inputs(meta_ref[...])
        zmeta_ref[...] = zm[N_META - SUBLANES:N_META, :]

    @pl.when(i % x_tiles == 0)
    def _():
        zs_ref[0:SUBLANES, :] = zmeta_ref[...]

    @pl.when(i == n_x_tiles)
    def _():
        zs_ref[0:SUBLANES, :] = jnp.zeros((SUBLANES, D), F32)

    h = jnp.where(i < n_x_tiles, x_ref[...], meta_ref[...])
    bcu, z = conv_inputs(h)
    zs_ref[SUBLANES:SUBLANES + TM, :] = z
    cw = conv_ref[...]
    conv = (cw[0:1, :] * zs_ref[SUBLANES - 2:SUBLANES - 2 + TM, :]
            + cw[1:2, :] * zs_ref[SUBLANES - 1:SUBLANES - 1 + TM, :]
            + cw[2:3, :] * z)
    zs_ref[0:SUBLANES, :] = zs_ref[TM:TM + SUBLANES, :]
    mix = jnp.dot((bcu[:, 0:D] * conv).astype(BF16), wout_ref[...], preferred_element_type=F32)
    h1 = h + mix
    h1_ref[...] = h1
    _route_tail(h1, fg_ref, wrt_ref, br_ref, tri_ref, xtt_ref, ri_ref, cnt_ref, run_ref)


def _mixer_a(x2, metapad, g, w_in, conv_w, w_out, fg, wrt, br, tri, x_tiles):
    n_x_tiles = x2.shape[0] // TM
    tp = (n_x_tiles + 1) * TM
    return pl.pallas_call(
        functools.partial(_mixer_a_kernel, x_tiles, n_x_tiles),
        grid=(tp // TM,),
        in_specs=[pl.BlockSpec((TM, D), lambda i: (jnp.minimum(i, n_x_tiles - 1), 0)),
                  pl.BlockSpec((TM, D), lambda i: (0, 0)),
                  pl.BlockSpec((1, D), lambda i: (0, 0)),
                  pl.BlockSpec((D, 3 * D), lambda i: (0, 0)),
                  pl.BlockSpec((3, D), lambda i: (0, 0)),
                  pl.BlockSpec((D, D), lambda i: (0, 0))] + _route_in_specs(),
        out_specs=(pl.BlockSpec((TM, D), lambda i: (i, 0)),) + _route_out_specs(),
        out_shape=(jax.ShapeDtypeStruct((tp, D), F32),) + _route_out_shapes(tp),
        scratch_shapes=[pltpu.VMEM((TM + SUBLANES, D), F32),
                        pltpu.VMEM((SUBLANES, D), F32),
                        pltpu.VMEM((BUCKET_ROWS, LANES), F32)],
        compiler_params=pltpu.CompilerParams(dimension_semantics=("arbitrary",),
                                             vmem_limit_bytes=VMEM_LIMIT),
        name="mixer_a",
    )(x2, metapad, g, w_in, conv_w, w_out, fg, wrt, br, tri)


def _start_token_copies(copy):
    def start(g, c):
        for u in range(PERM_UNROLL):
            copy(g * PERM_UNROLL + u).start(priority=u % 2)
        return c

    lax.fori_loop(0, TPERM // PERM_UNROLL, start, 0)


def _wait_token_copies(copy):
    def wait(g, c):
        for u in range(PERM_UNROLL):
            copy(g * PERM_UNROLL + u).wait()
        return c

    lax.fori_loop(0, TPERM // PERM_UNROLL, wait, 0)


def _tok_rows(ref, t, rows=TOK_ROWS):
    return ref.at[pl.ds(pl.multiple_of(t * rows, rows), rows), :]


def _scatter_kernel(n_dst_tiles, ztile_ref, total_ref, b_ref, x_ref, dst_ref, zero_ref, zsem, sem):
    token_copy = lambda k: pltpu.make_async_copy(
        _tok_rows(x_ref, k), _tok_rows(dst_ref, b_ref[0, 0, k]), sem)

    @pl.when(pl.program_id(0) == 0)
    def _():
        zero_ref[...] = jnp.zeros_like(zero_ref)

        def zero_tile(t):
            rows = TMM * TOK_ROWS
            return pltpu.make_async_copy(
                zero_ref, dst_ref.at[pl.ds(pl.multiple_of(t * rows, rows), rows), :], zsem)

        def each_zero_tile(fn):
            for b in range(N_BUCKETS):
                @pl.when(ztile_ref[b] >= 0)
                def _():
                    fn(zero_tile(ztile_ref[b]))

            def tail(t, c):
                fn(zero_tile(t))
                return c

            lax.fori_loop(total_ref[0], n_dst_tiles, tail, 0)

        each_zero_tile(lambda c: c.start())
        each_zero_tile(lambda c: c.wait())

    _start_token_copies(token_copy)
    _wait_token_copies(token_copy)


def _scatter_tokens(xtt, b_idx, ztile, total, n_dst_tiles):
    steps = b_idx.shape[0] // TPERM
    grid_spec = pltpu.PrefetchScalarGridSpec(
        num_scalar_prefetch=2,
        grid=(steps,),
        in_specs=[pl.BlockSpec((1, 1, TPERM), lambda i, *_: (i, 0, 0), memory_space=pltpu.SMEM),
                  pl.BlockSpec((TPERM * TOK_ROWS, LANES), lambda i, *_: (i, 0))],
        out_specs=pl.BlockSpec(memory_space=pl.ANY),
        scratch_shapes=[pltpu.VMEM((TMM * TOK_ROWS, LANES), F32),
                        pltpu.SemaphoreType.DMA, pltpu.SemaphoreType.DMA],
    )
    return pl.pallas_call(
        functools.partial(_scatter_kernel, n_dst_tiles),
        grid_spec=grid_spec,
        out_shape=jax.ShapeDtypeStruct((n_dst_tiles * TMM * TOK_ROWS, LANES), F32),
        compiler_params=pltpu.CompilerParams(dimension_semantics=("arbitrary",)),
        name="scatter_tokens",
    )(ztile, total.reshape(1), b_idx.reshape(steps, 1, TPERM), xtt)


def _moe_kernel(blk_ref, ea_ref, eb_ref, nrow_ref, newa_ref, newb_ref,
                xs_ref, wrt_ref, br_ref, wga_f32, wua_f32, wda_f32, wgb_f32, wub_f32, wdb_f32, ys_ref,
                lt_ref, wga_ref, wua_ref, wda_ref, wgb_ref, wub_ref, wdb_ref):
    j = pl.program_id(0)
    nrow = nrow_ref[j]
    ea, eb = ea_ref[j], eb_ref[j]

    @pl.when(newa_ref[j] == 1)
    def _():
        for dst, src in ((wga_ref, wga_f32), (wua_ref, wua_f32), (wda_ref, wda_f32)):
            dst[...] = src[...].astype(BF16)

    @pl.when(newb_ref[j] == 1)
    def _():
        for dst, src in ((wgb_ref, wgb_f32), (wub_ref, wub_f32), (wdb_ref, wdb_f32)):
            dst[...] = src[...].astype(BF16)

    @pl.when(nrow > 0)
    def _():
        x = jnp.concatenate(_load_token_tiled(xs_ref, TMM, XCHUNKS), axis=1).astype(BF16)

        lt_ref[...] = _router_logits(wrt_ref, br_ref, x)
        lg_sel = lt_ref[pl.ds(ea // EPG, 1), :]
        se = sum(jnp.exp(lt_ref[k:k + 1, :] - lg_sel) for k in range(N_GROUPS))
        va = lt_ref[pl.ds(N_GROUPS + ea, 1), :]
        vb = lt_ref[pl.ds(N_GROUPS + eb, 1), :]
        e2 = jnp.exp(jnp.minimum(va, vb) - jnp.maximum(va, vb))
        w_top = (1.0 / (1.0 + e2)) * (1.0 / se)
        w_2nd = (e2 / (1.0 + e2)) * (1.0 / se)
        a_first = va >= vb
        w_a = jnp.where(a_first, w_top, w_2nd)
        w_b = jnp.where(a_first, w_2nd, w_top)
        wrow = jnp.concatenate([w_a, w_b, jnp.zeros((LANES - 2, TMM), F32)], axis=0).T

        def expert(wg_ref, wu_ref, wd_ref, c):
            g = jnp.dot(x, wg_ref[...], preferred_element_type=F32)
            u = jnp.dot(x, wu_ref[...], preferred_element_type=F32)
            hmid = (jax.nn.silu(g) * u) * c
            return jnp.dot(hmid.astype(BF16), wd_ref[...], preferred_element_type=F32)

        y = expert(wga_ref, wua_ref, wda_ref, wrow[:, 0:1]) + expert(wgb_ref, wub_ref, wdb_ref, wrow[:, 1:2])
        _store_token_tiled(ys_ref, [y[:, r * LANES:(r + 1) * LANES] for r in range(XCHUNKS)], TMM)

    @pl.when(nrow == 0)
    def _():
        ys_ref[...] = jnp.zeros_like(ys_ref)


def _moe(xs, wrt, br, w_gate, w_up, w_down, layer, blk, ea, eb, nrow):
    n_tiles = blk.shape[0]
    p_tok = xs.shape[0] // TOK_ROWS
    first = jnp.ones((1,), I32)
    newa = jnp.concatenate([first, (ea[1:] != ea[:-1]).astype(I32)])
    newb = jnp.concatenate([first, (eb[1:] != eb[:-1]).astype(I32)])
    tok_spec = pl.BlockSpec((TMM * TOK_ROWS, LANES), lambda j, blk, *_: (blk[j], 0))
    fixed = lambda shape: pl.BlockSpec(shape, lambda j, *_: (0, 0))

    def wspec(shape, which):
        if which == 0:
            return pl.BlockSpec((None, None) + shape, lambda j, blk, ea, eb, *_: (layer, ea[j], 0, 0))
        return pl.BlockSpec((None, None) + shape, lambda j, blk, ea, eb, *_: (layer, eb[j], 0, 0))

    w_in, w_out = (D, D_EXPERT), (D_EXPERT, D)
    grid_spec = pltpu.PrefetchScalarGridSpec(
        num_scalar_prefetch=6,
        grid=(n_tiles,),
        in_specs=[tok_spec, fixed((BUCKET_ROWS, D)), fixed((BUCKET_ROWS, 1)),
                  wspec(w_in, 0), wspec(w_in, 0), wspec(w_out, 0),
                  wspec(w_in, 1), wspec(w_in, 1), wspec(w_out, 1)],
        out_specs=pl.BlockSpec((TMM * TOK_ROWS, LANES), lambda j, *_: (j, 0)),
        scratch_shapes=[pltpu.VMEM((BUCKET_ROWS, TMM), F32)]
                       + [pltpu.VMEM(w_in, BF16), pltpu.VMEM(w_in, BF16), pltpu.VMEM(w_out, BF16)] * 2,
    )
    return pl.pallas_call(
        _moe_kernel,
        grid_spec=grid_spec,
        out_shape=jax.ShapeDtypeStruct((p_tok * TOK_ROWS, LANES), F32),
        compiler_params=pltpu.CompilerParams(dimension_semantics=("arbitrary",),
                                             vmem_limit_bytes=VMEM_LIMIT),
        name="moe_experts",
    )(blk, ea, eb, nrow, newa, newb, xs, wrt, br, w_gate, w_up, w_down, w_gate, w_up, w_down)


def _count_le(ends, v):
    return jnp.sum((ends[None, :] <= v[:, None]).astype(I32), axis=1)


def _lookup(table, idx):
    hit = idx[:, None] == jnp.arange(table.shape[0], dtype=I32)[None, :]
    return jnp.sum(jnp.where(hit, table[None, :], 0), axis=1)


def _moe_layer(xtt, ri, cnt, wrt, br, w_gate, w_up, w_down, layer, tp):
    counts = cnt[:N_BUCKETS, 0].astype(I32)
    ntile = (counts + TMM - 1) // TMM
    tend = jnp.cumsum(ntile)
    tstart = tend - ntile
    total = tend[-1]
    bucket, rank = ri[0].astype(I32), ri[1].astype(I32)
    pos = _lookup(tstart * TMM, bucket) + rank

    n_tiles = -(-tp // TMM) + N_BUCKETS
    j = jnp.arange(n_tiles, dtype=I32)
    blk = jnp.minimum(j, total - 1)
    tb = jnp.minimum(_count_le(tend, blk), N_BUCKETS - 1)
    grp = tb // N_PAIRS
    pair = tb % N_PAIRS
    ea = grp * EPG + _lookup(jnp.asarray(PAIR_LO, I32), pair)
    eb = grp * EPG + _lookup(jnp.asarray(PAIR_HI, I32), pair)
    nrow = jnp.clip(_lookup(counts, tb) - (j - _lookup(tstart, tb)) * TMM, 0, TMM)
    nrow = jnp.where(j < total, nrow, 0).astype(I32)
    ztile = jnp.where(ntile > 0, tend - 1, -1)

    xs = _scatter_tokens(xtt, pos, ztile, total, n_tiles)
    ys = _moe(xs, wrt, br, w_gate, w_up, w_down, layer, blk, ea, eb, nrow)
    return ys, pos.reshape(tp // TM, 1, TM)


GATHER_AHEAD = 3
GATHER_BUFS = GATHER_AHEAD + 1


def _gather_expert_rows(pos_refs, ys_ref, buf_ref, sem, split_queues):
    i = pl.program_id(0)
    n = pl.num_programs(0)

    def copies(idx_ref, s):
        return [pltpu.make_async_copy(_tok_rows(ys_ref, idx_ref[0, 0, k]),
                                      buf_ref.at[s, pl.ds(k * TOK_ROWS, TOK_ROWS), :], sem.at[s])
                for k in range(TM)]

    def start_all(cs):
        for k, c in enumerate(cs):
            c.start(priority=k % 2 if split_queues else 0)

    @pl.when(i == 0)
    def _():
        for d in range(GATHER_AHEAD):
            start_all(copies(pos_refs[d], d))

    slot = i % GATHER_BUFS
    for c in copies(pos_refs[0], slot):
        c.wait()
    y = jnp.concatenate([buf_ref[slot, pl.ds(r, TM, stride=TOK_ROWS), :] for r in range(XCHUNKS)], axis=1)
    start_all(copies(pos_refs[GATHER_AHEAD], (i + GATHER_AHEAD) % GATHER_BUFS))

    def drain():
        @pl.when(i == n - 1)
        def _():
            for d in range(1, GATHER_BUFS):
                for c in copies(pos_refs[0], (i + d) % GATHER_BUFS):
                    c.wait()

    return y, drain


def _gather_specs(n_steps):
    assert n_steps > GATHER_AHEAD
    ahead = lambda d: pl.BlockSpec((1, 1, TM), lambda i: (jnp.minimum(i + d, n_steps - 1), 0, 0),
                                   memory_space=pltpu.SMEM)
    return [ahead(d) for d in range(GATHER_BUFS)] + [pl.BlockSpec(memory_space=pl.ANY)]


GATHER_SCRATCH = [pltpu.VMEM((GATHER_BUFS, TM * TOK_ROWS, LANES), F32),
                  pltpu.SemaphoreType.DMA((GATHER_BUFS,))]


def _attn_proj_kernel(h_ref, pos0_ref, pos1_ref, pos2_ref, pos3_ref, ys_ref, gkv_ref, gq_ref, wk_ref,
                      wvt_ref, wqt_ref, h2_ref, qt_ref, k_ref, vt_ref, buf_ref, sem):
    y, drain = _gather_expert_rows((pos0_ref, pos1_ref, pos2_ref, pos3_ref), ys_ref, buf_ref, sem, False)
    h2 = h_ref[...] + y
    h2_ref[...] = h2
    ms = jnp.mean(h2 * h2, axis=-1, keepdims=True)
    xhat = h2 * lax.rsqrt(ms + RMS_EPS)
    xkv = (xhat * gkv_ref[...]).astype(BF16)
    xq = (xhat * gq_ref[...]).astype(BF16)
    nt = (((1,), (1,)), ((), ()))
    k_ref[...] = jnp.dot(xkv, wk_ref[...], preferred_element_type=F32).astype(BF16)
    vt_ref[...] = lax.dot_general(wvt_ref[...], xkv, nt, preferred_element_type=F32).astype(BF16)
    qt = lax.dot_general(wqt_ref[...], xq, nt, preferred_element_type=F32)
    qt_ref[...] = (qt * (LOG2E * HEAD_DIM ** -0.5)).astype(BF16)
    drain()


def _attn_proj(h, pos3, ys, gkv, gq, wk, wvt, wqt):
    tp = h.shape[0]
    steps = tp // TM
    row = pl.BlockSpec((TM, D), lambda i: (i, 0))
    col = pl.BlockSpec((D, TM), lambda i: (0, i))
    vec = pl.BlockSpec((1, D), lambda i: (0, 0))
    mat = pl.BlockSpec((D, D), lambda i: (0, 0))
    return pl.pallas_call(
        _attn_proj_kernel,
        grid=(steps,),
        in_specs=[row] + _gather_specs(steps) + [vec, vec, mat, mat, mat],
        out_specs=(row, col, row, col),
        out_shape=(jax.ShapeDtypeStruct((tp, D), F32), jax.ShapeDtypeStruct((D, tp), BF16),
                   jax.ShapeDtypeStruct((tp, D), BF16), jax.ShapeDtypeStruct((D, tp), BF16)),
        scratch_shapes=GATHER_SCRATCH,
        compiler_params=pltpu.CompilerParams(dimension_semantics=("arbitrary",),
                                             vmem_limit_bytes=VMEM_LIMIT),
        name="attn_proj",
    )(h, *([pos3] * GATHER_BUFS), ys, gkv, gq, wk, wvt, wqt)


def _attn_kernel(lambda_init, qt_ref, k_ref, km_ref, vt_ref, vtm_ref, lam_ref, sg_ref, o_ref,
                 s_ref, m_ref, l_ref, acc_ref):
    hw = 2 * HEAD_DIM
    n_x = o_ref.shape[0]
    half = QB // 2
    assert QB == KB and n_x % QB == 0 and N_META <= LANES
    lam = lam_ref[...]
    lam_full = (jnp.exp(jnp.sum(lam[0:1, :] * lam[1:2, :], axis=1, keepdims=True))
                - jnp.exp(jnp.sum(lam[2:3, :] * lam[3:4, :], axis=1, keepdims=True)) + lambda_init)

    diag = ((0, half, 0, QB), (half, half, half, half))
    qw = QB

    def blocks(r, fn):
        for kb in range(r):
            fn(kb * KB, KB, 0, qw, None)
        for ko, kw, qo, qn in diag:
            fn(r * QB + ko, kw, qo, qn, lambda krow, qcol, d=ko - qo: krow + d <= qcol)
        fn(0, LANES, 0, qw, lambda krow, qcol: krow < N_META, meta=True)

    def score_pass(r):
        s_par, m_par = s_ref.at[r % 2], m_ref.at[r % 2]
        qt = qt_ref[:, r * QB:(r + 1) * QB]
        frow = lax.broadcasted_iota(I32, (hw, qw), 0)
        zero = jnp.zeros_like(qt)
        qc = (jnp.where(frow < HEAD_DIM, qt, zero), jnp.where(frow >= HEAD_DIM, qt, zero))
        m_par[...] = jnp.full(m_par.shape, -jnp.inf, F32)

        def scores(k0, kw, qo, qn, visible, meta=False):
            kblk = km_ref[0:kw, :] if meta else k_ref[pl.ds(k0, kw), :]
            k0 = n_x if meta else k0
            for c in range(2):
                s = jnp.dot(kblk, qc[c][:, qo:qo + qn], preferred_element_type=F32)
                if visible is not None:
                    krow = lax.broadcasted_iota(I32, (kw, qn), 0)
                    qcol = lax.broadcasted_iota(I32, (kw, qn), 1)
                    s = jnp.where(visible(krow, qcol), s, -jnp.inf)
                s_par[c, pl.ds(k0, kw), qo:qo + qn] = s
                smax = jnp.max(s.reshape(kw // SUBLANES, SUBLANES, qn), axis=0)
                m_par[c, :, qo:qo + qn] = jnp.maximum(m_par[c, :, qo:qo + qn], smax)

        blocks(r, scores)

    def value_pass(r):
        s_par, m_par = s_ref.at[r % 2], m_ref.at[r % 2]
        m = [jnp.max(m_par[c], axis=0, keepdims=True) for c in range(2)]
        l_ref[...] = jnp.zeros(l_ref.shape, F32)
        acc_ref[...] = jnp.zeros(acc_ref.shape, F32)

        def weighted_values(k0, kw, qo, qn, visible, meta=False):
            del visible
            vblk = vtm_ref[:, 0:kw] if meta else vt_ref[:, pl.ds(k0, kw)]
            k0 = n_x if meta else k0
            for c in range(2):
                p = jnp.exp2(s_par[c, pl.ds(k0, kw), qo:qo + qn] - m[c][:, qo:qo + qn])
                acc_ref[c, :, qo:qo + qn] += jnp.dot(vblk, p.astype(BF16), preferred_element_type=F32)
                l_ref[c, :, qo:qo + qn] += jnp.sum(p.reshape(kw // SUBLANES, SUBLANES, qn), axis=0)

        blocks(r, weighted_values)
        l = [jnp.sum(l_ref[c], axis=0, keepdims=True) for c in range(2)]
        ot = acc_ref[0] / l[0] - lam_full * (acc_ref[1] / l[1])
        ms = jnp.mean(ot * ot, axis=0, keepdims=True)
        y = ot * lax.rsqrt(ms + SUBLN_EPS) * sg_ref[...] * (1.0 - lambda_init)
        o_ref[r * QB:(r + 1) * QB, :] = y.T.astype(BF16)

    n_blocks = n_x // QB
    score_pass(0)
    for r in range(n_blocks):
        if r + 1 < n_blocks:
            score_pass(r + 1)
        value_pass(r)


def _attention(qt, k, vt, lam, sg, lambda_init, bsz, n_x):
    hw = 2 * HEAD_DIM
    meta_blk = bsz * n_x // TM
    fmaj = pl.BlockSpec((hw, n_x), lambda b, h: (h, b))
    tmaj = pl.BlockSpec((n_x, hw), lambda b, h: (b, h))
    return pl.pallas_call(
        functools.partial(_attn_kernel, lambda_init),
        grid=(bsz, N_HEADS),
        in_specs=[fmaj, tmaj, pl.BlockSpec((TM, hw), lambda b, h: (meta_blk, h)),
                  fmaj, pl.BlockSpec((hw, TM), lambda b, h: (h, meta_blk)),
                  pl.BlockSpec((4, HEAD_DIM), lambda b, h: (0, 0)),
                  pl.BlockSpec((hw, 1), lambda b, h: (0, 0))],
        out_specs=tmaj,
        out_shape=jax.ShapeDtypeStruct((bsz * n_x, D), BF16),
        scratch_shapes=[pltpu.VMEM((2, 2, n_x + LANES, QB), F32), pltpu.VMEM((2, 2, SUBLANES, QB), F32),
                        pltpu.VMEM((2, SUBLANES, QB), F32), pltpu.VMEM((2, hw, QB), F32)],
        compiler_params=pltpu.CompilerParams(dimension_semantics=("arbitrary",) * 2,
                                             vmem_limit_bytes=VMEM_LIMIT),
        name="diff_attention",
    )(qt, k, k, vt, vt, lam, sg)


def _attn_out_kernel(h_ref, o_ref, wo_ref, fg_ref, wrt_ref, br_ref, tri_ref,
                     h3_ref, xtt_ref, ri_ref, cnt_ref, run_ref):
    @pl.when(pl.program_id(0) == 0)
    def _():
        run_ref[...] = jnp.zeros_like(run_ref)

    h3 = h_ref[...] + jnp.dot(o_ref[...], wo_ref[...], preferred_element_type=F32)
    h3_ref[...] = h3
    _route_tail(h3, fg_ref, wrt_ref, br_ref, tri_ref, xtt_ref, ri_ref, cnt_ref, run_ref)


def _attn_out(h, o, wo, fg, wrt, br, tri):
    tp = h.shape[0]
    last_o = o.shape[0] // TM - 1
    row = pl.BlockSpec((TM, D), lambda i: (i, 0))
    return pl.pallas_call(
        _attn_out_kernel,
        grid=(tp // TM,),
        in_specs=[row, pl.BlockSpec((TM, D), lambda i: (jnp.minimum(i, last_o), 0)),
                  pl.BlockSpec((D, D), lambda i: (0, 0))] + _route_in_specs(),
        out_specs=(row,) + _route_out_specs(),
        out_shape=(jax.ShapeDtypeStruct((tp, D), F32),) + _route_out_shapes(tp),
        scratch_shapes=[pltpu.VMEM((BUCKET_ROWS, LANES), F32)],
        compiler_params=pltpu.CompilerParams(dimension_semantics=("arbitrary",),
                                             vmem_limit_bytes=VMEM_LIMIT),
        name="attn_out",
    )(h, o, wo, fg, wrt, br, tri)


def _final_kernel(h_ref, pos0_ref, pos1_ref, pos2_ref, pos3_ref, ys_ref, g_ref, o_ref, buf_ref, sem):
    y, drain = _gather_expert_rows((pos0_ref, pos1_ref, pos2_ref, pos3_ref), ys_ref, buf_ref, sem, True)
    o_ref[...] = _rms(h_ref[...] + y, g_ref[...], RMS_EPS)
    drain()


def _final(h, pos3, ys, g, n_x_tiles):
    row = pl.BlockSpec((TM, D), lambda i: (i, 0))
    return pl.pallas_call(
        _final_kernel,
        grid=(n_x_tiles,),
        in_specs=[row] + _gather_specs(n_x_tiles) + [pl.BlockSpec((1, D), lambda i: (0, 0))],
        out_specs=row,
        out_shape=jax.ShapeDtypeStruct((n_x_tiles * TM, D), F32),
        scratch_shapes=GATHER_SCRATCH,
        compiler_params=pltpu.CompilerParams(dimension_semantics=("arbitrary",),
                                             vmem_limit_bytes=VMEM_LIMIT),
        name="final_norm",
    )(h, *([pos3] * GATHER_BUFS), ys, g)


def _router_params(w_rg, b_rg, w_re, b_re):
    wr = jnp.concatenate([w_rg, w_re], axis=1)
    wrt = jnp.zeros((BUCKET_ROWS, D), F32).at[:N_GROUPS + N_EXPERTS].set(wr.T).astype(BF16)
    br = jnp.zeros((BUCKET_ROWS, 1), F32).at[:N_GROUPS + N_EXPERTS, 0].set(jnp.concatenate([b_rg, b_re]))
    return wrt, br


def kernel(x, meta_tokens, a_norm, a_w_in, a_conv, a_w_out, kv_norm, w_kv, b_norm, b_w_q, b_lambda, b_subln, b_w_o, ffn_norm, router_group_w, router_group_b, router_expert_w, router_expert_b, expert_w_gate, expert_w_up, expert_w_down, final_norm):
    bsz, seq, d = x.shape
    assert d == D and a_norm.shape[0] == 1 and b_norm.shape[0] == 1
    assert meta_tokens.shape[0] == N_META and seq % TM == 0
    x_tiles = seq // TM
    n_x_tiles = bsz * x_tiles
    tp = (n_x_tiles + 1) * TM
    assert tp % TPERM == 0

    metapad = jnp.concatenate([meta_tokens.astype(x.dtype), jnp.zeros((TM - N_META, D), x.dtype)])
    tri = jnp.triu(jnp.ones((TM, TM), F32)).astype(BF16)
    bf = lambda w: w.astype(BF16)

    wrt, br = _router_params(router_group_w[0], router_group_b[0], router_expert_w[0], router_expert_b[0])
    h, xtt, ri, cnt = _mixer_a(x.reshape(bsz * seq, D), metapad, a_norm[0][None], bf(a_w_in[0]), a_conv[0],
                               bf(a_w_out[0]), ffn_norm[0][None], wrt, br, tri, x_tiles)
    ys, pos3 = _moe_layer(xtt, ri, cnt, wrt, br, expert_w_gate, expert_w_up, expert_w_down, 0, tp)

    h, qt, k, vt = _attn_proj(h, pos3, ys, kv_norm[None], b_norm[0][None],
                              bf(w_kv[:, :D]), bf(w_kv[:, D:].T), bf(b_w_q[0].T))
    o = _attention(qt, k, vt, b_lambda[0], b_subln[0][:, None], _lambda_init(1), bsz, seq)
    wrt, br = _router_params(router_group_w[1], router_group_b[1], router_expert_w[1], router_expert_b[1])
    h, xtt, ri, cnt = _attn_out(h, o, bf(b_w_o[0]), ffn_norm[1][None], wrt, br, tri)
    ys, pos3 = _moe_layer(xtt, ri, cnt, wrt, br, expert_w_gate, expert_w_up, expert_w_down, 1, tp)

    out = _final(h, pos3, ys, final_norm[None], n_x_tiles)
    return out.reshape(bsz, seq, D)
```

```python
import functools
import math

import jax
import jax.numpy as jnp
from jax import lax
from jax.experimental import pallas as pl
from jax.experimental.pallas import tpu as pltpu

F32 = jnp.float32
BF16 = jnp.bfloat16
I32 = jnp.int32

D = 1024
N_META = 16
Q_BLOCK = 128
HEAD_DIM = 64
N_HEADS = D // (2 * HEAD_DIM)
N_GROUPS = 4
EPG = 4
N_EXPERTS = N_GROUPS * EPG
D_EXPERT = D // 2
RMS_EPS = 1e-6
SUBLN_EPS = 1e-5
LOG2E = math.log2(math.e)

LANES = 128
SUBLANES = 8
TOK_ROWS = SUBLANES
XCHUNKS = D // LANES

TM = 512
QB = 512
KB = 512
TMM = 512
TPERM = 512
PERM_UNROLL = 8
N_PAIRS = 6
N_BUCKETS = N_GROUPS * N_PAIRS
BUCKET_ROWS = 32
PAIR_LO = (0, 0, 0, 1, 1, 2)
PAIR_HI = (1, 2, 3, 2, 3, 3)
VMEM_LIMIT = 56 * 1024 * 1024


def _lambda_init(layer_idx):
    return 0.8 - 0.6 * math.exp(-0.3 * layer_idx)


def _rms(x, g, eps):
    ms = jnp.mean(x * x, axis=-1, keepdims=True)
    return x * lax.rsqrt(ms + eps) * g


def _load_token_tiled(ref, n_tok, n_chunks, rows=TOK_ROWS):
    return [ref[pl.ds(r, n_tok, stride=rows), :] for r in range(n_chunks)]


def _store_token_tiled(ref, chunks, n_tok, rows=TOK_ROWS):
    for r, c in enumerate(chunks):
        ref[pl.ds(r, n_tok, stride=rows), :] = c


def _router_logits(wrt_ref, br_ref, xb):
    lt = lax.dot_general(wrt_ref[...], xb, (((1,), (1,)), ((), ())), preferred_element_type=F32)
    return lt + br_ref[...]


def _route_tail(h, fg_ref, wrt_ref, br_ref, tri_ref, xtt_ref, ri_ref, cnt_ref, run_ref):
    tm = h.shape[0]
    xn = _rms(h, fg_ref[...], RMS_EPS)
    xb = xn.astype(BF16)

    lt = _router_logits(wrt_ref, br_ref, xb)
    lg = [lt[k:k + 1, :] for k in range(N_GROUPS)]
    m = jnp.maximum(jnp.maximum(lg[0], lg[1]), jnp.maximum(lg[2], lg[3]))
    gidx = jnp.where(lg[0] == m, 0, jnp.where(lg[1] == m, 1, jnp.where(lg[2] == m, 2, 3))).astype(I32)

    def le_row(g, j):
        r = N_GROUPS + g * EPG + j
        return lt[r:r + 1, :]

    sel = [jnp.where(gidx == 0, le_row(0, j),
                     jnp.where(gidx == 1, le_row(1, j),
                               jnp.where(gidx == 2, le_row(2, j), le_row(3, j)))) for j in range(EPG)]

    def first_argmax(vals):
        v = jnp.maximum(jnp.maximum(vals[0], vals[1]), jnp.maximum(vals[2], vals[3]))
        i = jnp.where(vals[0] == v, 0, jnp.where(vals[1] == v, 1, jnp.where(vals[2] == v, 2, 3))).astype(I32)
        return v, i

    _, i1 = first_argmax(sel)
    sel2 = [jnp.where(i1 == j, -jnp.inf, sel[j]) for j in range(EPG)]
    _, i2 = first_argmax(sel2)
    lo = jnp.minimum(i1, i2)
    hi = jnp.maximum(i1, i2)
    pair = jnp.where(lo == 0, hi - 1, jnp.where(lo == 1, hi + 1, 5))
    bucket = gidx * N_PAIRS + pair

    rows = lax.broadcasted_iota(I32, (BUCKET_ROWS, tm), 0)
    ohf = (rows == bucket).astype(F32)
    cum = jnp.dot(ohf.astype(BF16), tri_ref[...], preferred_element_type=F32)
    run = run_ref[:, 0:1]
    rank = jnp.sum(ohf * (cum - 1.0 + run), axis=0, keepdims=True)
    run_ref[...] = run_ref[...] + jnp.sum(ohf, axis=1, keepdims=True)
    cnt_ref[...] = run_ref[...]

    ri_ref[0:1, :] = bucket.astype(F32)
    ri_ref[1:2, :] = rank
    ri_ref[2:SUBLANES, :] = jnp.zeros((SUBLANES - 2, tm), F32)

    _store_token_tiled(xtt_ref, [xn[:, r * LANES:(r + 1) * LANES] for r in range(XCHUNKS)], tm)


def _route_out_shapes(tp):
    return (jax.ShapeDtypeStruct((tp * TOK_ROWS, LANES), F32),
            jax.ShapeDtypeStruct((SUBLANES, tp), F32),
            jax.ShapeDtypeStruct((BUCKET_ROWS, LANES), F32))


def _route_out_specs():
    return (pl.BlockSpec((TM * TOK_ROWS, LANES), lambda i: (i, 0)),
            pl.BlockSpec((SUBLANES, TM), lambda i: (0, i)),
            pl.BlockSpec((BUCKET_ROWS, LANES), lambda i: (0, 0)))


def _route_in_specs():
    return [pl.BlockSpec((1, D), lambda i: (0, 0)),
            pl.BlockSpec((BUCKET_ROWS, D), lambda i: (0, 0)),
            pl.BlockSpec((BUCKET_ROWS, 1), lambda i: (0, 0)),
            pl.BlockSpec((TM, TM), lambda i: (0, 0))]


def _mixer_a_kernel(x_tiles, n_x_tiles, x_ref, meta_ref, g_ref, win_ref, conv_ref, wout_ref,
                    fg_ref, wrt_ref, br_ref, tri_ref,
                    h1_ref, xtt_ref, ri_ref, cnt_ref, zs_ref, zmeta_ref, run_ref):
    i = pl.program_id(0)

    def conv_inputs(h):
        xn = _rms(h, g_ref[...], RMS_EPS).astype(BF16)
        bcu = jnp.dot(xn, win_ref[...], preferred_element_type=F32)
        return bcu, bcu[:, D:2 * D] * bcu[:, 2 * D:3 * D]

    @pl.when(i == 0)
    def _():
        run_ref[...] = jnp.zeros_like(run_ref)
        _, zm = conv_inputs(meta_ref[...])
        zmeta_ref[...] = zm[N_META - SUBLANES:N_META, :]

    @pl.when(i % x_tiles == 0)
    def _():
        zs_ref[0:SUBLANES, :] = zmeta_ref[...]

    @pl.when(i == n_x_tiles)
    def _():
        zs_ref[0:SUBLANES, :] = jnp.zeros((SUBLANES, D), F32)

    h = jnp.where(i < n_x_tiles, x_ref[...], meta_ref[...])
    bcu, z = conv_inputs(h)
    zs_ref[SUBLANES:SUBLANES + TM, :] = z
    cw = conv_ref[...]
    conv = (cw[0:1, :] * zs_ref[SUBLANES - 2:SUBLANES - 2 + TM, :]
            + cw[1:2, :] * zs_ref[SUBLANES - 1:SUBLANES - 1 + TM, :]
            + cw[2:3, :] * z)
    zs_ref[0:SUBLANES, :] = zs_ref[TM:TM + SUBLANES, :]
    mix = jnp.dot((bcu[:, 0:D] * conv).astype(BF16), wout_ref[...], preferred_element_type=F32)
    h1 = h + mix
    h1_ref[...] = h1
    _route_tail(h1, fg_ref, wrt_ref, br_ref, tri_ref, xtt_ref, ri_ref, cnt_ref, run_ref)


def _mixer_a(x2, metapad, g, w_in, conv_w, w_out, fg, wrt, br, tri, x_tiles):
    n_x_tiles = x2.shape[0] // TM
    tp = (n_x_tiles + 1) * TM
    return pl.pallas_call(
        functools.partial(_mixer_a_kernel, x_tiles, n_x_tiles),
        grid=(tp // TM,),
        in_specs=[pl.BlockSpec((TM, D), lambda i: (jnp.minimum(i, n_x_tiles - 1), 0)),
                  pl.BlockSpec((TM, D), lambda i: (0, 0)),
                  pl.BlockSpec((1, D), lambda i: (0, 0)),
                  pl.BlockSpec((D, 3 * D), lambda i: (0, 0)),
                  pl.BlockSpec((3, D), lambda i: (0, 0)),
                  pl.BlockSpec((D, D), lambda i: (0, 0))] + _route_in_specs(),
        out_specs=(pl.BlockSpec((TM, D), lambda i: (i, 0)),) + _route_out_specs(),
        out_shape=(jax.ShapeDtypeStruct((tp, D), F32),) + _route_out_shapes(tp),
        scratch_shapes=[pltpu.VMEM((TM + SUBLANES, D), F32),
                        pltpu.VMEM((SUBLANES, D), F32),
                        pltpu.VMEM((BUCKET_ROWS, LANES), F32)],
        compiler_params=pltpu.CompilerParams(dimension_semantics=("arbitrary",),
                                             vmem_limit_bytes=VMEM_LIMIT),
        name="mixer_a",
    )(x2, metapad, g, w_in, conv_w, w_out, fg, wrt, br, tri)


def _start_token_copies(copy):
    def start(g, c):
        for u in range(PERM_UNROLL):
            copy(g * PERM_UNROLL + u).start(priority=u % 2)
        return c

    lax.fori_loop(0, TPERM // PERM_UNROLL, start, 0)


def _wait_token_copies(copy):
    def wait(g, c):
        for u in range(PERM_UNROLL):
            copy(g * PERM_UNROLL + u).wait()
        return c

    lax.fori_loop(0, TPERM // PERM_UNROLL, wait, 0)


def _tok_rows(ref, t, rows=TOK_ROWS):
    return ref.at[pl.ds(pl.multiple_of(t * rows, rows), rows), :]


def _scatter_kernel(n_dst_tiles, ztile_ref, total_ref, b_ref, x_ref, dst_ref, zero_ref, zsem, sem):
    token_copy = lambda k: pltpu.make_async_copy(
        _tok_rows(x_ref, k), _tok_rows(dst_ref, b_ref[0, 0, k]), sem)

    @pl.when(pl.program_id(0) == 0)
    def _():
        zero_ref[...] = jnp.zeros_like(zero_ref)

        def zero_tile(t):
            rows = TMM * TOK_ROWS
            return pltpu.make_async_copy(
                zero_ref, dst_ref.at[pl.ds(pl.multiple_of(t * rows, rows), rows), :], zsem)

        def each_zero_tile(fn):
            for b in range(N_BUCKETS):
                @pl.when(ztile_ref[b] >= 0)
                def _():
                    fn(zero_tile(ztile_ref[b]))

            def tail(t, c):
                fn(zero_tile(t))
                return c

            lax.fori_loop(total_ref[0], n_dst_tiles, tail, 0)

        each_zero_tile(lambda c: c.start())
        each_zero_tile(lambda c: c.wait())

    _start_token_copies(token_copy)
    _wait_token_copies(token_copy)


def _scatter_tokens(xtt, b_idx, ztile, total, n_dst_tiles):
    steps = b_idx.shape[0] // TPERM
    grid_spec = pltpu.PrefetchScalarGridSpec(
        num_scalar_prefetch=2,
        grid=(steps,),
        in_specs=[pl.BlockSpec((1, 1, TPERM), lambda i, *_: (i, 0, 0), memory_space=pltpu.SMEM),
                  pl.BlockSpec((TPERM * TOK_ROWS, LANES), lambda i, *_: (i, 0))],
        out_specs=pl.BlockSpec(memory_space=pl.ANY),
        scratch_shapes=[pltpu.VMEM((TMM * TOK_ROWS, LANES), F32),
                        pltpu.SemaphoreType.DMA, pltpu.SemaphoreType.DMA],
    )
    return pl.pallas_call(
        functools.partial(_scatter_kernel, n_dst_tiles),
        grid_spec=grid_spec,
        out_shape=jax.ShapeDtypeStruct((n_dst_tiles * TMM * TOK_ROWS, LANES), F32),
        compiler_params=pltpu.CompilerParams(dimension_semantics=("arbitrary",)),
        name="scatter_tokens",
    )(ztile, total.reshape(1), b_idx.reshape(steps, 1, TPERM), xtt)


def _moe_kernel(blk_ref, ea_ref, eb_ref, nrow_ref, newa_ref, newb_ref,
                xs_ref, wrt_ref, br_ref, wga_f32, wua_f32, wda_f32, wgb_f32, wub_f32, wdb_f32, ys_ref,
                lt_ref, wga_ref, wua_ref, wda_ref, wgb_ref, wub_ref, wdb_ref):
    j = pl.program_id(0)
    nrow = nrow_ref[j]
    ea, eb = ea_ref[j], eb_ref[j]

    @pl.when(newa_ref[j] == 1)
    def _():
        for dst, src in ((wga_ref, wga_f32), (wua_ref, wua_f32), (wda_ref, wda_f32)):
            dst[...] = src[...].astype(BF16)

    @pl.when(newb_ref[j] == 1)
    def _():
        for dst, src in ((wgb_ref, wgb_f32), (wub_ref, wub_f32), (wdb_ref, wdb_f32)):
            dst[...] = src[...].astype(BF16)

    @pl.when(nrow > 0)
    def _():
        x = jnp.concatenate(_load_token_tiled(xs_ref, TMM, XCHUNKS), axis=1).astype(BF16)

        lt_ref[...] = _router_logits(wrt_ref, br_ref, x)
        lg_sel = lt_ref[pl.ds(ea // EPG, 1), :]
        se = sum(jnp.exp(lt_ref[k:k + 1, :] - lg_sel) for k in range(N_GROUPS))
        va = lt_ref[pl.ds(N_GROUPS + ea, 1), :]
        vb = lt_ref[pl.ds(N_GROUPS + eb, 1), :]
        e2 = jnp.exp(jnp.minimum(va, vb) - jnp.maximum(va, vb))
        w_top = (1.0 / (1.0 + e2)) * (1.0 / se)
        w_2nd = (e2 / (1.0 + e2)) * (1.0 / se)
        a_first = va >= vb
        w_a = jnp.where(a_first, w_top, w_2nd)
        w_b = jnp.where(a_first, w_2nd, w_top)
        wrow = jnp.concatenate([w_a, w_b, jnp.zeros((LANES - 2, TMM), F32)], axis=0).T

        def expert(wg_ref, wu_ref, wd_ref, c):
            g = jnp.dot(x, wg_ref[...], preferred_element_type=F32)
            u = jnp.dot(x, wu_ref[...], preferred_element_type=F32)
            hmid = (jax.nn.silu(g) * u) * c
            return jnp.dot(hmid.astype(BF16), wd_ref[...], preferred_element_type=F32)

        y = expert(wga_ref, wua_ref, wda_ref, wrow[:, 0:1]) + expert(wgb_ref, wub_ref, wdb_ref, wrow[:, 1:2])
        _store_token_tiled(ys_ref, [y[:, r * LANES:(r + 1) * LANES] for r in range(XCHUNKS)], TMM)

    @pl.when(nrow == 0)
    def _():
        ys_ref[...] = jnp.zeros_like(ys_ref)


def _moe(xs, wrt, br, w_gate, w_up, w_down, layer, blk, ea, eb, nrow):
    n_tiles = blk.shape[0]
    p_tok = xs.shape[0] // TOK_ROWS
    first = jnp.ones((1,), I32)
    newa = jnp.concatenate([first, (ea[1:] != ea[:-1]).astype(I32)])
    newb = jnp.concatenate([first, (eb[1:] != eb[:-1]).astype(I32)])
    tok_spec = pl.BlockSpec((TMM * TOK_ROWS, LANES), lambda j, blk, *_: (blk[j], 0))
    fixed = lambda shape: pl.BlockSpec(shape, lambda j, *_: (0, 0))

    def wspec(shape, which):
        if which == 0:
            return pl.BlockSpec((None, None) + shape, lambda j, blk, ea, eb, *_: (layer, ea[j], 0, 0))
        return pl.BlockSpec((None, None) + shape, lambda j, blk, ea, eb, *_: (layer, eb[j], 0, 0))

    w_in, w_out = (D, D_EXPERT), (D_EXPERT, D)
    grid_spec = pltpu.PrefetchScalarGridSpec(
        num_scalar_prefetch=6,
        grid=(n_tiles,),
        in_specs=[tok_spec, fixed((BUCKET_ROWS, D)), fixed((BUCKET_ROWS, 1)),
                  wspec(w_in, 0), wspec(w_in, 0), wspec(w_out, 0),
                  wspec(w_in, 1), wspec(w_in, 1), wspec(w_out, 1)],
        out_specs=pl.BlockSpec((TMM * TOK_ROWS, LANES), lambda j, *_: (j, 0)),
        scratch_shapes=[pltpu.VMEM((BUCKET_ROWS, TMM), F32)]
                       + [pltpu.VMEM(w_in, BF16), pltpu.VMEM(w_in, BF16), pltpu.VMEM(w_out, BF16)] * 2,
    )
    return pl.pallas_call(
        _moe_kernel,
        grid_spec=grid_spec,
        out_shape=jax.ShapeDtypeStruct((p_tok * TOK_ROWS, LANES), F32),
        compiler_params=pltpu.CompilerParams(dimension_semantics=("arbitrary",),
                                             vmem_limit_bytes=VMEM_LIMIT),
        name="moe_experts",
    )(blk, ea, eb, nrow, newa, newb, xs, wrt, br, w_gate, w_up, w_down, w_gate, w_up, w_down)


def _count_le(ends, v):
    return jnp.sum((ends[None, :] <= v[:, None]).astype(I32), axis=1)


def _lookup(table, idx):
    hit = idx[:, None] == jnp.arange(table.shape[0], dtype=I32)[None, :]
    return jnp.sum(jnp.where(hit, table[None, :], 0), axis=1)


def _moe_layer(xtt, ri, cnt, wrt, br, w_gate, w_up, w_down, layer, tp):
    counts = cnt[:N_BUCKETS, 0].astype(I32)
    ntile = (counts + TMM - 1) // TMM
    tend = jnp.cumsum(ntile)
    tstart = tend - ntile
    total = tend[-1]
    bucket, rank = ri[0].astype(I32), ri[1].astype(I32)
    pos = _lookup(tstart * TMM, bucket) + rank

    n_tiles = -(-tp // TMM) + N_BUCKETS
    j = jnp.arange(n_tiles, dtype=I32)
    blk = jnp.minimum(j, total - 1)
    tb = jnp.minimum(_count_le(tend, blk), N_BUCKETS - 1)
    grp = tb // N_PAIRS
    pair = tb % N_PAIRS
    ea = grp * EPG + _lookup(jnp.asarray(PAIR_LO, I32), pair)
    eb = grp * EPG + _lookup(jnp.asarray(PAIR_HI, I32), pair)
    nrow = jnp.clip(_lookup(counts, tb) - (j - _lookup(tstart, tb)) * TMM, 0, TMM)
    nrow = jnp.where(j < total, nrow, 0).astype(I32)
    ztile = jnp.where(ntile > 0, tend - 1, -1)

    xs = _scatter_tokens(xtt, pos, ztile, total, n_tiles)
    ys = _moe(xs, wrt, br, w_gate, w_up, w_down, layer, blk, ea, eb, nrow)
    return ys, pos.reshape(tp // TM, 1, TM)


GATHER_AHEAD = 2
GATHER_BUFS = GATHER_AHEAD + 1


def _gather_expert_rows(pos_refs, ys_ref, buf_ref, sem, split_queues):
    i = pl.program_id(0)
    n = pl.num_programs(0)

    def copies(idx_ref, s):
        return [pltpu.make_async_copy(_tok_rows(ys_ref, idx_ref[0, 0, k]),
                                      buf_ref.at[s, pl.ds(k * TOK_ROWS, TOK_ROWS), :], sem.at[s])
                for k in range(TM)]

    def start_all(cs):
        for k, c in enumerate(cs):
            c.start(priority=k % 2 if split_queues else 0)

    @pl.when(i == 0)
    def _():
        for d in range(GATHER_AHEAD):
            start_all(copies(pos_refs[d], d))

    start_all(copies(pos_refs[GATHER_AHEAD], (i + GATHER_AHEAD) % GATHER_BUFS))
    slot = i % GATHER_BUFS
    for c in copies(pos_refs[0], slot):
        c.wait()
    y = jnp.concatenate([buf_ref[slot, pl.ds(r, TM, stride=TOK_ROWS), :] for r in range(XCHUNKS)], axis=1)

    def drain():
        @pl.when(i == n - 1)
        def _():
            for d in range(1, GATHER_BUFS):
                for c in copies(pos_refs[0], (i + d) % GATHER_BUFS):
                    c.wait()

    return y, drain


def _gather_specs(n_steps):
    assert n_steps > GATHER_AHEAD
    ahead = lambda d: pl.BlockSpec((1, 1, TM), lambda i: (jnp.minimum(i + d, n_steps - 1), 0, 0),
                                   memory_space=pltpu.SMEM)
    return [ahead(d) for d in range(GATHER_BUFS)] + [pl.BlockSpec(memory_space=pl.ANY)]


GATHER_SCRATCH = [pltpu.VMEM((GATHER_BUFS, TM * TOK_ROWS, LANES), F32),
                  pltpu.SemaphoreType.DMA((GATHER_BUFS,))]


def _attn_proj_kernel(h_ref, pos0_ref, pos1_ref, pos2_ref, ys_ref, gkv_ref, gq_ref, wk_ref, wvt_ref,
                      wqt_ref, h2_ref, qt_ref, k_ref, vt_ref, buf_ref, sem):
    y, drain = _gather_expert_rows((pos0_ref, pos1_ref, pos2_ref), ys_ref, buf_ref, sem, False)
    h2 = h_ref[...] + y
    h2_ref[...] = h2
    ms = jnp.mean(h2 * h2, axis=-1, keepdims=True)
    xhat = h2 * lax.rsqrt(ms + RMS_EPS)
    xkv = (xhat * gkv_ref[...]).astype(BF16)
    xq = (xhat * gq_ref[...]).astype(BF16)
    nt = (((1,), (1,)), ((), ()))
    k_ref[...] = jnp.dot(xkv, wk_ref[...], preferred_element_type=F32).astype(BF16)
    vt_ref[...] = lax.dot_general(wvt_ref[...], xkv, nt, preferred_element_type=F32).astype(BF16)
    qt = lax.dot_general(wqt_ref[...], xq, nt, preferred_element_type=F32)
    qt_ref[...] = (qt * (LOG2E * HEAD_DIM ** -0.5)).astype(BF16)
    drain()


def _attn_proj(h, pos3, ys, gkv, gq, wk, wvt, wqt):
    tp = h.shape[0]
    steps = tp // TM
    row = pl.BlockSpec((TM, D), lambda i: (i, 0))
    col = pl.BlockSpec((D, TM), lambda i: (0, i))
    vec = pl.BlockSpec((1, D), lambda i: (0, 0))
    mat = pl.BlockSpec((D, D), lambda i: (0, 0))
    return pl.pallas_call(
        _attn_proj_kernel,
        grid=(steps,),
        in_specs=[row] + _gather_specs(steps) + [vec, vec, mat, mat, mat],
        out_specs=(row, col, row, col),
        out_shape=(jax.ShapeDtypeStruct((tp, D), F32), jax.ShapeDtypeStruct((D, tp), BF16),
                   jax.ShapeDtypeStruct((tp, D), BF16), jax.ShapeDtypeStruct((D, tp), BF16)),
        scratch_shapes=GATHER_SCRATCH,
        compiler_params=pltpu.CompilerParams(dimension_semantics=("arbitrary",),
                                             vmem_limit_bytes=VMEM_LIMIT),
        name="attn_proj",
    )(h, *([pos3] * GATHER_BUFS), ys, gkv, gq, wk, wvt, wqt)


def _attn_kernel(lambda_init, qt_ref, k_ref, km_ref, vt_ref, vtm_ref, lam_ref, sg_ref, o_ref,
                 s_ref, m_ref, l_ref, acc_ref):
    hw = 2 * HEAD_DIM
    n_x = o_ref.shape[0]
    half = QB // 2
    assert QB == KB and n_x % QB == 0 and N_META <= LANES
    lam = lam_ref[...]
    lam_full = (jnp.exp(jnp.sum(lam[0:1, :] * lam[1:2, :], axis=1, keepdims=True))
                - jnp.exp(jnp.sum(lam[2:3, :] * lam[3:4, :], axis=1, keepdims=True)) + lambda_init)

    diag = ((0, half, 0, QB), (half, half, half, half))
    qw = QB

    def blocks(r, fn):
        for kb in range(r):
            fn(kb * KB, KB, 0, qw, None)
        for ko, kw, qo, qn in diag:
            fn(r * QB + ko, kw, qo, qn, lambda krow, qcol, d=ko - qo: krow + d <= qcol)
        fn(0, LANES, 0, qw, lambda krow, qcol: krow < N_META, meta=True)

    def score_pass(r):
        s_par, m_par = s_ref.at[r % 2], m_ref.at[r % 2]
        qt = qt_ref[:, r * QB:(r + 1) * QB]
        frow = lax.broadcasted_iota(I32, (hw, qw), 0)
        zero = jnp.zeros_like(qt)
        qc = (jnp.where(frow < HEAD_DIM, qt, zero), jnp.where(frow >= HEAD_DIM, qt, zero))
        m_par[...] = jnp.full(m_par.shape, -jnp.inf, F32)

        def scores(k0, kw, qo, qn, visible, meta=False):
            kblk = km_ref[0:kw, :] if meta else k_ref[pl.ds(k0, kw), :]
            k0 = n_x if meta else k0
            for c in range(2):
                s = jnp.dot(kblk, qc[c][:, qo:qo + qn], preferred_element_type=F32)
                if visible is not None:
                    krow = lax.broadcasted_iota(I32, (kw, qn), 0)
                    qcol = lax.broadcasted_iota(I32, (kw, qn), 1)
                    s = jnp.where(visible(krow, qcol), s, -jnp.inf)
                s_par[c, pl.ds(k0, kw), qo:qo + qn] = s
                smax = jnp.max(s.reshape(kw // SUBLANES, SUBLANES, qn), axis=0)
                m_par[c, :, qo:qo + qn] = jnp.maximum(m_par[c, :, qo:qo + qn], smax)

        blocks(r, scores)

    def value_pass(r):
        s_par, m_par = s_ref.at[r % 2], m_ref.at[r % 2]
        m = [jnp.max(m_par[c], axis=0, keepdims=True) for c in range(2)]
        l_ref[...] = jnp.zeros(l_ref.shape, F32)
        acc_ref[...] = jnp.zeros(acc_ref.shape, F32)

        def weighted_values(k0, kw, qo, qn, visible, meta=False):
            del visible
            vblk = vtm_ref[:, 0:kw] if meta else vt_ref[:, pl.ds(k0, kw)]
            k0 = n_x if meta else k0
            for c in range(2):
                p = jnp.exp2(s_par[c, pl.ds(k0, kw), qo:qo + qn] - m[c][:, qo:qo + qn])
                acc_ref[c, :, qo:qo + qn] += jnp.dot(vblk, p.astype(BF16), preferred_element_type=F32)
                l_ref[c, :, qo:qo + qn] += jnp.sum(p.reshape(kw // SUBLANES, SUBLANES, qn), axis=0)

        blocks(r, weighted_values)
        l = [jnp.sum(l_ref[c], axis=0, keepdims=True) for c in range(2)]
        ot = acc_ref[0] / l[0] - lam_full * (acc_ref[1] / l[1])
        ms = jnp.mean(ot * ot, axis=0, keepdims=True)
        y = ot * lax.rsqrt(ms + SUBLN_EPS) * sg_ref[...] * (1.0 - lambda_init)
        o_ref[r * QB:(r + 1) * QB, :] = y.T.astype(BF16)

    n_blocks = n_x // QB
    score_pass(0)
    for r in range(n_blocks):
        if r + 1 < n_blocks:
            score_pass(r + 1)
        value_pass(r)


def _attention(qt, k, vt, lam, sg, lambda_init, bsz, n_x):
    hw = 2 * HEAD_DIM
    meta_blk = bsz * n_x // TM
    fmaj = pl.BlockSpec((hw, n_x), lambda b, h: (h, b))
    tmaj = pl.BlockSpec((n_x, hw), lambda b, h: (b, h))
    return pl.pallas_call(
        functools.partial(_attn_kernel, lambda_init),
        grid=(bsz, N_HEADS),
        in_specs=[fmaj, tmaj, pl.BlockSpec((TM, hw), lambda b, h: (meta_blk, h)),
                  fmaj, pl.BlockSpec((hw, TM), lambda b, h: (h, meta_blk)),
                  pl.BlockSpec((4, HEAD_DIM), lambda b, h: (0, 0)),
                  pl.BlockSpec((hw, 1), lambda b, h: (0, 0))],
        out_specs=tmaj,
        out_shape=jax.ShapeDtypeStruct((bsz * n_x, D), BF16),
        scratch_shapes=[pltpu.VMEM((2, 2, n_x + LANES, QB), F32), pltpu.VMEM((2, 2, SUBLANES, QB), F32),
                        pltpu.VMEM((2, SUBLANES, QB), F32), pltpu.VMEM((2, hw, QB), F32)],
        compiler_params=pltpu.CompilerParams(dimension_semantics=("arbitrary",) * 2,
                                             vmem_limit_bytes=VMEM_LIMIT),
        name="diff_attention",
    )(qt, k, k, vt, vt, lam, sg)


def _attn_out_kernel(h_ref, o_ref, wo_ref, fg_ref, wrt_ref, br_ref, tri_ref,
                     h3_ref, xtt_ref, ri_ref, cnt_ref, run_ref):
    @pl.when(pl.program_id(0) == 0)
    def _():
        run_ref[...] = jnp.zeros_like(run_ref)

    h3 = h_ref[...] + jnp.dot(o_ref[...], wo_ref[...], preferred_element_type=F32)
    h3_ref[...] = h3
    _route_tail(h3, fg_ref, wrt_ref, br_ref, tri_ref, xtt_ref, ri_ref, cnt_ref, run_ref)


def _attn_out(h, o, wo, fg, wrt, br, tri):
    tp = h.shape[0]
    last_o = o.shape[0] // TM - 1
    row = pl.BlockSpec((TM, D), lambda i: (i, 0))
    return pl.pallas_call(
        _attn_out_kernel,
        grid=(tp // TM,),
        in_specs=[row, pl.BlockSpec((TM, D), lambda i: (jnp.minimum(i, last_o), 0)),
                  pl.BlockSpec((D, D), lambda i: (0, 0))] + _route_in_specs(),
        out_specs=(row,) + _route_out_specs(),
        out_shape=(jax.ShapeDtypeStruct((tp, D), F32),) + _route_out_shapes(tp),
        scratch_shapes=[pltpu.VMEM((BUCKET_ROWS, LANES), F32)],
        compiler_params=pltpu.CompilerParams(dimension_semantics=("arbitrary",),
                                             vmem_limit_bytes=VMEM_LIMIT),
        name="attn_out",
    )(h, o, wo, fg, wrt, br, tri)


def _final_kernel(h_ref, pos0_ref, pos1_ref, pos2_ref, ys_ref, g_ref, o_ref, buf_ref, sem):
    y, drain = _gather_expert_rows((pos0_ref, pos1_ref, pos2_ref), ys_ref, buf_ref, sem, True)
    o_ref[...] = _rms(h_ref[...] + y, g_ref[...], RMS_EPS)
    drain()


def _final(h, pos3, ys, g, n_x_tiles):
    row = pl.BlockSpec((TM, D), lambda i: (i, 0))
    return pl.pallas_call(
        _final_kernel,
        grid=(n_x_tiles,),
        in_specs=[row] + _gather_specs(n_x_tiles) + [pl.BlockSpec((1, D), lambda i: (0, 0))],
        out_specs=row,
        out_shape=jax.ShapeDtypeStruct((n_x_tiles * TM, D), F32),
        scratch_shapes=GATHER_SCRATCH,
        compiler_params=pltpu.CompilerParams(dimension_semantics=("arbitrary",),
                                             vmem_limit_bytes=VMEM_LIMIT),
        name="final_norm",
    )(h, *([pos3] * GATHER_BUFS), ys, g)


def _router_params(w_rg, b_rg, w_re, b_re):
    wr = jnp.concatenate([w_rg, w_re], axis=1)
    wrt = jnp.zeros((BUCKET_ROWS, D), F32).at[:N_GROUPS + N_EXPERTS].set(wr.T).astype(BF16)
    br = jnp.zeros((BUCKET_ROWS, 1), F32).at[:N_GROUPS + N_EXPERTS, 0].set(jnp.concatenate([b_rg, b_re]))
    return wrt, br


def kernel(x, meta_tokens, a_norm, a_w_in, a_conv, a_w_out, kv_norm, w_kv, b_norm, b_w_q, b_lambda, b_subln, b_w_o, ffn_norm, router_group_w, router_group_b, router_expert_w, router_expert_b, expert_w_gate, expert_w_up, expert_w_down, final_norm):
    bsz, seq, d = x.shape
    assert d == D and a_norm.shape[0] == 1 and b_norm.shape[0] == 1
    assert meta_tokens.shape[0] == N_META and seq % TM == 0
    x_tiles = seq // TM
    n_x_tiles = bsz * x_tiles
    tp = (n_x_tiles + 1) * TM
    assert tp % TPERM == 0

    metapad = jnp.concatenate([meta_tokens.astype(x.dtype), jnp.zeros((TM - N_META, D), x.dtype)])
    tri = jnp.triu(jnp.ones((TM, TM), F32)).astype(BF16)
    bf = lambda w: w.astype(BF16)

    wrt, br = _router_params(router_group_w[0], router_group_b[0], router_expert_w[0], router_expert_b[0])
    h, xtt, ri, cnt = _mixer_a(x.reshape(bsz * seq, D), metapad, a_norm[0][None], bf(a_w_in[0]), a_conv[0],
                               bf(a_w_out[0]), ffn_norm[0][None], wrt, br, tri, x_tiles)
    ys, pos3 = _moe_layer(xtt, ri, cnt, wrt, br, expert_w_gate, expert_w_up, expert_w_down, 0, tp)

    h, qt, k, vt = _attn_proj(h, pos3, ys, kv_norm[None], b_norm[0][None],
                              bf(w_kv[:, :D]), bf(w_kv[:, D:].T), bf(b_w_q[0].T))
    o = _attention(qt, k, vt, b_lambda[0], b_subln[0][:, None], _lambda_init(1), bsz, seq)
    wrt, br = _router_params(router_group_w[1], router_group_b[1], router_expert_w[1], router_expert_b[1])
    h, xtt, ri, cnt = _attn_out(h, o, bf(b_w_o[0]), ffn_norm[1][None], wrt, br, tri)
    ys, pos3 = _moe_layer(xtt, ri, cnt, wrt, br, expert_w_gate, expert_w_up, expert_w_down, 1, tp)

    out = _final(h, pos3, ys, final_norm[None], n_x_tiles)
    return out.reshape(bsz, seq, D)
```
